```python
import math
import jax, jax.numpy as jnp
from jax import lax
import numpy as np

D_MODEL = 1024
BATCH = 2
SEQ = 8192
DEPTH = 2

GRID_W = 64
CTX_LEN = 256
HEAD_DIM = 64
N_BRANCH = 4
BRANCH_W = 256
NA_HEADS = 4
WIN_H = 8
WIN_W = 16
DIFF_HEADS = 4
DIFF_QK_DIM = 32
DIFF_V_DIM = 64
SGU_GROUPS = 4
SGU_CHUNK = 128
GQA_Q_HEADS = 4
GQA_KV_HEADS = 2
GQA_GROUP = GQA_Q_HEADS // GQA_KV_HEADS
ROPE_THETA = 10000.0
Q_BLOCK = 128
N_EXPERTS = 16
EXPERT_HIDDEN = 2048
CAPACITY_FACTOR = 2
EPS = 1e-6

NA_W = NA_HEADS * HEAD_DIM
DIFF_QK_W = DIFF_HEADS * 2 * DIFF_QK_DIM
DIFF_V_W = DIFF_HEADS * DIFF_V_DIM
GQA_Q_W = GQA_Q_HEADS * HEAD_DIM
GQA_KV_W = GQA_KV_HEADS * HEAD_DIM
GATE_W = N_BRANCH * D_MODEL
MAIN_W = NA_W + DIFF_QK_W + GQA_Q_W + 2 * BRANCH_W + GATE_W
KV_W = 2 * NA_W + DIFF_QK_W + DIFF_V_W + 2 * GQA_KV_W
IN_W = MAIN_W + KV_W

kernel_name = 'hybrid_parallel_mixers_ec_moe_dit'


def _split(t, sizes):
    out, start = [], 0
    for n in sizes:
        out.append(t[..., start:start + n])
        start += n
    return out


def rms_norm(x, g):
    xf = x.astype(jnp.float32)
    y = xf * lax.rsqrt(jnp.mean(xf * xf, axis=-1, keepdims=True) + EPS)
    return (y * g.astype(jnp.float32)).astype(x.dtype)


def layer_norm(x, g):
    xf = x.astype(jnp.float32)
    mu = jnp.mean(xf, axis=-1, keepdims=True)
    var = jnp.mean(jnp.square(xf - mu), axis=-1, keepdims=True)
    return ((xf - mu) * lax.rsqrt(var + EPS) * g.astype(jnp.float32)).astype(x.dtype)


def modulate(h, shift, scale):
    return h * (1 + scale) + shift


def to_heads(t, n):
    b, s, _ = t.shape
    return t.reshape(b, s, n, -1).transpose(0, 2, 1, 3)


def from_heads(t):
    b, h, s, d = t.shape
    return t.transpose(0, 2, 1, 3).reshape(b, s, h * d)


def to_diff_heads(t):
    b, s, _ = t.shape
    return t.reshape(b, s, DIFF_HEADS, 2, DIFF_QK_DIM).transpose(0, 2, 3, 1, 4)


def axial_rope_tables(n, dim):
    t = jnp.arange(n)
    row = (t // GRID_W).astype(jnp.float32)
    col = (t % GRID_W).astype(jnp.float32)
    n_pairs = dim // 4
    inv = ROPE_THETA ** (-jnp.arange(n_pairs, dtype=jnp.float32) / n_pairs)
    ang = jnp.concatenate([row[:, None] * inv, col[:, None] * inv], axis=-1)
    return (jnp.cos(ang), jnp.sin(ang))


def apply_rope(x, rope):
    cos, sin = rope
    xf = x.astype(jnp.float32).reshape(x.shape[:-1] + (x.shape[-1] // 2, 2))
    x0, x1 = xf[..., 0], xf[..., 1]
    out = jnp.stack([x0 * cos - x1 * sin, x0 * sin + x1 * cos], axis=-1)
    return out.reshape(x.shape).astype(x.dtype)


def sweep_query_blocks(fn, q):
    s = q.shape[-2]
    nb = s // Q_BLOCK
    qb = jnp.moveaxis(q.reshape(q.shape[:-2] + (nb, Q_BLOCK, q.shape[-1])), -3, 0)
    out = lax.map(fn, qb)
    out = jnp.moveaxis(out, 0, -3)
    return out.reshape(out.shape[:-3] + (s, out.shape[-1]))


def gqa_attention(q, k, v):
    scale = q.shape[-1] ** -0.5

    def block(qb):
        sc = jnp.einsum('bkgqd,bksd->bkgqs', qb, k).astype(jnp.float32) * scale
        p = jax.nn.softmax(sc, axis=-1).astype(v.dtype)
        return jnp.einsum('bkgqs,bksd->bkgqd', p, v)

    return sweep_query_blocks(block, q)


def diff_attention(q, k, v, lam):
    scale = q.shape[-1] ** -0.5

    def block(qb):
        sc = jnp.einsum('bhmqd,bhmsd->bhmqs', qb, k).astype(jnp.float32) * scale
        p = jax.nn.softmax(sc, axis=-1)
        a = p[:, :, 0] - lam * p[:, :, 1]
        return jnp.einsum('bhqs,bhsd->bhqd', a.astype(v.dtype), v)

    return sweep_query_blocks(block, q)


def neighbourhood_attention(q, k, v, kc, vc, rpb):
    b, h, s, d = q.shape
    n_ctx = kc.shape[2]
    rows = s // GRID_W
    kh, kw = min(WIN_H, rows), WIN_W
    scale = d ** -0.5
    qg = q.reshape(b, h, rows, GRID_W, d)
    kg = k.reshape(b, h, rows, GRID_W, d)
    vg = v.reshape(b, h, rows, GRID_W, d)
    cols = jnp.arange(GRID_W)
    c0 = jnp.clip(cols - kw // 2, 0, GRID_W - kw)
    cidx = c0[:, None] + jnp.arange(kw)[None, :]
    col_bias = rpb[:, :, cidx - cols[:, None] + WIN_W - 1]

    def row_block(args):
        r, q_row = args
        r0 = jnp.clip(r - kh // 2, 0, rows - kh)
        k_rows = lax.dynamic_slice_in_dim(kg, r0, kh, axis=2)
        v_rows = lax.dynamic_slice_in_dim(vg, r0, kh, axis=2)
        k_win = k_rows[:, :, :, cidx]
        v_win = v_rows[:, :, :, cidx]
        bias = jnp.take(col_bias, r0 + jnp.arange(kh) - r + WIN_H - 1, axis=1)
        s_loc = (jnp.einsum('bhqd,bhiqjd->bhqij', q_row, k_win).astype(jnp.float32) * scale
                 + bias.transpose(0, 2, 1, 3)[None].astype(jnp.float32))
        s_ctx = jnp.einsum('bhqd,bhld->bhql', q_row, kc).astype(jnp.float32) * scale
        p = jax.nn.softmax(jnp.concatenate([s_ctx, s_loc.reshape(b, h, GRID_W, kh * kw)], axis=-1), axis=-1)
        p = p.astype(v.dtype)
        p_ctx = p[..., :n_ctx]
        p_loc = p[..., n_ctx:].reshape(b, h, GRID_W, kh, kw)
        return (jnp.einsum('bhql,bhld->bhqd', p_ctx, vc)
                + jnp.einsum('bhqij,bhiqjd->bhqd', p_loc, v_win))

    out = lax.map(row_block, (jnp.arange(rows), jnp.moveaxis(qg, 2, 0)))
    return jnp.moveaxis(out, 0, 2).reshape(b, h, s, d)


def spatial_gating(u, v, norm_g, w_s, b_s):
    b, s, w = v.shape
    n = s // SGU_CHUNK
    vn = layer_norm(v, norm_g).reshape(b, n, SGU_CHUNK, SGU_GROUPS, w // SGU_GROUPS)
    mixed = jnp.einsum('gpq,bnqgc->bnpgc', w_s, vn) + b_s.T[:, :, None]
    return u * mixed.reshape(b, s, w)


def expert_choice_moe(h, w_router, w_gate, w_up, w_down):
    b, n, d = h.shape
    cap = CAPACITY_FACTOR * n // N_EXPERTS
    aff = jax.nn.softmax((h @ w_router).astype(jnp.float32), axis=-1)
    score, idx = lax.top_k(jnp.swapaxes(aff, 1, 2), cap)
    xin = jax.vmap(lambda hb, ib: hb[ib])(h, idx)
    hid = (jax.nn.silu(jnp.einsum('becd,edf->becf', xin, w_gate))
           * jnp.einsum('becd,edf->becf', xin, w_up))
    y = jnp.einsum('becf,efd->becd', hid, w_down) * score[..., None].astype(h.dtype)
    return jax.vmap(lambda yb, ib: jnp.zeros((n, d), yb.dtype).at[ib.reshape(-1)].add(yb.reshape(-1, d)))(y, idx)


def kv_heads(p_kv, k_gain, ropes):
    ka, va, kb, vb, kd, vd = _split(p_kv, (NA_W, NA_W, DIFF_QK_W, DIFF_V_W, GQA_KV_W, GQA_KV_W))
    kb = to_diff_heads(kb)
    kd = rms_norm(to_heads(kd, GQA_KV_HEADS), k_gain)
    if ropes is not None:
        kb = apply_rope(kb, ropes[0])
        kd = apply_rope(kd, ropes[1])
    return (to_heads(ka, NA_HEADS), to_heads(va, NA_HEADS), kb, to_heads(vb, DIFF_HEADS),
            kd, to_heads(vd, GQA_KV_HEADS))


def token_mixer(p_main, kv, kv_ctx, ropes, q_gain, rpb, lam, lam_init, sub_g, sgu_g, sgu_w, sgu_b,
                gate_b, w_branch, w_out):
    b, s, _ = p_main.shape
    qa, qb, qd, u, v, gate_logits = _split(p_main, (NA_W, DIFF_QK_W, GQA_Q_W, BRANCH_W, BRANCH_W, GATE_W))
    ka, va, kb, vb, kd, vd = kv
    qa = to_heads(qa, NA_HEADS)
    qb = to_diff_heads(qb)
    qd = rms_norm(to_heads(qd, GQA_Q_HEADS), q_gain)
    if kv_ctx is None:
        y_a = gqa_attention(qa[:, :, None], ka, va)[:, :, 0]
    else:
        qb = apply_rope(qb, ropes[0])
        qd = apply_rope(qd, ropes[1])
        kac, vac, kbc, vbc, kdc, vdc = kv_ctx
        y_a = neighbourhood_attention(qa, ka, va, kac, vac, rpb)
        kb = jnp.concatenate([kbc, kb], axis=3)
        vb = jnp.concatenate([vbc, vb], axis=2)
        kd = jnp.concatenate([kdc, kd], axis=2)
        vd = jnp.concatenate([vdc, vd], axis=2)
    y_b = rms_norm(diff_attention(qb, kb, vb, lam), sub_g) * (1.0 - lam_init)
    y_c = spatial_gating(jax.nn.gelu(u), jax.nn.gelu(v), sgu_g, sgu_w, sgu_b)
    y_d = gqa_attention(qd.reshape(b, GQA_KV_HEADS, GQA_GROUP, s, HEAD_DIM), kd, vd)
    y_d = y_d.reshape(b, GQA_Q_HEADS, s, HEAD_DIM)
    ys = jnp.stack([from_heads(y_a), from_heads(y_b), y_c, from_heads(y_d)], axis=2)
    proj = jnp.einsum('bsnc,ncd->bsnd', ys, w_branch)
    gates = jax.nn.sigmoid((gate_logits + gate_b).astype(jnp.float32)).astype(proj.dtype)
    gates = gates.reshape(b, s, N_BRANCH, D_MODEL)
    return jnp.einsum('bsnd,bsnd->bsd', gates, proj) @ w_out


def setup_inputs(seed: int = 0) -> dict:
    key = jax.random.key(seed)
    ks = jax.random.split(key, 28)
    L, D = DEPTH, D_MODEL

    def nrm(k, shape, s):
        return jax.random.normal(k, shape, jnp.float32) * s

    return {
        'x': nrm(ks[0], (BATCH, SEQ, D), 1.0),
        'c': nrm(ks[1], (BATCH, D), 1.0),
        'ctx': nrm(ks[2], (BATCH, CTX_LEN, D), 1.0),
        'c_ctx': nrm(ks[3], (D,), 1.0),
        'w_mod': nrm(ks[4], (L, D, 6 * D), 0.5 * D ** -0.5),
        'b_mod': nrm(ks[5], (L, 6 * D), 0.02),
        'norm1_g': 1.0 + nrm(ks[6], (L, D), 0.02),
        'norm2_g': 1.0 + nrm(ks[7], (L, D), 0.02),
        'w_in': nrm(ks[8], (L, D, IN_W), D ** -0.5),
        'q_gain': 1.0 + nrm(ks[9], (L, HEAD_DIM), 0.02),
        'k_gain': 1.0 + nrm(ks[10], (L, HEAD_DIM), 0.02),
        'na_rpb': nrm(ks[11], (L, NA_HEADS, 2 * WIN_H - 1, 2 * WIN_W - 1), 0.1),
        'lambda_q1': nrm(ks[12], (L, DIFF_QK_DIM), 0.1),
        'lambda_k1': nrm(ks[13], (L, DIFF_QK_DIM), 0.1),
        'lambda_q2': nrm(ks[14], (L, DIFF_QK_DIM), 0.1),
        'lambda_k2': nrm(ks[15], (L, DIFF_QK_DIM), 0.1),
        'diff_sub_g': 1.0 + nrm(ks[16], (L, DIFF_V_DIM), 0.02),
        'sgu_norm_g': 1.0 + nrm(ks[17], (L, BRANCH_W), 0.02),
        'sgu_w': nrm(ks[18], (L, SGU_GROUPS, SGU_CHUNK, SGU_CHUNK), SGU_CHUNK ** -0.5),
        'sgu_b': 1.0 + nrm(ks[19], (L, SGU_GROUPS, SGU_CHUNK), 0.02),
        'gate_b': nrm(ks[20], (L, GATE_W), 0.02),
        'w_branch': nrm(ks[21], (L, N_BRANCH, BRANCH_W, D), BRANCH_W ** -0.5),
        'w_out': nrm(ks[22], (L, D, D), D ** -0.5),
        'w_router': nrm(ks[23], (L, D, N_EXPERTS), D ** -0.5),
        'w_e_gate': nrm(ks[24], (L, N_EXPERTS, D, EXPERT_HIDDEN), D ** -0.5),
        'w_e_up': nrm(ks[25], (L, N_EXPERTS, D, EXPERT_HIDDEN), D ** -0.5),
        'w_e_down': nrm(ks[26], (L, N_EXPERTS, EXPERT_HIDDEN, D), EXPERT_HIDDEN ** -0.5),
        'final_g': 1.0 + nrm(ks[27], (D,), 0.02),
    }


def reference(x, c, ctx, c_ctx, w_mod, b_mod, norm1_g, norm2_g, w_in, q_gain, k_gain, na_rpb,
              lambda_q1, lambda_k1, lambda_q2, lambda_k2, diff_sub_g, sgu_norm_g, sgu_w, sgu_b, gate_b,
              w_branch, w_out, w_router, w_e_gate, w_e_up, w_e_down, final_g):
    s = x.shape[1]
    ropes = (axial_rope_tables(s, DIFF_QK_DIM), axial_rope_tables(s, HEAD_DIM))
    for l in range(DEPTH):
        last = l == DEPTH - 1
        mod = jax.nn.silu(c) @ w_mod[l] + b_mod[l]
        sh1, sc1, g1, sh2, sc2, g2 = jnp.split(mod[:, None, :], 6, axis=-1)
        mod_c = jax.nn.silu(c_ctx) @ w_mod[l] + b_mod[l]
        csh1, csc1, cg1, csh2, csc2, cg2 = jnp.split(mod_c, 6, axis=-1)
        lam_init = 0.8 - 0.6 * math.exp(-0.3 * l)
        lam = (jnp.exp(jnp.sum(lambda_q1[l].astype(jnp.float32) * lambda_k1[l].astype(jnp.float32)))
               - jnp.exp(jnp.sum(lambda_q2[l].astype(jnp.float32) * lambda_k2[l].astype(jnp.float32)))
               + lam_init)

        h = modulate(rms_norm(x, norm1_g[l]), sh1, sc1)
        hc = modulate(rms_norm(ctx, norm1_g[l]), csh1, csc1)
        p = h @ w_in[l]
        kv = kv_heads(p[..., MAIN_W:], k_gain[l], ropes)
        if last:
            kv_c = kv_heads(hc @ w_in[l][:, MAIN_W:], k_gain[l], None)
        else:
            pc = hc @ w_in[l]
            kv_c = kv_heads(pc[..., MAIN_W:], k_gain[l], None)
            ctx_mix = token_mixer(pc[..., :MAIN_W], kv_c, None, None, q_gain[l], na_rpb[l], lam, lam_init,
                                  diff_sub_g[l], sgu_norm_g[l], sgu_w[l], sgu_b[l], gate_b[l],
                                  w_branch[l], w_out[l])
        lat_mix = token_mixer(p[..., :MAIN_W], kv, kv_c, ropes, q_gain[l], na_rpb[l], lam, lam_init,
                              diff_sub_g[l], sgu_norm_g[l], sgu_w[l], sgu_b[l], gate_b[l],
                              w_branch[l], w_out[l])
        x = x + g1 * lat_mix

        h2 = modulate(rms_norm(x, norm2_g[l]), sh2, sc2)
        x = x + g2 * expert_choice_moe(h2, w_router[l], w_e_gate[l], w_e_up[l], w_e_down[l])
        if not last:
            ctx = ctx + cg1 * ctx_mix
            hc2 = modulate(rms_norm(ctx, norm2_g[l]), csh2, csc2)
            ctx = ctx + cg2 * expert_choice_moe(hc2, w_router[l], w_e_gate[l], w_e_up[l], w_e_down[l])
    return rms_norm(x, final_g)
```

```python
import functools
import math

import jax
import jax.numpy as jnp
from jax import lax
from jax.experimental import pallas as pl
from jax.experimental.pallas import tpu as pltpu

F32 = jnp.float32
BF16 = jnp.bfloat16
I32 = jnp.int32

GRID_W = 64
HEAD_DIM = 64
N_BRANCH = 4
BRANCH_W = 256
NA_HEADS = 4
WIN_H = 8
WIN_W = 16
DIFF_HEADS = 4
DIFF_QK_DIM = 32
DIFF_V_DIM = 64
SGU_GROUPS = 4
SGU_CHUNK = 128
GQA_Q_HEADS = 4
GQA_KV_HEADS = 2
ROPE_THETA = 10000.0
N_EXPERTS = 16
CAPACITY_FACTOR = 2
EPS = 1e-6
NEG = -1e30

GATE_W = N_BRANCH * 1024
COL_BLK = 256
SEG_QA, SEG_QB, SEG_QD, SEG_U, SEG_V, SEG_KA, SEG_VA, SEG_KB, SEG_VB = range(16, 25)
COL_KD = GATE_W + 9 * COL_BLK
KV_COL0 = GATE_W + 5 * COL_BLK

VMEM_LIMIT = 56 * 1024 * 1024

NBR_ROWS = 8
NBR_KROWS = 16
MOE_CHUNK = 256


def _cparams(sem):
    return pltpu.CompilerParams(dimension_semantics=sem, vmem_limit_bytes=VMEM_LIMIT)


def _dot_nt(a, b):
    return lax.dot_general(a, b, (((1,), (1,)), ((), ())), preferred_element_type=F32)


def _inproj_kernel(x_ref, mult_ref, shift_ref, w_ref, bias_ref, o_ref, h_scr, *, n_gate_tiles):
    j = pl.program_id(2)

    @pl.when(j == 0)
    def _():
        x = x_ref[0]
        ms = jnp.mean(x * x, axis=-1, keepdims=True)
        h = x * lax.rsqrt(ms + EPS) * mult_ref[0] + shift_ref[0]
        h_scr[...] = h.astype(BF16)

    acc = jnp.dot(h_scr[...], w_ref[...], preferred_element_type=F32) + bias_ref[...]

    @pl.when(j < n_gate_tiles)
    def _():
        o_ref[0] = (1.0 / (1.0 + jnp.exp(-acc))).astype(o_ref.dtype)

    @pl.when(j >= n_gate_tiles)
    def _():
        o_ref[0] = acc.astype(o_ref.dtype)


def _inproj(x, mult, shift, w, bias, n_gate_cols, tm, tn):
    B, S, D = x.shape
    N = w.shape[1]
    tm = min(tm, S)
    return pl.pallas_call(
        functools.partial(_inproj_kernel, n_gate_tiles=n_gate_cols // tn),
        grid=(B, S // tm, N // tn),
        in_specs=[
            pl.BlockSpec((1, tm, D), lambda b, i, j: (b, i, 0)),
            pl.BlockSpec((1, 1, D), lambda b, i, j: (b, 0, 0)),
            pl.BlockSpec((1, 1, D), lambda b, i, j: (b, 0, 0)),
            pl.BlockSpec((D, tn), lambda b, i, j: (0, j)),
            pl.BlockSpec((1, tn), lambda b, i, j: (0, j)),
        ],
        out_specs=pl.BlockSpec((1, tm, tn), lambda b, i, j: (b, i, j)),
        out_shape=jax.ShapeDtypeStruct((B, S, N), BF16),
        scratch_shapes=[pltpu.VMEM((tm, D), BF16)],
        compiler_params=_cparams(("parallel", "parallel", "arbitrary")),
        name="inproj",
    )(x, mult, shift, w, bias)


def _flash_kernel(*refs, nseg, tks):
    q_ref = refs[0]
    k_refs = refs[1:1 + nseg]
    v_refs = refs[1 + nseg:1 + 2 * nseg]
    o_ref = refs[1 + 2 * nseg]
    m_scr, l_scr, acc_scr = refs[2 + 2 * nseg:]
    q = q_ref[0, 0]
    m_scr[...] = jnp.full(m_scr.shape, NEG, F32)
    l_scr[...] = jnp.zeros(l_scr.shape, F32)
    acc_scr[...] = jnp.zeros(acc_scr.shape, F32)
    for k_ref, v_ref, tk in zip(k_refs, v_refs, tks):
        def body(c, carry, k_ref=k_ref, v_ref=v_ref, tk=tk):
            start = pl.multiple_of(c * tk, tk)
            kc = k_ref[0, 0, pl.ds(start, tk), :]
            vc = v_ref[0, 0, pl.ds(start, tk), :]
            s = _dot_nt(q, kc)
            m_prev = m_scr[...]
            m_new = jnp.maximum(m_prev, jnp.max(s, axis=1, keepdims=True))
            alpha = jnp.exp(m_prev - m_new)
            p = jnp.exp(s - m_new)
            l_scr[...] = alpha * l_scr[...] + jnp.sum(p, axis=1, keepdims=True)
            acc_scr[...] = alpha * acc_scr[...] + jnp.dot(p.astype(BF16), vc, preferred_element_type=F32)
            m_scr[...] = m_new
            return carry

        lax.fori_loop(0, k_ref.shape[2] // tk, body, 0)
    o_ref[0, 0] = (acc_scr[...] / l_scr[...]).astype(o_ref.dtype)


def _flash(q, ks, vs, gk, gv, tq, tk):
    B, Hq, Sq, d = q.shape
    dv = vs[0].shape[-1]
    tq = min(tq, Sq)
    tks = tuple(min(tk, k.shape[2]) for k in ks)
    in_specs = [pl.BlockSpec((1, 1, tq, d), lambda b, h, i: (b, h, i, 0))]
    for k in ks:
        in_specs.append(pl.BlockSpec((1, 1, k.shape[2], d), lambda b, h, i: (b, h // gk, 0, 0)))
    for v in vs:
        in_specs.append(pl.BlockSpec((1, 1, v.shape[2], dv), lambda b, h, i: (b, h // gv, 0, 0)))
    return pl.pallas_call(
        functools.partial(_flash_kernel, nseg=len(ks), tks=tks),
        grid=(B, Hq, Sq // tq),
        in_specs=in_specs,
        out_specs=pl.BlockSpec((1, 1, tq, dv), lambda b, h, i: (b, h, i, 0)),
        out_shape=jax.ShapeDtypeStruct((B, Hq, Sq, dv), F32),
        scratch_shapes=[pltpu.VMEM((tq, 1), F32), pltpu.VMEM((tq, 1), F32), pltpu.VMEM((tq, dv), F32)],
        compiler_params=_cparams(("parallel", "parallel", "arbitrary")),
        name="flash",
    )(q, *ks, *vs)


def _nbr_kernel(q_ref, k0, k1, k2, k3, v0, v1, v2, v3, kc_ref, vc_ref, bias_ref, o_ref):
    q = q_ref[0, 0]
    k = jnp.concatenate([k0[0, 0], k1[0, 0], k2[0, 0], k3[0, 0]], axis=0)
    v = jnp.concatenate([v0[0, 0], v1[0, 0], v2[0, 0], v3[0, 0]], axis=0)
    s_loc = _dot_nt(q, k) + bias_ref[0, 0]
    s_ctx = _dot_nt(q, kc_ref[0, 0])
    m = jnp.maximum(jnp.max(s_loc, axis=1, keepdims=True), jnp.max(s_ctx, axis=1, keepdims=True))
    p_loc = jnp.exp(s_loc - m)
    p_ctx = jnp.exp(s_ctx - m)
    l = jnp.sum(p_loc, axis=1, keepdims=True) + jnp.sum(p_ctx, axis=1, keepdims=True)
    o = (jnp.dot(p_ctx.astype(BF16), vc_ref[0, 0], preferred_element_type=F32)
         + jnp.dot(p_loc.astype(BF16), v, preferred_element_type=F32))
    o_ref[0, 0] = (o / l).astype(o_ref.dtype)


def _nbr_bias(rpb, rows):
    kh, kw = WIN_H, WIN_W

    def variant(r_start, k_start):
        j = jnp.arange(NBR_ROWS)[:, None, None, None]
        qc = jnp.arange(GRID_W)[None, :, None, None]
        kr = k_start + jnp.arange(NBR_KROWS)[None, None, :, None]
        kc = jnp.arange(GRID_W)[None, None, None, :]
        r = r_start + j
        r0 = jnp.clip(r - kh // 2, 0, rows - kh)
        c0 = jnp.clip(qc - kw // 2, 0, GRID_W - kw)
        valid = (kr >= r0) & (kr < r0 + kh) & (kc >= c0) & (kc < c0 + kw)
        di = jnp.clip(kr - r + WIN_H - 1, 0, 2 * WIN_H - 2)
        dj = jnp.clip(kc - qc + WIN_W - 1, 0, 2 * WIN_W - 2)
        b = rpb.astype(F32)[:, di, dj]
        b = jnp.where(valid[None], b, NEG)
        return b.reshape(rpb.shape[0], NBR_ROWS * GRID_W, NBR_KROWS * GRID_W)

    return jnp.stack([variant(0, 0), variant(NBR_ROWS, NBR_ROWS - kh // 2),
                      variant(rows - NBR_ROWS, rows - NBR_KROWS)])


def _nbr(q, k, v, kc, vc, bias):
    B, H, S, d = q.shape
    L = kc.shape[2]
    tq = NBR_ROWS * GRID_W
    kb = tq // 2
    nb = S // tq
    assert S % tq == 0 and nb >= 3

    def kmap(j):
        return lambda h, i, b: (b, h, jnp.clip(2 * i - 1, 0, 2 * nb - 4) + j, 0)

    def bmap(h, i, b):
        return (jnp.where(i == 0, 0, jnp.where(i == nb - 1, 2, 1)), h, 0, 0)

    kv_specs = [pl.BlockSpec((1, 1, kb, d), kmap(j)) for j in range(4)]
    return pl.pallas_call(
        _nbr_kernel,
        grid=(H, nb, B),
        in_specs=[pl.BlockSpec((1, 1, tq, d), lambda h, i, b: (b, h, i, 0))] + kv_specs + kv_specs + [
            pl.BlockSpec((1, 1, L, d), lambda h, i, b: (b, h, 0, 0)),
            pl.BlockSpec((1, 1, L, d), lambda h, i, b: (b, h, 0, 0)),
            pl.BlockSpec((1, 1, tq, NBR_KROWS * GRID_W), bmap),
        ],
        out_specs=pl.BlockSpec((1, 1, tq, d), lambda h, i, b: (b, h, i, 0)),
        out_shape=jax.ShapeDtypeStruct((B, H, S, d), F32),
        compiler_params=_cparams(("parallel", "parallel", "parallel")),
        name="nbr_attn",
    )(q, k, k, k, k, v, v, v, v, kc, vc, bias)


def _gelu(x):
    return 0.5 * x * (1.0 + jnp.tanh(math.sqrt(2.0 / math.pi) * (x + 0.044715 * (x * x * x))))


def _sgu_kernel(u_ref, v_ref, g_ref, w_ref, b_ref, o_ref, *, nchunk):
    grp = lax.broadcasted_iota(I32, (SGU_CHUNK, BRANCH_W), 1) // (BRANCH_W // SGU_GROUPS)
    for c in range(nchunk):
        rows = slice(c * SGU_CHUNK, (c + 1) * SGU_CHUNK)
        u = u_ref[0, rows, :].astype(F32)
        v = _gelu(v_ref[0, rows, :].astype(F32))
        mu = jnp.mean(v, axis=-1, keepdims=True)
        var = jnp.mean(jnp.square(v - mu), axis=-1, keepdims=True)
        vn = ((v - mu) * lax.rsqrt(var + EPS) * g_ref[...]).astype(BF16)
        mixed = b_ref[...]
        for g in range(SGU_GROUPS):
            mg = jnp.dot(w_ref[g], vn, preferred_element_type=F32)
            mixed = mixed + jnp.where(grp == g, mg, 0.0)
        o_ref[0, rows, :] = (_gelu(u) * mixed).astype(o_ref.dtype)


def _sgu(p, norm_g, w_s, b_full, tt):
    B, S, _ = p.shape
    tt = min(tt, S)
    return pl.pallas_call(
        functools.partial(_sgu_kernel, nchunk=tt // SGU_CHUNK),
        grid=(B, S // tt),
        in_specs=[
            pl.BlockSpec((1, tt, BRANCH_W), lambda b, i: (b, i, SEG_U)),
            pl.BlockSpec((1, tt, BRANCH_W), lambda b, i: (b, i, SEG_V)),
            pl.BlockSpec((1, BRANCH_W), lambda b, i: (0, 0)),
            pl.BlockSpec((SGU_GROUPS, SGU_CHUNK, SGU_CHUNK), lambda b, i: (0, 0, 0)),
            pl.BlockSpec((SGU_CHUNK, BRANCH_W), lambda b, i: (0, 0)),
        ],
        out_specs=pl.BlockSpec((1, tt, BRANCH_W), lambda b, i: (b, i, 0)),
        out_shape=jax.ShapeDtypeStruct((B, S, BRANCH_W), BF16),
        compiler_params=_cparams(("parallel", "parallel")),
        name="sgu",
    )(p, p, norm_g, w_s, b_full)


def _merge_kernel(ya, yb, yc, yd, g0, g1, g2, g3, wb_ref, wo_ref, x_ref, gate_ref, mult_ref, shift_ref,
                  wr_ref, xo_ref, h2_ref, lg_ref):
    mix = None
    for n, (y, g) in enumerate(((ya, g0), (yb, g1), (yc, g2), (yd, g3))):
        pr = jnp.dot(y[0], wb_ref[n], preferred_element_type=F32)
        t = g[0].astype(F32) * pr
        mix = t if mix is None else mix + t
    out = jnp.dot(mix.astype(BF16), wo_ref[...], preferred_element_type=F32)
    xn = x_ref[0] + gate_ref[0] * out
    xo_ref[0] = xn
    ms = jnp.mean(xn * xn, axis=-1, keepdims=True)
    h2 = xn * lax.rsqrt(ms + EPS) * mult_ref[0] + shift_ref[0]
    h2_ref[0] = h2.astype(BF16)
    lg_ref[0] = lax.dot_general(wr_ref[...], h2, (((1,), (1,)), ((), ())),
                                precision=lax.Precision.HIGHEST, preferred_element_type=F32)


def _merge(ys, p, w_branch, w_out, x, gate, mult2, shift2, w_router_t, tm):
    B, S, D = x.shape
    E = w_router_t.shape[0]
    tm = min(tm, S)
    y_spec = pl.BlockSpec((1, tm, BRANCH_W), lambda b, i: (b, i, 0))
    g_specs = [pl.BlockSpec((1, tm, D), functools.partial(lambda b, i, n: (b, i, n), n=n)) for n in range(N_BRANCH)]
    vec = pl.BlockSpec((1, 1, D), lambda b, i: (b, 0, 0))
    return pl.pallas_call(
        _merge_kernel,
        grid=(B, S // tm),
        in_specs=[y_spec] * 4 + g_specs + [
            pl.BlockSpec((N_BRANCH, BRANCH_W, D), lambda b, i: (0, 0, 0)),
            pl.BlockSpec((D, D), lambda b, i: (0, 0)),
            pl.BlockSpec((1, tm, D), lambda b, i: (b, i, 0)),
            vec, vec, vec,
            pl.BlockSpec((E, D), lambda b, i: (0, 0)),
        ],
        out_specs=[
            pl.BlockSpec((1, tm, D), lambda b, i: (b, i, 0)),
            pl.BlockSpec((1, tm, D), lambda b, i: (b, i, 0)),
            pl.BlockSpec((1, E, tm), lambda b, i: (b, 0, i)),
        ],
        out_shape=[
            jax.ShapeDtypeStruct((B, S, D), F32),
            jax.ShapeDtypeStruct((B, S, D), BF16),
            jax.ShapeDtypeStruct((B, E, S), F32),
        ],
        compiler_params=_cparams(("parallel", "parallel")),
        name="merge",
    )(*ys, p, p, p, p, w_branch, w_out, x, gate, mult2, shift2, w_router_t)


def _cumsum_excl(x, tri):
    n = x.shape[1]
    outs = []
    carry = jnp.zeros((x.shape[0], 1), F32)
    for c in range(n // 128):
        xc = x[:, c * 128:(c + 1) * 128]
        outs.append(jnp.dot(xc.astype(BF16), tri, preferred_element_type=F32) + carry)
        carry = carry + jnp.sum(xc, axis=1, keepdims=True)
    return jnp.concatenate(outs, axis=1)


def _route_kernel(lg_ref, rank_ref, score_ref, *, cap):
    lg = lg_ref[0]
    mx = jnp.max(lg, axis=0, keepdims=True)
    ex = jnp.exp(lg - mx)
    aff = ex / jnp.sum(ex, axis=0, keepdims=True)
    E = lg.shape[0]

    def bisect(i, thr_bits):
        cand = thr_bits | jnp.left_shift(jnp.int32(1), 30 - i)
        cnt = jnp.sum(jnp.where(aff >= pltpu.bitcast(cand, F32), 1, 0), axis=1, keepdims=True)
        return jnp.where(cnt >= cap, cand, thr_bits)

    thr = pltpu.bitcast(lax.fori_loop(0, 31, bisect, jnp.zeros((E, 1), I32)), F32)
    gt = aff > thr
    eq = aff == thr
    need = (cap - jnp.sum(jnp.where(gt, 1, 0), axis=1, keepdims=True)).astype(F32)
    ri = lax.broadcasted_iota(I32, (128, 128), 0)
    ci = lax.broadcasted_iota(I32, (128, 128), 1)
    tri = jnp.where(ri < ci, 1.0, 0.0).astype(BF16)
    eq_before = _cumsum_excl(jnp.where(eq, 1.0, 0.0), tri)
    sel = gt | (eq & (eq_before < need))
    rank = _cumsum_excl(jnp.where(sel, 1.0, 0.0), tri)
    sel = sel & (rank < cap)
    rank_ref[0] = jnp.where(sel, rank.astype(I32), -1)
    score_ref[0] = jnp.where(sel, aff, 0.0)


def _route(logits, cap):
    B, E, n = logits.shape
    spec = pl.BlockSpec((1, E, n), lambda b: (b, 0, 0))
    return pl.pallas_call(
        functools.partial(_route_kernel, cap=cap),
        grid=(B,),
        in_specs=[spec],
        out_specs=[spec, spec],
        out_shape=[jax.ShapeDtypeStruct((B, E, n), I32), jax.ShapeDtypeStruct((B, E, n), F32)],
        compiler_params=_cparams(("parallel",)),
        name="route",
    )(logits)


def _gather_kernel(cnt_ref, rank_ref, h_ref, o_ref, *, nc, W):
    b = pl.program_id(0)
    e = pl.program_id(2)
    base = (b * pl.num_programs(2) + e) * (nc + 1)
    o_ref[...] = jnp.zeros(o_ref.shape, o_ref.dtype)
    T = MOE_CHUNK

    def cbody(c, carry):
        lo = cnt_ref[base + c]
        hi = cnt_ref[base + c + 1]
        r = rank_ref[0, 0, pl.ds(c, 1), :]
        hc = h_ref[0, pl.ds(pl.multiple_of(c * T, T), T), :]

        def wbody(w, carry2):
            ws = pl.multiple_of(w * W, W)
            slot = lax.broadcasted_iota(I32, (W, T), 0) + ws
            oh = jnp.where(slot == r, 1.0, 0.0).astype(BF16)
            got = jnp.dot(oh, hc, preferred_element_type=F32)
            o_ref[0, 0, pl.ds(ws, W), :] = o_ref[0, 0, pl.ds(ws, W), :] + got.astype(o_ref.dtype)
            return carry2

        lax.fori_loop(lo // W, (hi + W - 1) // W, wbody, 0)
        return carry

    lax.fori_loop(0, nc, cbody, 0)


def _gather(cnt, rank, h, cap):
    B, E, n = rank.shape
    D = h.shape[-1]
    nc = n // MOE_CHUNK
    W = min(128, cap)
    ds = 2
    return pl.pallas_call(
        functools.partial(_gather_kernel, nc=nc, W=W),
        grid_spec=pltpu.PrefetchScalarGridSpec(
            num_scalar_prefetch=1,
            grid=(B, ds, E),
            in_specs=[
                pl.BlockSpec((1, 1, nc, MOE_CHUNK), lambda b, d, e, cnt: (b, e, 0, 0)),
                pl.BlockSpec((1, n, D // ds), lambda b, d, e, cnt: (b, 0, d)),
            ],
            out_specs=pl.BlockSpec((1, 1, cap, D // ds), lambda b, d, e, cnt: (b, e, 0, d)),
        ),
        out_shape=jax.ShapeDtypeStruct((B, E, cap, D), BF16),
        compiler_params=_cparams(("parallel", "parallel", "arbitrary")),
        name="moe_gather",
    )(cnt, rank.reshape(B, E, nc, MOE_CHUNK), h)


def _ffn_kernel(x_ref, wg_ref, wu_ref, wd_ref, o_ref, acc_scr):
    f = pl.program_id(1)
    nb, _, cap, D = x_ref.shape

    @pl.when(f == 0)
    def _():
        acc_scr[...] = jnp.zeros(acc_scr.shape, F32)

    wg = wg_ref[0].astype(BF16)
    wu = wu_ref[0].astype(BF16)
    wd = wd_ref[0].astype(BF16)
    for b in range(nb):
        x = x_ref[b, 0]
        g = jnp.dot(x, wg, preferred_element_type=F32)
        u = jnp.dot(x, wu, preferred_element_type=F32)
        hid = (g * (1.0 / (1.0 + jnp.exp(-g))) * u).astype(BF16)
        acc_scr[b] = acc_scr[b] + jnp.dot(hid, wd, preferred_element_type=F32)

    @pl.when(f == pl.num_programs(1) - 1)
    def _():
        o_ref[:, 0] = acc_scr[...].astype(o_ref.dtype)


def _ffn(xin, w_gate, w_up, w_down, tf):
    B, E, cap, D = xin.shape
    Fh = w_gate.shape[-1]
    return pl.pallas_call(
        _ffn_kernel,
        grid=(E, Fh // tf),
        in_specs=[
            pl.BlockSpec((B, 1, cap, D), lambda e, f: (0, e, 0, 0)),
            pl.BlockSpec((1, D, tf), lambda e, f: (e, 0, f)),
            pl.BlockSpec((1, D, tf), lambda e, f: (e, 0, f)),
            pl.BlockSpec((1, tf, D), lambda e, f: (e, f, 0)),
        ],
        out_specs=pl.BlockSpec((B, 1, cap, D), lambda e, f: (0, e, 0, 0)),
        out_shape=jax.ShapeDtypeStruct((B, E, cap, D), BF16),
        scratch_shapes=[pltpu.VMEM((B, cap, D), F32)],
        compiler_params=_cparams(("parallel", "arbitrary")),
        name="moe_ffn",
    )(xin, w_gate, w_up, w_down)


def _scatter_kernel(cnt_ref, rank_ref, score_ref, y_ref, x_ref, g2_ref, fg_ref, o_ref, acc_scr, *, nc, W, final):
    b = pl.program_id(0)
    i = pl.program_id(1)
    e = pl.program_id(2)
    E = pl.num_programs(2)
    T = MOE_CHUNK
    nsub = acc_scr.shape[0] // T
    base = (b * E + e) * (nc + 1) + i * nsub

    @pl.when(e == 0)
    def _():
        acc_scr[...] = jnp.zeros(acc_scr.shape, F32)

    lane = lax.broadcasted_iota(I32, (T, E), 1)
    for c in range(nsub):
        rows = slice(c * T, (c + 1) * T)
        lo = cnt_ref[base + c]
        hi = cnt_ref[base + c + 1]
        rc = jnp.sum(jnp.where(lane == e, rank_ref[0, rows, :], 0), axis=1, keepdims=True)
        sc = jnp.sum(jnp.where(lane == e, score_ref[0, rows, :], 0.0), axis=1, keepdims=True)

        def wbody(w, carry, rc=rc, sc=sc, rows=rows):
            ws = pl.multiple_of(w * W, W)
            slot = lax.broadcasted_iota(I32, (T, W), 1) + ws
            oh = jnp.where(slot == rc, 1.0, 0.0).astype(BF16)
            got = jnp.dot(oh, y_ref[0, 0, pl.ds(ws, W), :], preferred_element_type=F32)
            acc_scr[rows, :] = acc_scr[rows, :] + sc * got
            return carry

        lax.fori_loop(lo // W, (hi + W - 1) // W, wbody, 0)

    @pl.when(e == E - 1)
    def _():
        xn = x_ref[0] + g2_ref[0] * acc_scr[...]
        if final:
            ms = jnp.mean(xn * xn, axis=-1, keepdims=True)
            xn = xn * lax.rsqrt(ms + EPS) * fg_ref[...]
        o_ref[0] = xn


def _scatter(cnt, rank_tm, score_tm, y, x, g2, final_g, tt, final):
    B, n, E = rank_tm.shape
    cap, D = y.shape[2], y.shape[3]
    tt = min(tt, n)
    nc = n // MOE_CHUNK
    W = min(128, cap)
    return pl.pallas_call(
        functools.partial(_scatter_kernel, nc=nc, W=W, final=final),
        grid_spec=pltpu.PrefetchScalarGridSpec(
            num_scalar_prefetch=1,
            grid=(B, n // tt, E),
            in_specs=[
                pl.BlockSpec((1, tt, E), lambda b, i, e, cnt: (b, i, 0)),
                pl.BlockSpec((1, tt, E), lambda b, i, e, cnt: (b, i, 0)),
                pl.BlockSpec((1, 1, cap, D), lambda b, i, e, cnt: (b, e, 0, 0)),
                pl.BlockSpec((1, tt, D), lambda b, i, e, cnt: (b, i, 0)),
                pl.BlockSpec((1, 1, D), lambda b, i, e, cnt: (b, 0, 0)),
                pl.BlockSpec((1, D), lambda b, i, e, cnt: (0, 0)),
            ],
            out_specs=pl.BlockSpec((1, tt, D), lambda b, i, e, cnt: (b, i, 0)),
            scratch_shapes=[pltpu.VMEM((tt, D), F32)],
        ),
        out_shape=jax.ShapeDtypeStruct((B, n, D), F32),
        compiler_params=_cparams(("parallel", "parallel", "arbitrary")),
        name="moe_scatter",
    )(cnt, rank_tm, score_tm, y, x, g2, final_g)


def _moe(h2, logits, x, g2, w_gate, w_up, w_down, final_g, final):
    B, n, D = h2.shape
    E = logits.shape[1]
    cap = CAPACITY_FACTOR * n // N_EXPERTS
    rank, score = _route(logits, cap)
    nc = n // MOE_CHUNK
    per_chunk = jnp.sum((rank >= 0).reshape(B, E, nc, MOE_CHUNK), axis=-1, dtype=I32)
    cnt = jnp.concatenate([jnp.zeros((B, E, 1), I32), jnp.cumsum(per_chunk, axis=-1, dtype=I32)], axis=-1)
    cnt = cnt.reshape(-1)
    xin = _gather(cnt, rank, h2, cap)
    y = _ffn(xin, w_gate, w_up, w_down, tf=256)
    return _scatter(cnt, jnp.swapaxes(rank, 1, 2), jnp.swapaxes(score, 1, 2), y, x, g2, final_g,
                    tt=1024, final=final)


def _rms(x, g):
    xf = x.astype(F32)
    return xf * lax.rsqrt(jnp.mean(xf * xf, axis=-1, keepdims=True) + EPS) * g.astype(F32)


def _rope_tables(n, dim):
    t = jnp.arange(n)
    row = (t // GRID_W).astype(F32)
    col = (t % GRID_W).astype(F32)
    n_pairs = dim // 4
    inv = ROPE_THETA ** (-jnp.arange(n_pairs, dtype=F32) / n_pairs)
    ang = jnp.concatenate([row[:, None] * inv, col[:, None] * inv], axis=-1)
    return jnp.cos(ang), jnp.sin(ang)


def _rope(x, rope):
    cos, sin = rope
    xf = x.astype(F32).reshape(x.shape[:-1] + (x.shape[-1] // 2, 2))
    x0, x1 = xf[..., 0], xf[..., 1]
    out = jnp.stack([x0 * cos - x1 * sin, x0 * sin + x1 * cos], axis=-1)
    return out.reshape(x.shape)


def _heads(t, n):
    b, s, _ = t.shape
    return t.reshape(b, s, n, -1).transpose(0, 2, 1, 3)


def _unheads(t):
    b, h, s, d = t.shape
    return t.transpose(0, 2, 1, 3).reshape(b, s, h * d)


def _seg(p, blk):
    return p[..., blk * COL_BLK:(blk + 1) * COL_BLK]


def _kv_heads(p, k_gain, ropes):
    ka, va, kb, vb = (p[..., i * COL_BLK:(i + 1) * COL_BLK] for i in range(4))
    kd = p[..., 4 * COL_BLK:4 * COL_BLK + GQA_KV_HEADS * HEAD_DIM]
    vd = p[..., 4 * COL_BLK + GQA_KV_HEADS * HEAD_DIM:]
    b, s, _ = kb.shape
    kb = kb.reshape(b, s, DIFF_HEADS * 2, DIFF_QK_DIM).transpose(0, 2, 1, 3).astype(F32)
    kd = _rms(_heads(kd, GQA_KV_HEADS), k_gain)
    if ropes is not None:
        kb = _rope(kb, ropes[0])
        kd = _rope(kd, ropes[1])
    return (_heads(ka, NA_HEADS), _heads(va, NA_HEADS), kb.astype(BF16), _heads(vb, DIFF_HEADS),
            kd.astype(BF16), _heads(vd, GQA_KV_HEADS))


def _mixer_branches(p, kv, kv_ctx, ropes, q_gain, bias_a, lam, lam_init, sub_g, sgu_g, sgu_w, sgu_bf):
    b, s, _ = p.shape
    ka, va, kb, vb, kd, vd = kv
    qa = (_heads(_seg(p, SEG_QA), NA_HEADS).astype(F32) * HEAD_DIM ** -0.5).astype(BF16)
    qb = _seg(p, SEG_QB).reshape(b, s, DIFF_HEADS * 2, DIFF_QK_DIM).transpose(0, 2, 1, 3).astype(F32)
    qd = _rms(_heads(_seg(p, SEG_QD), GQA_Q_HEADS), q_gain)
    if kv_ctx is None:
        y_a = _flash(qa, [ka], [va], 1, 1, tq=256, tk=256)
        kbs, vbs, kds, vds = [kb], [vb], [kd], [vd]
    else:
        qb = _rope(qb, ropes[0])
        qd = _rope(qd, ropes[1])
        kac, vac, kbc, vbc, kdc, vdc = kv_ctx
        y_a = _nbr(qa, ka, va, kac, vac, bias_a)
        kbs, vbs, kds, vds = [kbc, kb], [vbc, vb], [kdc, kd], [vdc, vd]
    qb = (qb * DIFF_QK_DIM ** -0.5).astype(BF16)
    qd = (qd * HEAD_DIM ** -0.5).astype(BF16)
    o_b = _flash(qb, kbs, vbs, 1, 2, tq=512, tk=512).reshape(b, DIFF_HEADS, 2, s, DIFF_V_DIM)
    y_b = _rms(o_b[:, :, 0] - lam * o_b[:, :, 1], sub_g) * (1.0 - lam_init)
    y_d = _flash(qd, kds, vds, GQA_Q_HEADS // GQA_KV_HEADS, GQA_Q_HEADS // GQA_KV_HEADS, tq=512, tk=512)
    y_c = _sgu(p, sgu_g, sgu_w, sgu_bf, tt=1024)
    return [_unheads(y_a).astype(BF16), _unheads(y_b).astype(BF16), y_c, _unheads(y_d).astype(BF16)]


def kernel(x, c, ctx, c_ctx, w_mod, b_mod, norm1_g, norm2_g, w_in, q_gain, k_gain, na_rpb, lambda_q1, lambda_k1, lambda_q2, lambda_k2, diff_sub_g, sgu_norm_g, sgu_w, sgu_b, gate_b, w_branch, w_out, w_router, w_e_gate, w_e_up, w_e_down, final_g):
    B, S, D = x.shape
    depth = w_mod.shape[0]
    rows = S // GRID_W
    ropes = (_rope_tables(S, DIFF_QK_DIM), _rope_tables(S, HEAD_DIM))
    hp = lax.Precision.HIGHEST
    fg = final_g.astype(F32)[None, :]
    for l in range(depth):
        last = l == depth - 1
        mod = jnp.dot(jax.nn.silu(c), w_mod[l], precision=hp) + b_mod[l]
        sh1, sc1, g1, sh2, sc2, g2 = (t[:, None, :] for t in jnp.split(mod, 6, axis=-1))
        mod_c = jnp.dot(jax.nn.silu(c_ctx), w_mod[l], precision=hp) + b_mod[l]
        csh1, csc1, cg1, csh2, csc2, cg2 = (jnp.broadcast_to(t[None, None, :], (B, 1, D))
                                            for t in jnp.split(mod_c, 6, axis=-1))
        lam_init = 0.8 - 0.6 * math.exp(-0.3 * l)
        lam = (jnp.exp(jnp.sum(lambda_q1[l].astype(F32) * lambda_k1[l].astype(F32)))
               - jnp.exp(jnp.sum(lambda_q2[l].astype(F32) * lambda_k2[l].astype(F32))) + lam_init)

        n_main = w_in.shape[-1] - 1280
        w_l = jnp.concatenate([w_in[l][:, n_main - GATE_W:n_main], w_in[l][:, :n_main - GATE_W],
                               w_in[l][:, n_main:]], axis=1).astype(BF16)
        bias_in = jnp.concatenate([gate_b[l].astype(F32), jnp.zeros((w_l.shape[1] - GATE_W,), F32)])[None, :]
        n1 = norm1_g[l].astype(F32)[None, None, :]
        n2 = norm2_g[l].astype(F32)[None, None, :]
        wb = w_branch[l].astype(BF16)
        wo = w_out[l].astype(BF16)
        wr_t = w_router[l].astype(F32).T
        sgu_g = sgu_norm_g[l].astype(F32)[None, :]
        sgu_wb = sgu_w[l].astype(BF16)
        sgu_bf = jnp.repeat(sgu_b[l].astype(F32).T, BRANCH_W // SGU_GROUPS, axis=1)
        bias_a = _nbr_bias(na_rpb[l], rows)

        p = _inproj(x, n1 * (1.0 + sc1), sh1, w_l, bias_in, GATE_W, tm=1024, tn=512)
        kv = _kv_heads(p[..., KV_COL0:], k_gain[l], ropes)
        if last:
            pc_kv = _inproj(ctx, n1 * (1.0 + csc1), csh1, w_l[:, KV_COL0:], bias_in[:, KV_COL0:], 0,
                            tm=256, tn=256)
            kv_c = _kv_heads(pc_kv, k_gain[l], None)
        else:
            pc = _inproj(ctx, n1 * (1.0 + csc1), csh1, w_l, bias_in, GATE_W, tm=256, tn=512)
            kv_c = _kv_heads(pc[..., KV_COL0:], k_gain[l], None)
            ys_c = _mixer_branches(pc, kv_c, None, None, q_gain[l], None, lam, lam_init, diff_sub_g[l],
                                   sgu_g, sgu_wb, sgu_bf)
            ctx, hc2, lg_c = _merge(ys_c, pc, wb, wo, ctx, cg1, n2 * (1.0 + csc2), csh2, wr_t, tm=256)
        ys = _mixer_branches(p, kv, kv_c, ropes, q_gain[l], bias_a, lam, lam_init, diff_sub_g[l],
                             sgu_g, sgu_wb, sgu_bf)
        x, h2, lg = _merge(ys, p, wb, wo, x, g1, n2 * (1.0 + sc2), sh2, wr_t, tm=512)

        x = _moe(h2, lg, x, g2, w_e_gate[l], w_e_up[l], w_e_down[l], fg, final=last)
        if not last:
            ctx = _moe(hc2, lg_c, ctx, cg2, w_e_gate[l], w_e_up[l], w_e_down[l], fg, final=False)
    return x
```

```python
import functools
import math

import jax
import jax.numpy as jnp
from jax import lax
from jax.experimental import pallas as pl
from jax.experimental.pallas import tpu as pltpu

F32 = jnp.float32
BF16 = jnp.bfloat16
I32 = jnp.int32

GRID_W = 64
HEAD_DIM = 64
N_BRANCH = 4
BRANCH_W = 256
NA_HEADS = 4
WIN_H = 8
WIN_W = 16
DIFF_HEADS = 4
DIFF_QK_DIM = 32
DIFF_V_DIM = 64
SGU_GROUPS = 4
SGU_CHUNK = 128
GQA_Q_HEADS = 4
GQA_KV_HEADS = 2
ROPE_THETA = 10000.0
N_EXPERTS = 16
CAPACITY_FACTOR = 2
EPS = 1e-6
NEG = -1e30

GATE_W = N_BRANCH * 1024
COL_BLK = 256
SEG_QA, SEG_QB, SEG_QD, SEG_U, SEG_V, SEG_KA, SEG_VA, SEG_KB, SEG_VB = range(16, 25)
COL_KD = GATE_W + 9 * COL_BLK
KV_COL0 = GATE_W + 5 * COL_BLK

VMEM_LIMIT = 56 * 1024 * 1024

NBR_ROWS = 8
NBR_KROWS = 16
MOE_CHUNK = 256


def _cparams(sem):
    return pltpu.CompilerParams(dimension_semantics=sem, vmem_limit_bytes=VMEM_LIMIT)


def _dot_nt(a, b):
    return lax.dot_general(a, b, (((1,), (1,)), ((), ())), preferred_element_type=F32)


def _inproj_kernel(x_ref, mult_ref, shift_ref, w_ref, bias_ref, o_ref, h_scr, *, n_gate_tiles):
    j = pl.program_id(2)

    @pl.when(j == 0)
    def _():
        x = x_ref[0]
        ms = jnp.mean(x * x, axis=-1, keepdims=True)
        h = x * lax.rsqrt(ms + EPS) * mult_ref[0] + shift_ref[0]
        h_scr[...] = h.astype(BF16)

    acc = jnp.dot(h_scr[...], w_ref[...], preferred_element_type=F32) + bias_ref[...]

    @pl.when(j < n_gate_tiles)
    def _():
        o_ref[0] = (1.0 / (1.0 + jnp.exp(-acc))).astype(o_ref.dtype)

    @pl.when(j >= n_gate_tiles)
    def _():
        o_ref[0] = acc.astype(o_ref.dtype)


def _inproj(x, mult, shift, w, bias, n_gate_cols, tm, tn):
    B, S, D = x.shape
    N = w.shape[1]
    tm = min(tm, S)
    return pl.pallas_call(
        functools.partial(_inproj_kernel, n_gate_tiles=n_gate_cols // tn),
        grid=(B, S // tm, N // tn),
        in_specs=[
            pl.BlockSpec((1, tm, D), lambda b, i, j: (b, i, 0)),
            pl.BlockSpec((1, 1, D), lambda b, i, j: (b, 0, 0)),
            pl.BlockSpec((1, 1, D), lambda b, i, j: (b, 0, 0)),
            pl.BlockSpec((D, tn), lambda b, i, j: (0, j)),
            pl.BlockSpec((1, tn), lambda b, i, j: (0, j)),
        ],
        out_specs=pl.BlockSpec((1, tm, tn), lambda b, i, j: (b, i, j)),
        out_shape=jax.ShapeDtypeStruct((B, S, N), BF16),
        scratch_shapes=[pltpu.VMEM((tm, D), BF16)],
        compiler_params=_cparams(("parallel", "parallel", "arbitrary")),
        name="inproj",
    )(x, mult, shift, w, bias)


LANES = 128


def _flash_kernel(*refs, nseg, tks, J, R, dv, diff):
    if diff:
        lam_ref, subg_ref = refs[:2]
        refs = refs[2:]
    q_ref = refs[0]
    k_refs = refs[1:1 + nseg]
    v_refs = refs[1 + nseg:1 + 2 * nseg]
    o_ref = refs[1 + 2 * nseg]
    m_scr, acc_scr = refs[2 + 2 * nseg:]
    tq, d = q_ref.shape[-2:]
    rows = R * tq
    qs = [q_ref[0, 0, j].reshape(rows, d) for j in range(J)]
    m_scr[...] = jnp.full(m_scr.shape, NEG, F32)
    acc_scr[...] = jnp.zeros(acc_scr.shape, F32)
    for k_ref, v_ref, tk in zip(k_refs, v_refs, tks):
        def body(c, carry, k_ref=k_ref, v_ref=v_ref, tk=tk):
            start = pl.multiple_of(c * tk, tk)
            for j in range(J):
                kc = k_ref[0, 0, j, pl.ds(start, tk), :]
                vc = v_ref[0, 0, j, pl.ds(start, tk), :]
                s = _dot_nt(qs[j], kc)
                slabs = [s[:, t * LANES:(t + 1) * LANES] for t in range(tk // LANES)]
                m_cur = functools.reduce(jnp.maximum, slabs)
                m_prev = m_scr[j]
                m_new = jnp.maximum(m_prev, jnp.max(m_cur, axis=1, keepdims=True))
                alpha = jnp.exp2(m_prev - m_new)
                p = jnp.concatenate([jnp.exp2(sl - m_new) for sl in slabs], axis=1).astype(BF16)
                acc_scr[j] = alpha * acc_scr[j] + jnp.dot(p, vc, preferred_element_type=F32)
                m_scr[j] = m_new
            return carry

        lax.fori_loop(0, k_ref.shape[3] // tk, body, 0)
    pieces = []
    for j in range(J):
        acc = acc_scr[j]
        o = acc[:, :dv] / acc[:, dv:dv + 1]
        parts = [o[r * tq:(r + 1) * tq] for r in range(R)]
        if diff:
            y = parts[0] - lam_ref[...] * parts[1]
            y = y * lax.rsqrt(jnp.mean(y * y, axis=-1, keepdims=True) + EPS) * subg_ref[...]
            pieces.append(y)
        else:
            pieces.extend(parts)
    o_ref[0] = jnp.concatenate(pieces, axis=1).astype(o_ref.dtype)


def _flash(q, ks, vs, tq, tk, diff=None):
    B, Hs, J, R, Sq, d = q.shape
    dv = LANES // 2
    assert (J if diff else J * R) * dv == LANES
    tq = min(tq, Sq)
    tks = tuple(min(tk, k.shape[3]) for k in ks)
    in_specs = []
    if diff:
        in_specs += [pl.BlockSpec((1, dv), lambda b, h, i: (0, 0))] * 2
    in_specs.append(pl.BlockSpec((1, 1, J, R, tq, d), lambda b, h, i: (b, h, 0, 0, i, 0)))
    for k in ks:
        in_specs.append(pl.BlockSpec((1, 1, J, k.shape[3], d), lambda b, h, i: (b, h, 0, 0, 0)))
    for v in vs:
        in_specs.append(pl.BlockSpec((1, 1, J, v.shape[3], LANES), lambda b, h, i: (b, h, 0, 0, 0)))
    return pl.pallas_call(
        functools.partial(_flash_kernel, nseg=len(ks), tks=tks, J=J, R=R, dv=dv, diff=bool(diff)),
        grid=(B, Hs, Sq // tq),
        in_specs=in_specs,
        out_specs=pl.BlockSpec((1, tq, LANES), lambda b, h, i: (b, i, h)),
        out_shape=jax.ShapeDtypeStruct((B, Sq, Hs * LANES), BF16),
        scratch_shapes=[pltpu.VMEM((J, R * tq, LANES), F32), pltpu.VMEM((J, R * tq, LANES), F32)],
        compiler_params=_cparams(("parallel", "parallel", "arbitrary")),
        name="flash_diff" if diff else "flash",
    )(*(diff or ()), q, *ks, *vs)


NBR_HEADS = 2


def _nbr_kernel(q_ref, k0, k1, k2, k3, v0, v1, v2, v3, kc_ref, vc_ref, bias_ref, o_ref):
    outs = []
    for j in range(NBR_HEADS):
        q = q_ref[0, j]
        k = jnp.concatenate([k0[0, j], k1[0, j], k2[0, j], k3[0, j]], axis=0)
        v = jnp.concatenate([v0[0, j], v1[0, j], v2[0, j], v3[0, j]], axis=0)
        s_loc = _dot_nt(q, k) + bias_ref[0, j]
        s_ctx = _dot_nt(q, kc_ref[0, j])
        m = jnp.maximum(jnp.max(s_loc, axis=1, keepdims=True), jnp.max(s_ctx, axis=1, keepdims=True))
        p_loc = jnp.exp(s_loc - m)
        p_ctx = jnp.exp(s_ctx - m)
        l = jnp.sum(p_loc, axis=1, keepdims=True) + jnp.sum(p_ctx, axis=1, keepdims=True)
        o = (jnp.dot(p_ctx.astype(BF16), vc_ref[0, j], preferred_element_type=F32)
             + jnp.dot(p_loc.astype(BF16), v, preferred_element_type=F32))
        outs.append(o / l)
    o_ref[0] = jnp.concatenate(outs, axis=1).astype(o_ref.dtype)


def _nbr_bias(rpb, rows):
    kh, kw = WIN_H, WIN_W
    qc = jnp.arange(GRID_W)
    kc = jnp.arange(GRID_W)
    c0 = jnp.clip(qc - kw // 2, 0, GRID_W - kw)
    col_ok = (kc[None, :] >= c0[:, None]) & (kc[None, :] < c0[:, None] + kw)
    dj = kc[None, :] - qc[:, None] + WIN_W - 1
    oh_c = ((dj[..., None] == jnp.arange(2 * WIN_W - 1)) & col_ok[..., None]).astype(F32)

    def variant(r_start, k_start):
        r = r_start + jnp.arange(NBR_ROWS)
        kr = k_start + jnp.arange(NBR_KROWS)
        r0 = jnp.clip(r - kh // 2, 0, rows - kh)
        row_ok = (kr[None, :] >= r0[:, None]) & (kr[None, :] < r0[:, None] + kh)
        di = kr[None, :] - r[:, None] + WIN_H - 1
        oh_r = ((di[..., None] == jnp.arange(2 * WIN_H - 1)) & row_ok[..., None]).astype(F32)
        b = jnp.einsum('jka,hab,qcb->hjqkc', oh_r, rpb.astype(F32), oh_c, precision=lax.Precision.HIGHEST)
        valid = row_ok[:, None, :, None] & col_ok[None, :, None, :]
        b = jnp.where(valid[None], b, NEG)
        return b.reshape(rpb.shape[0], NBR_ROWS * GRID_W, NBR_KROWS * GRID_W)

    return jnp.stack([variant(0, 0), variant(NBR_ROWS, NBR_ROWS - kh // 2),
                      variant(rows - NBR_ROWS, rows - NBR_KROWS)])


def _nbr(q, k, v, kc, vc, bias):
    B, H, S, d = q.shape
    L = kc.shape[2]
    tq = NBR_ROWS * GRID_W
    kb = tq // 2
    nb = S // tq
    nh = NBR_HEADS
    assert S % tq == 0 and nb >= 3 and nh * d == LANES

    def kmap(j):
        return lambda h, i, b: (b, h, jnp.clip(2 * i - 1, 0, 2 * nb - 4) + j, 0)

    def bmap(h, i, b):
        return (jnp.where(i == 0, 0, jnp.where(i == nb - 1, 2, 1)), h, 0, 0)

    kv_specs = [pl.BlockSpec((1, nh, kb, d), kmap(j)) for j in range(4)]
    return pl.pallas_call(
        _nbr_kernel,
        grid=(H // nh, nb, B),
        in_specs=[pl.BlockSpec((1, nh, tq, d), lambda h, i, b: (b, h, i, 0))] + kv_specs + kv_specs + [
            pl.BlockSpec((1, nh, L, d), lambda h, i, b: (b, h, 0, 0)),
            pl.BlockSpec((1, nh, L, d), lambda h, i, b: (b, h, 0, 0)),
            pl.BlockSpec((1, nh, tq, NBR_KROWS * GRID_W), bmap),
        ],
        out_specs=pl.BlockSpec((1, tq, LANES), lambda h, i, b: (b, i, h)),
        out_shape=jax.ShapeDtypeStruct((B, S, H * d), BF16),
        compiler_params=_cparams(("parallel", "parallel", "parallel")),
        name="nbr_attn",
    )(q, k, k, k, k, v, v, v, v, kc, vc, bias)


def _gelu(x):
    return 0.5 * x * (1.0 + jnp.tanh(math.sqrt(2.0 / math.pi) * (x + 0.044715 * (x * x * x))))


def _sgu_kernel(u_ref, v_ref, g_ref, w_ref, b_ref, o_ref, *, nchunk):
    grp = lax.broadcasted_iota(I32, (SGU_CHUNK, BRANCH_W), 1) // (BRANCH_W // SGU_GROUPS)
    for c in range(nchunk):
        rows = slice(c * SGU_CHUNK, (c + 1) * SGU_CHUNK)
        u = u_ref[0, rows, :].astype(F32)
        v = _gelu(v_ref[0, rows, :].astype(F32))
        mu = jnp.mean(v, axis=-1, keepdims=True)
        var = jnp.mean(jnp.square(v - mu), axis=-1, keepdims=True)
        vn = ((v - mu) * lax.rsqrt(var + EPS) * g_ref[...]).astype(BF16)
        mixed = b_ref[...]
        for g in range(SGU_GROUPS):
            mg = jnp.dot(w_ref[g], vn, preferred_element_type=F32)
            mixed = mixed + jnp.where(grp == g, mg, 0.0)
        o_ref[0, rows, :] = (_gelu(u) * mixed).astype(o_ref.dtype)


def _sgu(p, norm_g, w_s, b_full, tt):
    B, S, _ = p.shape
    tt = min(tt, S)
    return pl.pallas_call(
        functools.partial(_sgu_kernel, nchunk=tt // SGU_CHUNK),
        grid=(B, S // tt),
        in_specs=[
            pl.BlockSpec((1, tt, BRANCH_W), lambda b, i: (b, i, SEG_U)),
            pl.BlockSpec((1, tt, BRANCH_W), lambda b, i: (b, i, SEG_V)),
            pl.BlockSpec((1, BRANCH_W), lambda b, i: (0, 0)),
            pl.BlockSpec((SGU_GROUPS, SGU_CHUNK, SGU_CHUNK), lambda b, i: (0, 0, 0)),
            pl.BlockSpec((SGU_CHUNK, BRANCH_W), lambda b, i: (0, 0)),
        ],
        out_specs=pl.BlockSpec((1, tt, BRANCH_W), lambda b, i: (b, i, 0)),
        out_shape=jax.ShapeDtypeStruct((B, S, BRANCH_W), BF16),
        compiler_params=_cparams(("parallel", "parallel")),
        name="sgu",
    )(p, p, norm_g, w_s, b_full)


def _merge_kernel(ya, yb, yc, yd, g0, g1, g2, g3, wb_ref, wo_ref, x_ref, gate_ref, mult_ref, shift_ref,
                  wr_ref, xo_ref, h2_ref, lg_ref):
    mix = None
    for n, (y, g) in enumerate(((ya, g0), (yb, g1), (yc, g2), (yd, g3))):
        pr = jnp.dot(y[0], wb_ref[n], preferred_element_type=F32)
        t = g[0].astype(F32) * pr
        mix = t if mix is None else mix + t
    out = jnp.dot(mix.astype(BF16), wo_ref[...], preferred_element_type=F32)
    xn = x_ref[0] + gate_ref[0] * out
    xo_ref[0] = xn
    ms = jnp.mean(xn * xn, axis=-1, keepdims=True)
    h2 = xn * lax.rsqrt(ms + EPS) * mult_ref[0] + shift_ref[0]
    h2_ref[0] = h2.astype(BF16)
    lg_ref[0] = lax.dot_general(wr_ref[...], h2, (((1,), (1,)), ((), ())),
                                precision=lax.Precision.HIGHEST, preferred_element_type=F32)


def _merge(ys, p, w_branch, w_out, x, gate, mult2, shift2, w_router_t, tm):
    B, S, D = x.shape
    E = w_router_t.shape[0]
    tm = min(tm, S)
    y_spec = pl.BlockSpec((1, tm, BRANCH_W), lambda b, i: (b, i, 0))
    g_specs = [pl.BlockSpec((1, tm, D), functools.partial(lambda b, i, n: (b, i, n), n=n)) for n in range(N_BRANCH)]
    vec = pl.BlockSpec((1, 1, D), lambda b, i: (b, 0, 0))
    return pl.pallas_call(
        _merge_kernel,
        grid=(B, S // tm),
        in_specs=[y_spec] * 4 + g_specs + [
            pl.BlockSpec((N_BRANCH, BRANCH_W, D), lambda b, i: (0, 0, 0)),
            pl.BlockSpec((D, D), lambda b, i: (0, 0)),
            pl.BlockSpec((1, tm, D), lambda b, i: (b, i, 0)),
            vec, vec, vec,
            pl.BlockSpec((E, D), lambda b, i: (0, 0)),
        ],
        out_specs=[
            pl.BlockSpec((1, tm, D), lambda b, i: (b, i, 0)),
            pl.BlockSpec((1, tm, D), lambda b, i: (b, i, 0)),
            pl.BlockSpec((1, E, tm), lambda b, i: (b, 0, i)),
        ],
        out_shape=[
            jax.ShapeDtypeStruct((B, S, D), F32),
            jax.ShapeDtypeStruct((B, S, D), BF16),
            jax.ShapeDtypeStruct((B, E, S), F32),
        ],
        compiler_params=_cparams(("parallel", "parallel")),
        name="merge",
    )(*ys, p, p, p, p, w_branch, w_out, x, gate, mult2, shift2, w_router_t)


def _cumsum_excl(x, tri):
    n = x.shape[1]
    outs = []
    carry = jnp.zeros((x.shape[0], 1), F32)
    for c in range(n // 128):
        xc = x[:, c * 128:(c + 1) * 128]
        outs.append(jnp.dot(xc.astype(BF16), tri, preferred_element_type=F32) + carry)
        carry = carry + jnp.sum(xc, axis=1, keepdims=True)
    return jnp.concatenate(outs, axis=1)


def _route_kernel(lg_ref, rank_ref, score_ref, *, cap):
    lg = lg_ref[0]
    mx = jnp.max(lg, axis=0, keepdims=True)
    ex = jnp.exp(lg - mx)
    aff = ex / jnp.sum(ex, axis=0, keepdims=True)
    E = lg.shape[0]

    def bisect(i, thr_bits):
        cand = thr_bits | jnp.left_shift(jnp.int32(1), 30 - i)
        cnt = jnp.sum(jnp.where(aff >= pltpu.bitcast(cand, F32), 1, 0), axis=1, keepdims=True)
        return jnp.where(cnt >= cap, cand, thr_bits)

    thr = pltpu.bitcast(lax.fori_loop(0, 31, bisect, jnp.zeros((E, 1), I32)), F32)
    gt = aff > thr
    eq = aff == thr
    need = (cap - jnp.sum(jnp.where(gt, 1, 0), axis=1, keepdims=True)).astype(F32)
    ri = lax.broadcasted_iota(I32, (128, 128), 0)
    ci = lax.broadcasted_iota(I32, (128, 128), 1)
    tri = jnp.where(ri < ci, 1.0, 0.0).astype(BF16)
    eq_before = _cumsum_excl(jnp.where(eq, 1.0, 0.0), tri)
    sel = gt | (eq & (eq_before < need))
    rank = _cumsum_excl(jnp.where(sel, 1.0, 0.0), tri)
    sel = sel & (rank < cap)
    rank_ref[0] = jnp.where(sel, rank.astype(I32), -1)
    score_ref[0] = jnp.where(sel, aff, 0.0)


def _route(logits, cap):
    B, E, n = logits.shape
    spec = pl.BlockSpec((1, E, n), lambda b: (b, 0, 0))
    return pl.pallas_call(
        functools.partial(_route_kernel, cap=cap),
        grid=(B,),
        in_specs=[spec],
        out_specs=[spec, spec],
        out_shape=[jax.ShapeDtypeStruct((B, E, n), I32), jax.ShapeDtypeStruct((B, E, n), F32)],
        compiler_params=_cparams(("parallel",)),
        name="route",
    )(logits)


def _gather_kernel(cnt_ref, rank_ref, h_ref, o_ref, *, nc, W):
    b = pl.program_id(0)
    e = pl.program_id(2)
    base = (b * pl.num_programs(2) + e) * (nc + 1)
    o_ref[...] = jnp.zeros(o_ref.shape, o_ref.dtype)
    T = MOE_CHUNK

    def cbody(c, carry):
        lo = cnt_ref[base + c]
        hi = cnt_ref[base + c + 1]
        r = rank_ref[0, 0, pl.ds(c, 1), :]
        hc = h_ref[0, pl.ds(pl.multiple_of(c * T, T), T), :]

        def wbody(w, carry2):
            ws = pl.multiple_of(w * W, W)
            slot = lax.broadcasted_iota(I32, (W, T), 0) + ws
            oh = jnp.where(slot == r, 1.0, 0.0).astype(BF16)
            got = jnp.dot(oh, hc, preferred_element_type=F32)
            o_ref[0, 0, pl.ds(ws, W), :] = o_ref[0, 0, pl.ds(ws, W), :] + got.astype(o_ref.dtype)
            return carry2

        lax.fori_loop(lo // W, (hi + W - 1) // W, wbody, 0)
        return carry

    lax.fori_loop(0, nc, cbody, 0)


def _gather(cnt, rank, h, cap):
    B, E, n = rank.shape
    D = h.shape[-1]
    nc = n // MOE_CHUNK
    W = min(128, cap)
    ds = 2
    return pl.pallas_call(
        functools.partial(_gather_kernel, nc=nc, W=W),
        grid_spec=pltpu.PrefetchScalarGridSpec(
            num_scalar_prefetch=1,
            grid=(B, ds, E),
            in_specs=[
                pl.BlockSpec((1, 1, nc, MOE_CHUNK), lambda b, d, e, cnt: (b, e, 0, 0)),
                pl.BlockSpec((1, n, D // ds), lambda b, d, e, cnt: (b, 0, d)),
            ],
            out_specs=pl.BlockSpec((1, 1, cap, D // ds), lambda b, d, e, cnt: (b, e, 0, d)),
        ),
        out_shape=jax.ShapeDtypeStruct((B, E, cap, D), BF16),
        compiler_params=_cparams(("parallel", "parallel", "arbitrary")),
        name="moe_gather",
    )(cnt, rank.reshape(B, E, nc, MOE_CHUNK), h)


def _ffn_kernel(x_ref, wg_ref, wu_ref, wd_ref, o_ref, acc_scr):
    f = pl.program_id(1)
    nb, _, cap, D = x_ref.shape

    @pl.when(f == 0)
    def _():
        acc_scr[...] = jnp.zeros(acc_scr.shape, F32)

    wg = wg_ref[0].astype(BF16)
    wu = wu_ref[0].astype(BF16)
    wd = wd_ref[0].astype(BF16)
    for b in range(nb):
        x = x_ref[b, 0]
        g = jnp.dot(x, wg, preferred_element_type=F32)
        u = jnp.dot(x, wu, preferred_element_type=F32)
        hid = (g * (1.0 / (1.0 + jnp.exp(-g))) * u).astype(BF16)
        acc_scr[b] = acc_scr[b] + jnp.dot(hid, wd, preferred_element_type=F32)

    @pl.when(f == pl.num_programs(1) - 1)
    def _():
        o_ref[:, 0] = acc_scr[...].astype(o_ref.dtype)


def _ffn(xin, w_gate, w_up, w_down, tf):
    B, E, cap, D = xin.shape
    Fh = w_gate.shape[-1]
    return pl.pallas_call(
        _ffn_kernel,
        grid=(E, Fh // tf),
        in_specs=[
            pl.BlockSpec((B, 1, cap, D), lambda e, f: (0, e, 0, 0)),
            pl.BlockSpec((1, D, tf), lambda e, f: (e, 0, f)),
            pl.BlockSpec((1, D, tf), lambda e, f: (e, 0, f)),
            pl.BlockSpec((1, tf, D), lambda e, f: (e, f, 0)),
        ],
        out_specs=pl.BlockSpec((B, 1, cap, D), lambda e, f: (0, e, 0, 0)),
        out_shape=jax.ShapeDtypeStruct((B, E, cap, D), BF16),
        scratch_shapes=[pltpu.VMEM((B, cap, D), F32)],
        compiler_params=_cparams(("parallel", "arbitrary")),
        name="moe_ffn",
    )(xin, w_gate, w_up, w_down)


def _scatter_kernel(cnt_ref, rank_ref, score_ref, y_ref, x_ref, g2_ref, fg_ref, o_ref, acc_scr, *, nc, W, final):
    b = pl.program_id(0)
    i = pl.program_id(1)
    e = pl.program_id(2)
    E = pl.num_programs(2)
    T = MOE_CHUNK
    nsub = acc_scr.shape[0] // T
    base = (b * E + e) * (nc + 1) + i * nsub

    @pl.when(e == 0)
    def _():
        acc_scr[...] = jnp.zeros(acc_scr.shape, F32)

    lane = lax.broadcasted_iota(I32, (T, E), 1)
    for c in range(nsub):
        rows = slice(c * T, (c + 1) * T)
        lo = cnt_ref[base + c]
        hi = cnt_ref[base + c + 1]
        rc = jnp.sum(jnp.where(lane == e, rank_ref[0, rows, :], 0), axis=1, keepdims=True)
        sc = jnp.sum(jnp.where(lane == e, score_ref[0, rows, :], 0.0), axis=1, keepdims=True)

        def wbody(w, carry, rc=rc, sc=sc, rows=rows):
            ws = pl.multiple_of(w * W, W)
            slot = lax.broadcasted_iota(I32, (T, W), 1) + ws
            oh = jnp.where(slot == rc, 1.0, 0.0).astype(BF16)
            got = jnp.dot(oh, y_ref[0, 0, pl.ds(ws, W), :], preferred_element_type=F32)
            acc_scr[rows, :] = acc_scr[rows, :] + sc * got
            return carry

        lax.fori_loop(lo // W, (hi + W - 1) // W, wbody, 0)

    @pl.when(e == E - 1)
    def _():
        xn = x_ref[0] + g2_ref[0] * acc_scr[...]
        if final:
            ms = jnp.mean(xn * xn, axis=-1, keepdims=True)
            xn = xn * lax.rsqrt(ms + EPS) * fg_ref[...]
        o_ref[0] = xn


def _scatter(cnt, rank_tm, score_tm, y, x, g2, final_g, tt, final):
    B, n, E = rank_tm.shape
    cap, D = y.shape[2], y.shape[3]
    tt = min(tt, n)
    nc = n // MOE_CHUNK
    W = min(128, cap)
    return pl.pallas_call(
        functools.partial(_scatter_kernel, nc=nc, W=W, final=final),
        grid_spec=pltpu.PrefetchScalarGridSpec(
            num_scalar_prefetch=1,
            grid=(B, n // tt, E),
            in_specs=[
                pl.BlockSpec((1, tt, E), lambda b, i, e, cnt: (b, i, 0)),
                pl.BlockSpec((1, tt, E), lambda b, i, e, cnt: (b, i, 0)),
                pl.BlockSpec((1, 1, cap, D), lambda b, i, e, cnt: (b, e, 0, 0)),
                pl.BlockSpec((1, tt, D), lambda b, i, e, cnt: (b, i, 0)),
                pl.BlockSpec((1, 1, D), lambda b, i, e, cnt: (b, 0, 0)),
                pl.BlockSpec((1, D), lambda b, i, e, cnt: (0, 0)),
            ],
            out_specs=pl.BlockSpec((1, tt, D), lambda b, i, e, cnt: (b, i, 0)),
            scratch_shapes=[pltpu.VMEM((tt, D), F32)],
        ),
        out_shape=jax.ShapeDtypeStruct((B, n, D), F32),
        compiler_params=_cparams(("parallel", "parallel", "arbitrary")),
        name="moe_scatter",
    )(cnt, rank_tm, score_tm, y, x, g2, final_g)


def _moe(h2, logits, x, g2, w_gate, w_up, w_down, final_g, final):
    B, n, D = h2.shape
    E = logits.shape[1]
    cap = CAPACITY_FACTOR * n // N_EXPERTS
    rank, score = _route(logits, cap)
    nc = n // MOE_CHUNK
    per_chunk = jnp.sum((rank >= 0).reshape(B, E, nc, MOE_CHUNK), axis=-1, dtype=I32)
    cnt = jnp.concatenate([jnp.zeros((B, E, 1), I32), jnp.cumsum(per_chunk, axis=-1, dtype=I32)], axis=-1)
    cnt = cnt.reshape(-1)
    xin = _gather(cnt, rank, h2, cap)
    y = _ffn(xin, w_gate, w_up, w_down, tf=256)
    return _scatter(cnt, jnp.swapaxes(rank, 1, 2), jnp.swapaxes(score, 1, 2), y, x, g2, final_g,
                    tt=1024, final=final)


def _rms(x, g):
    xf = x.astype(F32)
    return xf * lax.rsqrt(jnp.mean(xf * xf, axis=-1, keepdims=True) + EPS) * g.astype(F32)


def _rope_tables(n, dim):
    t = jnp.arange(n)
    row = (t // GRID_W).astype(F32)
    col = (t % GRID_W).astype(F32)
    n_pairs = dim // 4
    inv = ROPE_THETA ** (-jnp.arange(n_pairs, dtype=F32) / n_pairs)
    ang = jnp.concatenate([row[:, None] * inv, col[:, None] * inv], axis=-1)
    return jnp.cos(ang), jnp.sin(ang)


def _rope(x, rope, per_half=False):
    cos, sin = rope
    if per_half:
        cos, sin = cos[:, None, :], sin[:, None, :]
    xf = x.astype(F32).reshape(x.shape[:-1] + (x.shape[-1] // 2, 2))
    x0, x1 = xf[..., 0], xf[..., 1]
    out = jnp.stack([x0 * cos - x1 * sin, x0 * sin + x1 * cos], axis=-1)
    return out.reshape(x.shape)


def _split_halves(t):
    return t.reshape(t.shape[:-1] + (2, t.shape[-1] // 2))


def _with_ones(v):
    pad = [(0, 0)] * (v.ndim - 1) + [(0, LANES - v.shape[-1] - 1)]
    return jnp.pad(jnp.concatenate([v, jnp.ones(v.shape[:-1] + (1,), v.dtype)], axis=-1), pad)


def _heads(t, n):
    b, s, _ = t.shape
    return t.reshape(b, s, n, -1).transpose(0, 2, 1, 3)


def _seg(p, blk):
    return p[..., blk * COL_BLK:(blk + 1) * COL_BLK]


def _kv_heads(p, k_gain, ropes):
    ka, va, kb, vb = (p[..., i * COL_BLK:(i + 1) * COL_BLK] for i in range(4))
    kd = p[..., 4 * COL_BLK:4 * COL_BLK + GQA_KV_HEADS * HEAD_DIM]
    vd = p[..., 4 * COL_BLK + GQA_KV_HEADS * HEAD_DIM:]
    b, s, _ = kb.shape
    kb = _split_halves(_heads(kb, DIFF_HEADS).astype(F32))
    kd = _rms(_heads(kd, GQA_KV_HEADS), k_gain)
    if ropes is not None:
        kb = _rope(kb, ropes[0], per_half=True)
        kd = _rope(kd, ropes[1])
    kb = kb.reshape(b, DIFF_HEADS // 2, 2, s, 2 * DIFF_QK_DIM).astype(BF16)
    vb = _with_ones(_heads(vb, DIFF_HEADS)).reshape(b, DIFF_HEADS // 2, 2, s, LANES)
    kd = kd.astype(BF16)[:, :, None]
    vd = _with_ones(_heads(vd, GQA_KV_HEADS))[:, :, None]
    return _heads(ka, NA_HEADS), _heads(va, NA_HEADS), kb, vb, kd, vd


LOG2E = math.log2(math.e)


def _mixer_branches(p, kv, kv_ctx, ropes, q_gain, bias_a, lam, lam_init, sub_g, sgu_g, sgu_w, sgu_bf):
    b, s, _ = p.shape
    ka, va, kb, vb, kd, vd = kv
    qa = _heads(_seg(p, SEG_QA), NA_HEADS).astype(F32)
    qb = _split_halves(_heads(_seg(p, SEG_QB), DIFF_HEADS).astype(F32))
    qd = _rms(_heads(_seg(p, SEG_QD), GQA_Q_HEADS), q_gain)
    if kv_ctx is None:
        qa6 = (qa * (HEAD_DIM ** -0.5 * LOG2E)).astype(BF16).reshape(b, NA_HEADS // 2, 2, 1, s, HEAD_DIM)
        ka5 = ka.reshape(b, NA_HEADS // 2, 2, s, HEAD_DIM)
        va5 = _with_ones(va).reshape(b, NA_HEADS // 2, 2, s, LANES)
        y_a = _flash(qa6, [ka5], [va5], tq=256, tk=256)
        kbs, vbs, kds, vds = [kb], [vb], [kd], [vd]
    else:
        qb = _rope(qb, ropes[0], per_half=True)
        qd = _rope(qd, ropes[1])
        kac, vac, kbc, vbc, kdc, vdc = kv_ctx
        y_a = _nbr((qa * HEAD_DIM ** -0.5).astype(BF16), ka, va, kac, vac, bias_a)
        kbs, vbs, kds, vds = [kbc, kb], [vbc, vb], [kdc, kd], [vdc, vd]
    qb = qb * (DIFF_QK_DIM ** -0.5 * LOG2E)
    zero = jnp.zeros_like(qb[..., 0, :])
    qb = jnp.stack([jnp.concatenate([qb[..., 0, :], zero], axis=-1),
                    jnp.concatenate([zero, qb[..., 1, :]], axis=-1)], axis=2)
    qb6 = qb.astype(BF16).reshape(b, DIFF_HEADS // 2, 2, 2, s, 2 * DIFF_QK_DIM)
    lam_v = jnp.full((1, DIFF_V_DIM), lam, F32)
    gain_v = (sub_g.astype(F32) * (1.0 - lam_init))[None, :]
    y_b = _flash(qb6, kbs, vbs, tq=256, tk=512, diff=(lam_v, gain_v))
    qd6 = (qd * (HEAD_DIM ** -0.5 * LOG2E)).astype(BF16).reshape(
        b, GQA_KV_HEADS, 1, GQA_Q_HEADS // GQA_KV_HEADS, s, HEAD_DIM)
    y_d = _flash(qd6, kds, vds, tq=512, tk=512)
    y_c = _sgu(p, sgu_g, sgu_w, sgu_bf, tt=1024)
    return [y_a, y_b, y_c, y_d]


def kernel(x, c, ctx, c_ctx, w_mod, b_mod, norm1_g, norm2_g, w_in, q_gain, k_gain, na_rpb, lambda_q1, lambda_k1, lambda_q2, lambda_k2, diff_sub_g, sgu_norm_g, sgu_w, sgu_b, gate_b, w_branch, w_out, w_router, w_e_gate, w_e_up, w_e_down, final_g):
    B, S, D = x.shape
    depth = w_mod.shape[0]
    rows = S // GRID_W
    ropes = (_rope_tables(S, DIFF_QK_DIM), _rope_tables(S, HEAD_DIM))
    hp = lax.Precision.HIGHEST
    fg = final_g.astype(F32)[None, :]
    for l in range(depth):
        last = l == depth - 1
        mod = jnp.dot(jax.nn.silu(c), w_mod[l], precision=hp) + b_mod[l]
        sh1, sc1, g1, sh2, sc2, g2 = (t[:, None, :] for t in jnp.split(mod, 6, axis=-1))
        mod_c = jnp.dot(jax.nn.silu(c_ctx), w_mod[l], precision=hp) + b_mod[l]
        csh1, csc1, cg1, csh2, csc2, cg2 = (jnp.broadcast_to(t[None, None, :], (B, 1, D))
                                            for t in jnp.split(mod_c, 6, axis=-1))
        lam_init = 0.8 - 0.6 * math.exp(-0.3 * l)
        lam = (jnp.exp(jnp.sum(lambda_q1[l].astype(F32) * lambda_k1[l].astype(F32)))
               - jnp.exp(jnp.sum(lambda_q2[l].astype(F32) * lambda_k2[l].astype(F32))) + lam_init)

        n_main = w_in.shape[-1] - 1280
        w_l = jnp.concatenate([w_in[l][:, n_main - GATE_W:n_main], w_in[l][:, :n_main - GATE_W],
                               w_in[l][:, n_main:]], axis=1).astype(BF16)
        bias_in = jnp.concatenate([gate_b[l].astype(F32), jnp.zeros((w_l.shape[1] - GATE_W,), F32)])[None, :]
        n1 = norm1_g[l].astype(F32)[None, None, :]
        n2 = norm2_g[l].astype(F32)[None, None, :]
        wb = w_branch[l].astype(BF16)
        wo = w_out[l].astype(BF16)
        wr_t = w_router[l].astype(F32).T
        sgu_g = sgu_norm_g[l].astype(F32)[None, :]
        sgu_wb = sgu_w[l].astype(BF16)
        sgu_bf = jnp.repeat(sgu_b[l].astype(F32).T, BRANCH_W // SGU_GROUPS, axis=1)
        bias_a = _nbr_bias(na_rpb[l], rows)

        p = _inproj(x, n1 * (1.0 + sc1), sh1, w_l, bias_in, GATE_W, tm=1024, tn=512)
        kv = _kv_heads(p[..., KV_COL0:], k_gain[l], ropes)
        if last:
            pc_kv = _inproj(ctx, n1 * (1.0 + csc1), csh1, w_l[:, KV_COL0:], bias_in[:, KV_COL0:], 0,
                            tm=256, tn=256)
            kv_c = _kv_heads(pc_kv, k_gain[l], None)
        else:
            pc = _inproj(ctx, n1 * (1.0 + csc1), csh1, w_l, bias_in, GATE_W, tm=256, tn=512)
            kv_c = _kv_heads(pc[..., KV_COL0:], k_gain[l], None)
            ys_c = _mixer_branches(pc, kv_c, None, None, q_gain[l], None, lam, lam_init, diff_sub_g[l],
                                   sgu_g, sgu_wb, sgu_bf)
            ctx, hc2, lg_c = _merge(ys_c, pc, wb, wo, ctx, cg1, n2 * (1.0 + csc2), csh2, wr_t, tm=256)
        ys = _mixer_branches(p, kv, kv_c, ropes, q_gain[l], bias_a, lam, lam_init, diff_sub_g[l],
                             sgu_g, sgu_wb, sgu_bf)
        x, h2, lg = _merge(ys, p, wb, wo, x, g1, n2 * (1.0 + sc2), sh2, wr_t, tm=512)

        x = _moe(h2, lg, x, g2, w_e_gate[l], w_e_up[l], w_e_down[l], fg, final=last)
        if not last:
            ctx = _moe(hc2, lg_c, ctx, cg2, w_e_gate[l], w_e_up[l], w_e_down[l], fg, final=False)
    return x
```

```python
import functools
import math

import jax
import jax.numpy as jnp
from jax import lax
from jax.experimental import pallas as pl
from jax.experimental.pallas import tpu as pltpu

F32 = jnp.float32
BF16 = jnp.bfloat16
I32 = jnp.int32

GRID_W = 64
HEAD_DIM = 64
N_BRANCH = 4
BRANCH_W = 256
NA_HEADS = 4
WIN_H = 8
WIN_W = 16
DIFF_HEADS = 4
DIFF_QK_DIM = 32
DIFF_V_DIM = 64
SGU_GROUPS = 4
SGU_CHUNK = 128
GQA_Q_HEADS = 4
GQA_KV_HEADS = 2
ROPE_THETA = 10000.0
N_EXPERTS = 16
CAPACITY_FACTOR = 2
EPS = 1e-6
NEG = -1e30

GATE_W = N_BRANCH * 1024
COL_BLK = 256
SEG_QA, SEG_QB, SEG_QD, SEG_U, SEG_V, SEG_KA, SEG_VA, SEG_KB, SEG_VB = range(16, 25)
COL_KD = GATE_W + 9 * COL_BLK
KV_COL0 = GATE_W + 5 * COL_BLK

VMEM_LIMIT = 56 * 1024 * 1024

NBR_ROWS = 8
NBR_KROWS = 16
MOE_CHUNK = 256


def _cparams(sem):
    return pltpu.CompilerParams(dimension_semantics=sem, vmem_limit_bytes=VMEM_LIMIT)


def _dot_nt(a, b):
    return lax.dot_general(a, b, (((1,), (1,)), ((), ())), preferred_element_type=F32)


def _inproj_kernel(x_ref, mult_ref, shift_ref, w_ref, bias_ref, o_ref, h_scr, *, n_gate_cols):
    j = pl.program_id(2)
    tn = o_ref.shape[-1]

    @pl.when(j == 0)
    def _():
        x = x_ref[0]
        ms = jnp.mean(x * x, axis=-1, keepdims=True)
        h = x * lax.rsqrt(ms + EPS) * mult_ref[0] + shift_ref[0]
        h_scr[...] = h.astype(BF16)

    acc = jnp.dot(h_scr[...], w_ref[...], preferred_element_type=F32) + bias_ref[...]

    @pl.when(j * tn < n_gate_cols)
    def _():
        col = j * tn + lax.broadcasted_iota(I32, acc.shape, 1)
        o_ref[0] = jnp.where(col < n_gate_cols, 1.0 / (1.0 + jnp.exp(-acc)), acc).astype(o_ref.dtype)

    @pl.when(j * tn >= n_gate_cols)
    def _():
        o_ref[0] = acc.astype(o_ref.dtype)


def _inproj(x, mult, shift, w, bias, n_gate_cols, tm, tn):
    B, S, D = x.shape
    N = w.shape[1]
    tm = min(tm, S)
    return pl.pallas_call(
        functools.partial(_inproj_kernel, n_gate_cols=n_gate_cols),
        grid=(B, S // tm, N // tn),
        in_specs=[
            pl.BlockSpec((1, tm, D), lambda b, i, j: (b, i, 0)),
            pl.BlockSpec((1, 1, D), lambda b, i, j: (b, 0, 0)),
            pl.BlockSpec((1, 1, D), lambda b, i, j: (b, 0, 0)),
            pl.BlockSpec((D, tn), lambda b, i, j: (0, j)),
            pl.BlockSpec((1, tn), lambda b, i, j: (0, j)),
        ],
        out_specs=pl.BlockSpec((1, tm, tn), lambda b, i, j: (b, i, j)),
        out_shape=jax.ShapeDtypeStruct((B, S, N), BF16),
        scratch_shapes=[pltpu.VMEM((tm, D), BF16)],
        compiler_params=_cparams(("parallel", "parallel", "arbitrary")),
        name="inproj",
    )(x, mult, shift, w, bias)


LANES = 128


def _flash_kernel(*refs, nseg, tks, J, R, dv, diff):
    if diff:
        lam_ref, subg_ref = refs[:2]
        refs = refs[2:]
    q_ref = refs[0]
    k_refs = refs[1:1 + nseg]
    v_refs = refs[1 + nseg:1 + 2 * nseg]
    o_ref = refs[1 + 2 * nseg]
    m_scr, acc_scr = refs[2 + 2 * nseg:4 + 2 * nseg]
    s_scrs = refs[4 + 2 * nseg:]
    tq, d = q_ref.shape[-2:]
    rows = R * tq
    qs = [q_ref[0, 0, j].reshape(rows, d) for j in range(J)]
    m_scr[...] = jnp.full(m_scr.shape, NEG, F32)
    acc_scr[...] = jnp.zeros(acc_scr.shape, F32)

    def scores(j, k_ref, c, tk):
        return _dot_nt(qs[j], k_ref[0, 0, j, pl.ds(pl.multiple_of(c * tk, tk), tk), :])

    def accumulate(j, s, v_ref, c, tk):
        vc = v_ref[0, 0, j, pl.ds(pl.multiple_of(c * tk, tk), tk), :]
        slabs = [s[:, t * LANES:(t + 1) * LANES] for t in range(tk // LANES)]
        m_cur = functools.reduce(jnp.maximum, slabs)
        m_prev = m_scr[j]
        m_new = jnp.maximum(m_prev, jnp.max(m_cur, axis=1, keepdims=True))
        alpha = jnp.exp2(m_prev - m_new)
        p = jnp.concatenate([jnp.exp2(sl - m_new) for sl in slabs], axis=1).astype(BF16)
        acc_scr[j] = alpha * acc_scr[j] + jnp.dot(p, vc, preferred_element_type=F32)
        m_scr[j] = m_new

    if nseg == 1:
        k_ref, v_ref, tk = k_refs[0], v_refs[0], tks[0]

        def body(c, carry):
            for j in range(J):
                accumulate(j, scores(j, k_ref, c, tk), v_ref, c, tk)
            return carry

        lax.fori_loop(0, k_ref.shape[3] // tk, body, 0)
    else:
        (kc_ref, k_ref), (vc_ref, v_ref), (tkc, tk) = k_refs, v_refs, tks
        nchunks = k_ref.shape[3] // tk
        assert kc_ref.shape[3] == tkc and nchunks % 2 == 0
        s_a = s_scrs[:J]
        s_b = s_scrs[J:]
        for j in range(J):
            s_b[j][...] = scores(j, k_ref, 0, tk)
            accumulate(j, scores(j, kc_ref, 0, tkc), vc_ref, 0, tkc)

        def body(i, carry):
            c = 2 * i
            for j in range(J):
                s_a[j][...] = scores(j, k_ref, c + 1, tk)
                accumulate(j, s_b[j][...], v_ref, c, tk)
            for j in range(J):
                s_b[j][...] = scores(j, k_ref, c + 2, tk)
                accumulate(j, s_a[j][...], v_ref, c + 1, tk)
            return carry

        lax.fori_loop(0, nchunks // 2 - 1, body, 0)
        for j in range(J):
            s_a[j][...] = scores(j, k_ref, nchunks - 1, tk)
            accumulate(j, s_b[j][...], v_ref, nchunks - 2, tk)
        for j in range(J):
            accumulate(j, s_a[j][...], v_ref, nchunks - 1, tk)
    pieces = []
    for j in range(J):
        acc = acc_scr[j]
        o = acc[:, :dv] / acc[:, dv:dv + 1]
        parts = [o[r * tq:(r + 1) * tq] for r in range(R)]
        if diff:
            y = parts[0] - lam_ref[...] * parts[1]
            y = y * lax.rsqrt(jnp.mean(y * y, axis=-1, keepdims=True) + EPS) * subg_ref[...]
            pieces.append(y)
        else:
            pieces.extend(parts)
    o_ref[0] = jnp.concatenate(pieces, axis=1).astype(o_ref.dtype)


def _flash(q, ks, vs, tq, tk, diff=None):
    B, Hs, J, R, Sq, d = q.shape
    dv = LANES // 2
    assert (J if diff else J * R) * dv == LANES
    tq = min(tq, Sq)
    tks = tuple(min(tk, k.shape[3]) for k in ks)
    in_specs = []
    if diff:
        in_specs += [pl.BlockSpec((1, dv), lambda b, h, i: (0, 0))] * 2
    in_specs.append(pl.BlockSpec((1, 1, J, R, tq, d), lambda b, h, i: (b, h, 0, 0, i, 0)))
    for k in ks:
        in_specs.append(pl.BlockSpec((1, 1, J, k.shape[3], d), lambda b, h, i: (b, h, 0, 0, 0)))
    for v in vs:
        in_specs.append(pl.BlockSpec((1, 1, J, v.shape[3], LANES), lambda b, h, i: (b, h, 0, 0, 0)))
    return pl.pallas_call(
        functools.partial(_flash_kernel, nseg=len(ks), tks=tks, J=J, R=R, dv=dv, diff=bool(diff)),
        grid=(B, Hs, Sq // tq),
        in_specs=in_specs,
        out_specs=pl.BlockSpec((1, tq, LANES), lambda b, h, i: (b, i, h)),
        out_shape=jax.ShapeDtypeStruct((B, Sq, Hs * LANES), BF16),
        scratch_shapes=[pltpu.VMEM((J, R * tq, LANES), F32), pltpu.VMEM((J, R * tq, LANES), F32)]
        + ([pltpu.VMEM((R * tq, tks[1]), F32)] * (2 * J) if len(ks) == 2 else []),
        compiler_params=_cparams(("parallel", "parallel", "arbitrary")),
        name="flash_diff" if diff else "flash",
    )(*(diff or ()), q, *ks, *vs)


NBR_HEADS = 2


def _nbr_kernel(q_ref, k0, k1, k2, k3, v0, v1, v2, v3, kc_ref, vc_ref, bias_ref, o_ref):
    outs = []
    for j in range(NBR_HEADS):
        q = q_ref[0, j]
        k = jnp.concatenate([k0[0, j], k1[0, j], k2[0, j], k3[0, j]], axis=0)
        v = jnp.concatenate([v0[0, j], v1[0, j], v2[0, j], v3[0, j]], axis=0)
        s_loc = _dot_nt(q, k) + bias_ref[0, j]
        s_ctx = _dot_nt(q, kc_ref[0, j])
        m = jnp.maximum(jnp.max(s_loc, axis=1, keepdims=True), jnp.max(s_ctx, axis=1, keepdims=True))
        p_loc = jnp.exp(s_loc - m)
        p_ctx = jnp.exp(s_ctx - m)
        l = jnp.sum(p_loc, axis=1, keepdims=True) + jnp.sum(p_ctx, axis=1, keepdims=True)
        o = (jnp.dot(p_ctx.astype(BF16), vc_ref[0, j], preferred_element_type=F32)
             + jnp.dot(p_loc.astype(BF16), v, preferred_element_type=F32))
        outs.append(o / l)
    o_ref[0] = jnp.concatenate(outs, axis=1).astype(o_ref.dtype)


def _nbr_bias(rpb, rows):
    kh, kw = WIN_H, WIN_W
    qc = jnp.arange(GRID_W)
    kc = jnp.arange(GRID_W)
    c0 = jnp.clip(qc - kw // 2, 0, GRID_W - kw)
    col_ok = (kc[None, :] >= c0[:, None]) & (kc[None, :] < c0[:, None] + kw)
    dj = kc[None, :] - qc[:, None] + WIN_W - 1
    oh_c = ((dj[..., None] == jnp.arange(2 * WIN_W - 1)) & col_ok[..., None]).astype(F32)

    def variant(r_start, k_start):
        r = r_start + jnp.arange(NBR_ROWS)
        kr = k_start + jnp.arange(NBR_KROWS)
        r0 = jnp.clip(r - kh // 2, 0, rows - kh)
        row_ok = (kr[None, :] >= r0[:, None]) & (kr[None, :] < r0[:, None] + kh)
        di = kr[None, :] - r[:, None] + WIN_H - 1
        oh_r = ((di[..., None] == jnp.arange(2 * WIN_H - 1)) & row_ok[..., None]).astype(F32)
        b = jnp.einsum('jka,hab,qcb->hjqkc', oh_r, rpb.astype(F32), oh_c, precision=lax.Precision.HIGHEST)
        valid = row_ok[:, None, :, None] & col_ok[None, :, None, :]
        b = jnp.where(valid[None], b, NEG)
        return b.reshape(rpb.shape[0], NBR_ROWS * GRID_W, NBR_KROWS * GRID_W)

    return jnp.stack([variant(0, 0), variant(NBR_ROWS, NBR_ROWS - kh // 2),
                      variant(rows - NBR_ROWS, rows - NBR_KROWS)])


def _nbr(q, k, v, kc, vc, bias):
    B, H, S, d = q.shape
    L = kc.shape[2]
    tq = NBR_ROWS * GRID_W
    kb = tq // 2
    nb = S // tq
    nh = NBR_HEADS
    assert S % tq == 0 and nb >= 3 and nh * d == LANES

    def kmap(j):
        return lambda h, i, b: (b, h, jnp.clip(2 * i - 1, 0, 2 * nb - 4) + j, 0)

    def bmap(h, i, b):
        return (jnp.where(i == 0, 0, jnp.where(i == nb - 1, 2, 1)), h, 0, 0)

    kv_specs = [pl.BlockSpec((1, nh, kb, d), kmap(j)) for j in range(4)]
    return pl.pallas_call(
        _nbr_kernel,
        grid=(H // nh, nb, B),
        in_specs=[pl.BlockSpec((1, nh, tq, d), lambda h, i, b: (b, h, i, 0))] + kv_specs + kv_specs + [
            pl.BlockSpec((1, nh, L, d), lambda h, i, b: (b, h, 0, 0)),
            pl.BlockSpec((1, nh, L, d), lambda h, i, b: (b, h, 0, 0)),
            pl.BlockSpec((1, nh, tq, NBR_KROWS * GRID_W), bmap),
        ],
        out_specs=pl.BlockSpec((1, tq, LANES), lambda h, i, b: (b, i, h)),
        out_shape=jax.ShapeDtypeStruct((B, S, H * d), BF16),
        compiler_params=_cparams(("parallel", "parallel", "parallel")),
        name="nbr_attn",
    )(q, k, k, k, k, v, v, v, v, kc, vc, bias)


def _gelu(x):
    return 0.5 * x * (1.0 + jnp.tanh(math.sqrt(2.0 / math.pi) * (x + 0.044715 * (x * x * x))))


def _sgu_kernel(u_ref, v_ref, g_ref, w_ref, b_ref, o_ref, *, nchunk):
    grp = lax.broadcasted_iota(I32, (SGU_CHUNK, BRANCH_W), 1) // (BRANCH_W // SGU_GROUPS)
    for c in range(nchunk):
        rows = slice(c * SGU_CHUNK, (c + 1) * SGU_CHUNK)
        u = u_ref[0, rows, :].astype(F32)
        v = _gelu(v_ref[0, rows, :].astype(F32))
        mu = jnp.mean(v, axis=-1, keepdims=True)
        var = jnp.mean(jnp.square(v - mu), axis=-1, keepdims=True)
        vn = ((v - mu) * lax.rsqrt(var + EPS) * g_ref[...]).astype(BF16)
        mixed = b_ref[...]
        for g in range(SGU_GROUPS):
            mg = jnp.dot(w_ref[g], vn, preferred_element_type=F32)
            mixed = mixed + jnp.where(grp == g, mg, 0.0)
        o_ref[0, rows, :] = (_gelu(u) * mixed).astype(o_ref.dtype)


def _sgu(p, norm_g, w_s, b_full, tt):
    B, S, _ = p.shape
    tt = min(tt, S)
    return pl.pallas_call(
        functools.partial(_sgu_kernel, nchunk=tt // SGU_CHUNK),
        grid=(B, S // tt),
        in_specs=[
            pl.BlockSpec((1, tt, BRANCH_W), lambda b, i: (b, i, SEG_U)),
            pl.BlockSpec((1, tt, BRANCH_W), lambda b, i: (b, i, SEG_V)),
            pl.BlockSpec((1, BRANCH_W), lambda b, i: (0, 0)),
            pl.BlockSpec((SGU_GROUPS, SGU_CHUNK, SGU_CHUNK), lambda b, i: (0, 0, 0)),
            pl.BlockSpec((SGU_CHUNK, BRANCH_W), lambda b, i: (0, 0)),
        ],
        out_specs=pl.BlockSpec((1, tt, BRANCH_W), lambda b, i: (b, i, 0)),
        out_shape=jax.ShapeDtypeStruct((B, S, BRANCH_W), BF16),
        compiler_params=_cparams(("parallel", "parallel")),
        name="sgu",
    )(p, p, norm_g, w_s, b_full)


def _merge_kernel(ya, yb, yc, yd, g0, g1, g2, g3, wb_ref, wo_ref, x_ref, gate_ref, mult_ref, shift_ref,
                  wr_ref, xo_ref, h2_ref, lg_ref):
    mix = None
    for n, (y, g) in enumerate(((ya, g0), (yb, g1), (yc, g2), (yd, g3))):
        pr = jnp.dot(y[0], wb_ref[n], preferred_element_type=F32)
        t = g[0].astype(F32) * pr
        mix = t if mix is None else mix + t
    out = jnp.dot(mix.astype(BF16), wo_ref[...], preferred_element_type=F32)
    xn = x_ref[0] + gate_ref[0] * out
    xo_ref[0] = xn
    ms = jnp.mean(xn * xn, axis=-1, keepdims=True)
    h2 = xn * lax.rsqrt(ms + EPS) * mult_ref[0] + shift_ref[0]
    h2_ref[0] = h2.astype(BF16)
    lg_ref[0] = lax.dot_general(wr_ref[...], h2, (((1,), (1,)), ((), ())),
                                precision=lax.Precision.HIGHEST, preferred_element_type=F32)


def _merge(ys, p, w_branch, w_out, x, gate, mult2, shift2, w_router_t, tm):
    B, S, D = x.shape
    E = w_router_t.shape[0]
    tm = min(tm, S)
    y_spec = pl.BlockSpec((1, tm, BRANCH_W), lambda b, i: (b, i, 0))
    g_specs = [pl.BlockSpec((1, tm, D), functools.partial(lambda b, i, n: (b, i, n), n=n)) for n in range(N_BRANCH)]
    vec = pl.BlockSpec((1, 1, D), lambda b, i: (b, 0, 0))
    return pl.pallas_call(
        _merge_kernel,
        grid=(B, S // tm),
        in_specs=[y_spec] * 4 + g_specs + [
            pl.BlockSpec((N_BRANCH, BRANCH_W, D), lambda b, i: (0, 0, 0)),
            pl.BlockSpec((D, D), lambda b, i: (0, 0)),
            pl.BlockSpec((1, tm, D), lambda b, i: (b, i, 0)),
            vec, vec, vec,
            pl.BlockSpec((E, D), lambda b, i: (0, 0)),
        ],
        out_specs=[
            pl.BlockSpec((1, tm, D), lambda b, i: (b, i, 0)),
            pl.BlockSpec((1, tm, D), lambda b, i: (b, i, 0)),
            pl.BlockSpec((1, E, tm), lambda b, i: (b, 0, i)),
        ],
        out_shape=[
            jax.ShapeDtypeStruct((B, S, D), F32),
            jax.ShapeDtypeStruct((B, S, D), BF16),
            jax.ShapeDtypeStruct((B, E, S), F32),
        ],
        compiler_params=_cparams(("parallel", "parallel")),
        name="merge",
    )(*ys, p, p, p, p, w_branch, w_out, x, gate, mult2, shift2, w_router_t)


def _cumsum_excl(x, tri):
    n = x.shape[1]
    outs = []
    carry = jnp.zeros((x.shape[0], 1), F32)
    for c in range(n // 128):
        xc = x[:, c * 128:(c + 1) * 128]
        outs.append(jnp.dot(xc.astype(BF16), tri, preferred_element_type=F32) + carry)
        carry = carry + jnp.sum(xc, axis=1, keepdims=True)
    return jnp.concatenate(outs, axis=1)


def _route_kernel(lg_ref, rank_ref, score_ref, *, cap):
    lg = lg_ref[0]
    mx = jnp.max(lg, axis=0, keepdims=True)
    ex = jnp.exp(lg - mx)
    aff = ex / jnp.sum(ex, axis=0, keepdims=True)
    E = lg.shape[0]

    def bisect(i, thr_bits):
        cand = thr_bits | jnp.left_shift(jnp.int32(1), 30 - i)
        cnt = jnp.sum(jnp.where(aff >= pltpu.bitcast(cand, F32), 1, 0), axis=1, keepdims=True)
        return jnp.where(cnt >= cap, cand, thr_bits)

    thr = pltpu.bitcast(lax.fori_loop(0, 31, bisect, jnp.zeros((E, 1), I32)), F32)
    gt = aff > thr
    eq = aff == thr
    need = (cap - jnp.sum(jnp.where(gt, 1, 0), axis=1, keepdims=True)).astype(F32)
    ri = lax.broadcasted_iota(I32, (128, 128), 0)
    ci = lax.broadcasted_iota(I32, (128, 128), 1)
    tri = jnp.where(ri < ci, 1.0, 0.0).astype(BF16)
    eq_before = _cumsum_excl(jnp.where(eq, 1.0, 0.0), tri)
    sel = gt | (eq & (eq_before < need))
    rank = _cumsum_excl(jnp.where(sel, 1.0, 0.0), tri)
    sel = sel & (rank < cap)
    rank_ref[0] = jnp.where(sel, rank.astype(I32), -1)
    score_ref[0] = jnp.where(sel, aff, 0.0)


def _route(logits, cap):
    B, E, n = logits.shape
    spec = pl.BlockSpec((1, E, n), lambda b: (b, 0, 0))
    return pl.pallas_call(
        functools.partial(_route_kernel, cap=cap),
        grid=(B,),
        in_specs=[spec],
        out_specs=[spec, spec],
        out_shape=[jax.ShapeDtypeStruct((B, E, n), I32), jax.ShapeDtypeStruct((B, E, n), F32)],
        compiler_params=_cparams(("parallel",)),
        name="route",
    )(logits)


def _gather_kernel(cnt_ref, rank_ref, h_ref, o_ref, *, nc, W):
    b = pl.program_id(0)
    e = pl.program_id(2)
    base = (b * pl.num_programs(2) + e) * (nc + 1)
    o_ref[...] = jnp.zeros(o_ref.shape, o_ref.dtype)
    T = MOE_CHUNK

    def cbody(c, carry):
        lo = cnt_ref[base + c]
        hi = cnt_ref[base + c + 1]
        r = rank_ref[0, 0, pl.ds(c, 1), :]
        hc = h_ref[0, pl.ds(pl.multiple_of(c * T, T), T), :]

        def wbody(w, carry2):
            ws = pl.multiple_of(w * W, W)
            slot = lax.broadcasted_iota(I32, (W, T), 0) + ws
            oh = jnp.where(slot == r, 1.0, 0.0).astype(BF16)
            got = jnp.dot(oh, hc, preferred_element_type=F32)
            o_ref[0, 0, pl.ds(ws, W), :] = o_ref[0, 0, pl.ds(ws, W), :] + got.astype(o_ref.dtype)
            return carry2

        lax.fori_loop(lo // W, (hi + W - 1) // W, wbody, 0)
        return carry

    lax.fori_loop(0, nc, cbody, 0)


def _gather(cnt, rank, h, cap):
    B, E, n = rank.shape
    D = h.shape[-1]
    nc = n // MOE_CHUNK
    W = min(128, cap)
    ds = 2
    return pl.pallas_call(
        functools.partial(_gather_kernel, nc=nc, W=W),
        grid_spec=pltpu.PrefetchScalarGridSpec(
            num_scalar_prefetch=1,
            grid=(B, ds, E),
            in_specs=[
                pl.BlockSpec((1, 1, nc, MOE_CHUNK), lambda b, d, e, cnt: (b, e, 0, 0)),
                pl.BlockSpec((1, n, D // ds), lambda b, d, e, cnt: (b, 0, d)),
            ],
            out_specs=pl.BlockSpec((1, 1, cap, D // ds), lambda b, d, e, cnt: (b, e, 0, d)),
        ),
        out_shape=jax.ShapeDtypeStruct((B, E, cap, D), BF16),
        compiler_params=_cparams(("parallel", "parallel", "arbitrary")),
        name="moe_gather",
    )(cnt, rank.reshape(B, E, nc, MOE_CHUNK), h)


def _ffn_kernel(x_ref, wg_ref, wu_ref, wd_ref, o_ref, acc_scr):
    f = pl.program_id(1)
    nb, _, cap, D = x_ref.shape

    @pl.when(f == 0)
    def _():
        acc_scr[...] = jnp.zeros(acc_scr.shape, F32)

    wg = wg_ref[0, 0].astype(BF16)
    wu = wu_ref[0, 0].astype(BF16)
    wd = wd_ref[0, 0].astype(BF16)
    for b in range(nb):
        x = x_ref[b, 0]
        g = jnp.dot(x, wg, preferred_element_type=F32)
        u = jnp.dot(x, wu, preferred_element_type=F32)
        hid = (g * (1.0 / (1.0 + jnp.exp(-g))) * u).astype(BF16)
        acc_scr[b] = acc_scr[b] + jnp.dot(hid, wd, preferred_element_type=F32)

    @pl.when(f == pl.num_programs(1) - 1)
    def _():
        o_ref[:, 0] = acc_scr[...].astype(o_ref.dtype)


def _ffn(xin, w_gate, w_up, w_down, layer, tf):
    B, E, cap, D = xin.shape
    Fh = w_gate.shape[-1]
    return pl.pallas_call(
        _ffn_kernel,
        grid=(E, Fh // tf),
        in_specs=[
            pl.BlockSpec((B, 1, cap, D), lambda e, f: (0, e, 0, 0)),
            pl.BlockSpec((1, 1, D, tf), lambda e, f: (layer, e, 0, f)),
            pl.BlockSpec((1, 1, D, tf), lambda e, f: (layer, e, 0, f)),
            pl.BlockSpec((1, 1, tf, D), lambda e, f: (layer, e, f, 0)),
        ],
        out_specs=pl.BlockSpec((B, 1, cap, D), lambda e, f: (0, e, 0, 0)),
        out_shape=jax.ShapeDtypeStruct((B, E, cap, D), BF16),
        scratch_shapes=[pltpu.VMEM((B, cap, D), F32)],
        compiler_params=_cparams(("parallel", "arbitrary")),
        name="moe_ffn",
    )(xin, w_gate, w_up, w_down)


def _scatter_kernel(cnt_ref, rank_ref, score_ref, y_ref, x_ref, g2_ref, fg_ref, o_ref, acc_scr, *, nc, W, final):
    b = pl.program_id(0)
    i = pl.program_id(1)
    e = pl.program_id(2)
    E = pl.num_programs(2)
    T = MOE_CHUNK
    nsub = acc_scr.shape[0] // T
    base = (b * E + e) * (nc + 1) + i * nsub

    @pl.when(e == 0)
    def _():
        acc_scr[...] = jnp.zeros(acc_scr.shape, F32)

    lane = lax.broadcasted_iota(I32, (T, E), 1)
    for c in range(nsub):
        rows = slice(c * T, (c + 1) * T)
        lo = cnt_ref[base + c]
        hi = cnt_ref[base + c + 1]
        rc = jnp.sum(jnp.where(lane == e, rank_ref[0, rows, :], 0), axis=1, keepdims=True)
        sc = jnp.sum(jnp.where(lane == e, score_ref[0, rows, :], 0.0), axis=1, keepdims=True)

        def wbody(w, carry, rc=rc, sc=sc, rows=rows):
            ws = pl.multiple_of(w * W, W)
            slot = lax.broadcasted_iota(I32, (T, W), 1) + ws
            oh = jnp.where(slot == rc, 1.0, 0.0).astype(BF16)
            got = jnp.dot(oh, y_ref[0, 0, pl.ds(ws, W), :], preferred_element_type=F32)
            acc_scr[rows, :] = acc_scr[rows, :] + sc * got
            return carry

        lax.fori_loop(lo // W, (hi + W - 1) // W, wbody, 0)

    @pl.when(e == E - 1)
    def _():
        xn = x_ref[0] + g2_ref[0] * acc_scr[...]
        if final:
            ms = jnp.mean(xn * xn, axis=-1, keepdims=True)
            xn = xn * lax.rsqrt(ms + EPS) * fg_ref[...]
        o_ref[0] = xn


def _scatter(cnt, rank_tm, score_tm, y, x, g2, final_g, tt, final):
    B, n, E = rank_tm.shape
    cap, D = y.shape[2], y.shape[3]
    tt = min(tt, n)
    nc = n // MOE_CHUNK
    W = min(128, cap)
    return pl.pallas_call(
        functools.partial(_scatter_kernel, nc=nc, W=W, final=final),
        grid_spec=pltpu.PrefetchScalarGridSpec(
            num_scalar_prefetch=1,
            grid=(B, n // tt, E),
            in_specs=[
                pl.BlockSpec((1, tt, E), lambda b, i, e, cnt: (b, i, 0)),
                pl.BlockSpec((1, tt, E), lambda b, i, e, cnt: (b, i, 0)),
                pl.BlockSpec((1, 1, cap, D), lambda b, i, e, cnt: (b, e, 0, 0)),
                pl.BlockSpec((1, tt, D), lambda b, i, e, cnt: (b, i, 0)),
                pl.BlockSpec((1, 1, D), lambda b, i, e, cnt: (b, 0, 0)),
                pl.BlockSpec((1, D), lambda b, i, e, cnt: (0, 0)),
            ],
            out_specs=pl.BlockSpec((1, tt, D), lambda b, i, e, cnt: (b, i, 0)),
            scratch_shapes=[pltpu.VMEM((tt, D), F32)],
        ),
        out_shape=jax.ShapeDtypeStruct((B, n, D), F32),
        compiler_params=_cparams(("parallel", "parallel", "arbitrary")),
        name="moe_scatter",
    )(cnt, rank_tm, score_tm, y, x, g2, final_g)


def _moe(h2, logits, x, g2, w_gate, w_up, w_down, layer, final_g, final):
    B, n, D = h2.shape
    E = logits.shape[1]
    cap = CAPACITY_FACTOR * n // N_EXPERTS
    rank, score = _route(logits, cap)
    nc = n // MOE_CHUNK
    per_chunk = jnp.sum((rank >= 0).reshape(B, E, nc, MOE_CHUNK), axis=-1, dtype=I32)
    cnt = jnp.concatenate([jnp.zeros((B, E, 1), I32), jnp.cumsum(per_chunk, axis=-1, dtype=I32)], axis=-1)
    cnt = cnt.reshape(-1)
    xin = _gather(cnt, rank, h2, cap)
    y = _ffn(xin, w_gate, w_up, w_down, layer, tf=256)
    return _scatter(cnt, jnp.swapaxes(rank, 1, 2), jnp.swapaxes(score, 1, 2), y, x, g2, final_g,
                    tt=1024, final=final)


LOG2E = math.log2(math.e)
HEADS_PER_BLK = COL_BLK // HEAD_DIM


def _prep_kernel(*refs, rope, qa_scale):
    qa_r, qb_r, qd_r, ka_r, va_r, kb_r, vb_r, kvd_r = refs[:8]
    refs = refs[8:]
    if rope:
        cb_r, sb_r, cd_r, sd_r = refs[:4]
        refs = refs[4:]
    qg_r, kg_r, gm_r = refs[:3]
    qa_o, qb_o, qd_o, ka_o, va_o, kb_o, vb_o, kd_o, vd_o = refs[3:]
    ts = qa_r.shape[1]
    kvw = GQA_KV_HEADS * HEAD_DIM

    def partner(x, half):
        n = x.shape[1]
        lane = lax.broadcasted_iota(I32, x.shape, 1)
        return jnp.where((lane & half) == 0, pltpu.roll(x, n - half, 1), pltpu.roll(x, half, 1))

    def rot(x, cos, sin, half):
        return x * cos + partner(x, half) * sin

    def group_norm(x, gm, gain):
        ms = jnp.dot(x * x, gm, precision=lax.Precision.HIGHEST, preferred_element_type=F32)
        return x * lax.rsqrt(ms + EPS) * gain

    def head(x, h):
        return x[:, h * HEAD_DIM:(h + 1) * HEAD_DIM]

    lane64 = lax.broadcasted_iota(I32, (ts, HEAD_DIM), 1)
    ones_tail = jnp.where(lane64 == 0, 1.0, 0.0).astype(BF16)

    qa = qa_r[0].astype(F32) * qa_scale
    for h in range(HEADS_PER_BLK):
        qa_o[0, h] = head(qa, h).astype(BF16)
        ka_o[0, h] = head(ka_r[0], h)
        va_o[0, h] = head(va_r[0], h)

    qb = qb_r[0].astype(F32)
    kb = kb_r[0].astype(F32)
    if rope:
        qb = rot(qb, cb_r[...], sb_r[...], DIFF_QK_DIM // 2)
        kb = rot(kb, cb_r[...], sb_r[...], DIFF_QK_DIM // 2)
    qb = qb * (DIFF_QK_DIM ** -0.5 * LOG2E)
    for h in range(HEADS_PER_BLK):
        xh = head(qb, h)
        qb_o[0, h // 2, h % 2, 0] = jnp.where(lane64 < DIFF_QK_DIM, xh, 0.0).astype(BF16)
        qb_o[0, h // 2, h % 2, 1] = jnp.where(lane64 >= DIFF_QK_DIM, xh, 0.0).astype(BF16)
        kb_o[0, h // 2, h % 2] = head(kb, h).astype(BF16)
        vb_o[0, h // 2, h % 2] = jnp.concatenate([head(vb_r[0], h), ones_tail], axis=1)

    qd = group_norm(qd_r[0].astype(F32), gm_r[...], qg_r[...])
    kd = group_norm(kvd_r[0, :, :kvw].astype(F32), gm_r[:kvw, :kvw], kg_r[...])
    if rope:
        qd = rot(qd, cd_r[...], sd_r[...], HEAD_DIM // 2)
        kd = rot(kd, cd_r[:, :kvw], sd_r[:, :kvw], HEAD_DIM // 2)
    qd = qd * (HEAD_DIM ** -0.5 * LOG2E)
    grp = GQA_Q_HEADS // GQA_KV_HEADS
    for h in range(GQA_Q_HEADS):
        qd_o[0, h // grp, 0, h % grp] = head(qd, h).astype(BF16)
    vd = kvd_r[0, :, kvw:]
    for h in range(GQA_KV_HEADS):
        kd_o[0, h, 0] = head(kd, h).astype(BF16)
        vd_o[0, h, 0] = jnp.concatenate([head(vd, h), ones_tail], axis=1)


def _prep(p, tables, qg, kg, gm, qa_scale, ts):
    B, S, _ = p.shape
    ts = min(ts, S)

    def seg(blk):
        return pl.BlockSpec((1, ts, COL_BLK), lambda b, i: (b, i, blk))

    in_specs = [seg(s) for s in (SEG_QA, SEG_QB, SEG_QD, SEG_KA, SEG_VA, SEG_KB, SEG_VB, SEG_VB + 1)]
    args = [p] * 8
    if tables is not None:
        in_specs += [pl.BlockSpec((ts, COL_BLK), lambda b, i: (i, 0))] * 4
        args += list(tables)
    in_specs += [pl.BlockSpec(a.shape, lambda b, i: (0, 0)) for a in (qg, kg, gm)]
    args += [qg, kg, gm]

    def out(lead, width):
        shape = (B,) + lead + (S, width)
        nl = len(lead)
        spec = pl.BlockSpec((1,) + lead + (ts, width), lambda b, i: (b,) + (0,) * nl + (i, 0))
        return jax.ShapeDtypeStruct(shape, BF16), spec

    grp = GQA_Q_HEADS // GQA_KV_HEADS
    outs = [out((NA_HEADS,), HEAD_DIM), out((DIFF_HEADS // 2, 2, 2), HEAD_DIM),
            out((GQA_KV_HEADS, 1, grp), HEAD_DIM), out((NA_HEADS,), HEAD_DIM), out((NA_HEADS,), HEAD_DIM),
            out((DIFF_HEADS // 2, 2), HEAD_DIM), out((DIFF_HEADS // 2, 2), LANES),
            out((GQA_KV_HEADS, 1), HEAD_DIM), out((GQA_KV_HEADS, 1), LANES)]
    return pl.pallas_call(
        functools.partial(_prep_kernel, rope=tables is not None, qa_scale=qa_scale),
        grid=(B, S // ts),
        in_specs=in_specs,
        out_specs=[o[1] for o in outs],
        out_shape=[o[0] for o in outs],
        compiler_params=_cparams(("parallel", "parallel")),
        name="attn_prep",
    )(*args)


def _rope_lane_tables(n, dim):
    t = jnp.arange(n)
    row = (t // GRID_W).astype(F32)
    col = (t % GRID_W).astype(F32)
    n_pairs = dim // 4
    inv = ROPE_THETA ** (-jnp.arange(n_pairs, dtype=F32) / n_pairs)
    ang = jnp.concatenate([row[:, None] * inv, col[:, None] * inv], axis=-1)
    cos, sin = jnp.cos(ang), jnp.sin(ang)
    reps = COL_BLK // dim
    return jnp.tile(jnp.concatenate([cos, cos], -1), (1, reps)), jnp.tile(jnp.concatenate([-sin, sin], -1), (1, reps))


def _deinterleave(w, width):
    lead = w.shape[:-1]
    n = w.shape[-1]
    return jnp.swapaxes(w.reshape(lead + (n // width, width // 2, 2)), -1, -2).reshape(lead + (n,))


def _with_ones(v):
    pad = [(0, 0)] * (v.ndim - 1) + [(0, LANES - v.shape[-1] - 1)]
    return jnp.pad(jnp.concatenate([v, jnp.ones(v.shape[:-1] + (1,), v.dtype)], axis=-1), pad)


def _mixer_branches(p, ops, ops_ctx, bias_a, lam, lam_init, sub_g, sgu_g, sgu_w, sgu_bf):
    qa, qb, qd, ka, va, kb, vb, kd, vd = ops
    b, _, s, _ = qa.shape
    if ops_ctx is None:
        y_a = _flash(qa.reshape(b, NA_HEADS // 2, 2, 1, s, HEAD_DIM), [ka.reshape(b, NA_HEADS // 2, 2, s, HEAD_DIM)],
                     [_with_ones(va).reshape(b, NA_HEADS // 2, 2, s, LANES)], tq=256, tk=256)
        kbs, vbs, kds, vds = [kb], [vb], [kd], [vd]
    else:
        _, _, _, kac, vac, kbc, vbc, kdc, vdc = ops_ctx
        y_a = _nbr(qa, ka, va, kac, vac, bias_a)
        kbs, vbs, kds, vds = [kbc, kb], [vbc, vb], [kdc, kd], [vdc, vd]
    lam_v = jnp.full((1, DIFF_V_DIM), lam, F32)
    gain_v = (sub_g.astype(F32) * (1.0 - lam_init))[None, :]
    y_b = _flash(qb, kbs, vbs, tq=256, tk=512, diff=(lam_v, gain_v))
    y_d = _flash(qd, kds, vds, tq=512, tk=512)
    y_c = _sgu(p, sgu_g, sgu_w, sgu_bf, tt=1024)
    return [y_a, y_b, y_c, y_d]


def kernel(x, c, ctx, c_ctx, w_mod, b_mod, norm1_g, norm2_g, w_in, q_gain, k_gain, na_rpb, lambda_q1, lambda_k1, lambda_q2, lambda_k2, diff_sub_g, sgu_norm_g, sgu_w, sgu_b, gate_b, w_branch, w_out, w_router, w_e_gate, w_e_up, w_e_down, final_g):
    B, S, D = x.shape
    depth = w_mod.shape[0]
    rows = S // GRID_W
    tables = _rope_lane_tables(S, DIFF_QK_DIM) + _rope_lane_tables(S, HEAD_DIM)
    hp = lax.Precision.HIGHEST
    fg = final_g.astype(F32)[None, :]
    grp_id = jnp.arange(COL_BLK) // HEAD_DIM
    gm = (grp_id[:, None] == grp_id[None, :]).astype(F32) / HEAD_DIM
    for l in range(depth):
        last = l == depth - 1
        mod = jnp.dot(jax.nn.silu(c), w_mod[l], precision=hp) + b_mod[l]
        sh1, sc1, g1, sh2, sc2, g2 = (t[:, None, :] for t in jnp.split(mod, 6, axis=-1))
        mod_c = jnp.dot(jax.nn.silu(c_ctx), w_mod[l], precision=hp) + b_mod[l]
        csh1, csc1, cg1, csh2, csc2, cg2 = (jnp.broadcast_to(t[None, None, :], (B, 1, D))
                                            for t in jnp.split(mod_c, 6, axis=-1))
        lam_init = 0.8 - 0.6 * math.exp(-0.3 * l)
        lam = (jnp.exp(jnp.sum(lambda_q1[l].astype(F32) * lambda_k1[l].astype(F32)))
               - jnp.exp(jnp.sum(lambda_q2[l].astype(F32) * lambda_k2[l].astype(F32))) + lam_init)

        wl = w_in[l]
        q0, kv0 = 0, wl.shape[-1] - 5 * COL_BLK

        def cols(start, blk, n=1):
            return wl[:, start + blk * COL_BLK:start + (blk + n) * COL_BLK]

        kdw = GQA_KV_HEADS * HEAD_DIM
        w_l = jnp.concatenate([
            wl[:, 5 * COL_BLK:kv0],
            cols(q0, 0), _deinterleave(cols(q0, 1), DIFF_QK_DIM), _deinterleave(cols(q0, 2), HEAD_DIM),
            cols(q0, 3, 2),
            cols(kv0, 0, 2), _deinterleave(cols(kv0, 2), DIFF_QK_DIM), cols(kv0, 3),
            _deinterleave(wl[:, kv0 + 4 * COL_BLK:kv0 + 4 * COL_BLK + kdw], HEAD_DIM),
            wl[:, kv0 + 4 * COL_BLK + kdw:],
        ], axis=1).astype(BF16)
        qg = jnp.tile(_deinterleave(q_gain[l].astype(F32), HEAD_DIM), GQA_Q_HEADS)[None, :]
        kg = jnp.tile(_deinterleave(k_gain[l].astype(F32), HEAD_DIM), GQA_KV_HEADS)[None, :]
        bias_in = jnp.concatenate([gate_b[l].astype(F32), jnp.zeros((w_l.shape[1] - GATE_W,), F32)])[None, :]
        n1 = norm1_g[l].astype(F32)[None, None, :]
        n2 = norm2_g[l].astype(F32)[None, None, :]
        wb = w_branch[l].astype(BF16)
        wo = w_out[l].astype(BF16)
        wr_t = w_router[l].astype(F32).T
        sgu_g = sgu_norm_g[l].astype(F32)[None, :]
        sgu_wb = sgu_w[l].astype(BF16)
        sgu_bf = jnp.repeat(sgu_b[l].astype(F32).T, BRANCH_W // SGU_GROUPS, axis=1)
        bias_a = _nbr_bias(na_rpb[l], rows)

        tn = w_l.shape[1] // 4
        p = _inproj(x, n1 * (1.0 + sc1), sh1, w_l, bias_in, GATE_W, tm=1024, tn=tn)
        ops = _prep(p, tables, qg, kg, gm, HEAD_DIM ** -0.5, ts=512)
        pc = _inproj(ctx, n1 * (1.0 + csc1), csh1, w_l, bias_in, GATE_W, tm=256, tn=tn)
        ops_c = _prep(pc, None, qg, kg, gm, HEAD_DIM ** -0.5 * LOG2E, ts=256)
        if not last:
            ys_c = _mixer_branches(pc, ops_c, None, None, lam, lam_init, diff_sub_g[l], sgu_g, sgu_wb, sgu_bf)
            ctx, hc2, lg_c = _merge(ys_c, pc, wb, wo, ctx, cg1, n2 * (1.0 + csc2), csh2, wr_t, tm=256)
        ys = _mixer_branches(p, ops, ops_c, bias_a, lam, lam_init, diff_sub_g[l], sgu_g, sgu_wb, sgu_bf)
        x, h2, lg = _merge(ys, p, wb, wo, x, g1, n2 * (1.0 + sc2), sh2, wr_t, tm=512)

        x = _moe(h2, lg, x, g2, w_e_gate, w_e_up, w_e_down, l, fg, final=last)
        if not last:
            ctx = _moe(hc2, lg_c, ctx, cg2, w_e_gate, w_e_up, w_e_down, l, fg, final=False)
    return x
```

```python
import functools
import math

import jax
import jax.numpy as jnp
from jax import lax
from jax.experimental import pallas as pl
from jax.experimental.pallas import tpu as pltpu

F32 = jnp.float32
BF16 = jnp.bfloat16
I32 = jnp.int32

GRID_W = 64
HEAD_DIM = 64
N_BRANCH = 4
BRANCH_W = 256
NA_HEADS = 4
WIN_H = 8
WIN_W = 16
DIFF_HEADS = 4
DIFF_QK_DIM = 32
DIFF_V_DIM = 64
SGU_GROUPS = 4
SGU_CHUNK = 128
GQA_Q_HEADS = 4
GQA_KV_HEADS = 2
ROPE_THETA = 10000.0
N_EXPERTS = 16
CAPACITY_FACTOR = 2
EPS = 1e-6
NEG = -1e30

GATE_W = N_BRANCH * 1024
COL_BLK = 256
SEG_QA, SEG_QB, SEG_QD, SEG_U, SEG_V, SEG_KA, SEG_VA, SEG_KB, SEG_VB = range(16, 25)
COL_KD = GATE_W + 9 * COL_BLK
KV_COL0 = GATE_W + 5 * COL_BLK

VMEM_LIMIT = 56 * 1024 * 1024

NBR_ROWS = 8
NBR_KROWS = 16
MOE_CHUNK = 256


def _cparams(sem):
    return pltpu.CompilerParams(dimension_semantics=sem, vmem_limit_bytes=VMEM_LIMIT)


def _dot_nt(a, b):
    return lax.dot_general(a, b, (((1,), (1,)), ((), ())), preferred_element_type=F32)


def _inproj_kernel(x_ref, mult_ref, shift_ref, w_ref, bias_ref, o_ref, h_scr, *, n_gate_cols):
    j = pl.program_id(2)
    tn = o_ref.shape[-1]

    @pl.when(j == 0)
    def _():
        x = x_ref[0]
        ms = jnp.mean(x * x, axis=-1, keepdims=True)
        h = x * lax.rsqrt(ms + EPS) * mult_ref[0] + shift_ref[0]
        h_scr[...] = h.astype(BF16)

    acc = jnp.dot(h_scr[...], w_ref[...], preferred_element_type=F32) + bias_ref[...]

    @pl.when(j * tn < n_gate_cols)
    def _():
        col = j * tn + lax.broadcasted_iota(I32, acc.shape, 1)
        o_ref[0] = jnp.where(col < n_gate_cols, 1.0 / (1.0 + jnp.exp(-acc)), acc).astype(o_ref.dtype)

    @pl.when(j * tn >= n_gate_cols)
    def _():
        o_ref[0] = acc.astype(o_ref.dtype)


def _inproj(x, mult, shift, w, bias, n_gate_cols, tm, tn):
    B, S, D = x.shape
    N = w.shape[1]
    tm = min(tm, S)
    return pl.pallas_call(
        functools.partial(_inproj_kernel, n_gate_cols=n_gate_cols),
        grid=(B, S // tm, N // tn),
        in_specs=[
            pl.BlockSpec((1, tm, D), lambda b, i, j: (b, i, 0)),
            pl.BlockSpec((1, 1, D), lambda b, i, j: (b, 0, 0)),
            pl.BlockSpec((1, 1, D), lambda b, i, j: (b, 0, 0)),
            pl.BlockSpec((D, tn), lambda b, i, j: (0, j)),
            pl.BlockSpec((1, tn), lambda b, i, j: (0, j)),
        ],
        out_specs=pl.BlockSpec((1, tm, tn), lambda b, i, j: (b, i, j)),
        out_shape=jax.ShapeDtypeStruct((B, S, N), BF16),
        scratch_shapes=[pltpu.VMEM((tm, D), BF16)],
        compiler_params=_cparams(("parallel", "parallel", "arbitrary")),
        name="inproj",
    )(x, mult, shift, w, bias)


LANES = 128


def _flash_kernel(*refs, nseg, tks, J, R, dv, diff):
    if diff:
        lam_ref, subg_ref = refs[:2]
        refs = refs[2:]
    q_ref = refs[0]
    k_refs = refs[1:1 + nseg]
    v_refs = refs[1 + nseg:1 + 2 * nseg]
    o_ref = refs[1 + 2 * nseg]
    m_scr, acc_scr = refs[2 + 2 * nseg:4 + 2 * nseg]
    s_scrs = refs[4 + 2 * nseg:]
    tq, d = q_ref.shape[-2:]
    rows = R * tq
    qs = [q_ref[0, 0, j].reshape(rows, d) for j in range(J)]
    m_scr[...] = jnp.full(m_scr.shape, NEG, F32)
    acc_scr[...] = jnp.zeros(acc_scr.shape, F32)

    def scores(j, k_ref, c, tk):
        return _dot_nt(qs[j], k_ref[0, 0, j, pl.ds(pl.multiple_of(c * tk, tk), tk), :])

    def accumulate(j, s, v_ref, c, tk):
        vc = v_ref[0, 0, j, pl.ds(pl.multiple_of(c * tk, tk), tk), :]
        slabs = [s[:, t * LANES:(t + 1) * LANES] for t in range(tk // LANES)]
        m_cur = functools.reduce(jnp.maximum, slabs)
        m_prev = m_scr[j]
        m_new = jnp.maximum(m_prev, jnp.max(m_cur, axis=1, keepdims=True))
        alpha = jnp.exp2(m_prev - m_new)
        p = jnp.concatenate([jnp.exp2(sl - m_new) for sl in slabs], axis=1).astype(BF16)
        acc_scr[j] = alpha * acc_scr[j] + jnp.dot(p, vc, preferred_element_type=F32)
        m_scr[j] = m_new

    if nseg == 1:
        k_ref, v_ref, tk = k_refs[0], v_refs[0], tks[0]

        def body(c, carry):
            for j in range(J):
                accumulate(j, scores(j, k_ref, c, tk), v_ref, c, tk)
            return carry

        lax.fori_loop(0, k_ref.shape[3] // tk, body, 0)
    else:
        (kc_ref, k_ref), (vc_ref, v_ref), (tkc, tk) = k_refs, v_refs, tks
        nchunks = k_ref.shape[3] // tk
        assert kc_ref.shape[3] == tkc and nchunks % 2 == 0
        s_a = s_scrs[:J]
        s_b = s_scrs[J:]
        for j in range(J):
            s_b[j][...] = scores(j, k_ref, 0, tk)
            accumulate(j, scores(j, kc_ref, 0, tkc), vc_ref, 0, tkc)

        def body(i, carry):
            c = 2 * i
            for j in range(J):
                s_a[j][...] = scores(j, k_ref, c + 1, tk)
                accumulate(j, s_b[j][...], v_ref, c, tk)
            for j in range(J):
                s_b[j][...] = scores(j, k_ref, c + 2, tk)
                accumulate(j, s_a[j][...], v_ref, c + 1, tk)
            return carry

        lax.fori_loop(0, nchunks // 2 - 1, body, 0)
        for j in range(J):
            s_a[j][...] = scores(j, k_ref, nchunks - 1, tk)
            accumulate(j, s_b[j][...], v_ref, nchunks - 2, tk)
        for j in range(J):
            accumulate(j, s_a[j][...], v_ref, nchunks - 1, tk)
    pieces = []
    for j in range(J):
        acc = acc_scr[j]
        o = acc[:, :dv] / acc[:, dv:dv + 1]
        parts = [o[r * tq:(r + 1) * tq] for r in range(R)]
        if diff:
            y = parts[0] - lam_ref[...] * parts[1]
            y = y * lax.rsqrt(jnp.mean(y * y, axis=-1, keepdims=True) + EPS) * subg_ref[...]
            pieces.append(y)
        else:
            pieces.extend(parts)
    o_ref[0] = jnp.concatenate(pieces, axis=1).astype(o_ref.dtype)


def _flash(q, ks, vs, tq, tk, diff=None):
    B, Hs, J, R, Sq, d = q.shape
    dv = LANES // 2
    assert (J if diff else J * R) * dv == LANES
    tq = min(tq, Sq)
    tks = tuple(min(tk, k.shape[3]) for k in ks)
    in_specs = []
    if diff:
        in_specs += [pl.BlockSpec((1, dv), lambda b, h, i: (0, 0))] * 2
    in_specs.append(pl.BlockSpec((1, 1, J, R, tq, d), lambda b, h, i: (b, h, 0, 0, i, 0)))
    for k in ks:
        in_specs.append(pl.BlockSpec((1, 1, J, k.shape[3], d), lambda b, h, i: (b, h, 0, 0, 0)))
    for v in vs:
        in_specs.append(pl.BlockSpec((1, 1, J, v.shape[3], LANES), lambda b, h, i: (b, h, 0, 0, 0)))
    return pl.pallas_call(
        functools.partial(_flash_kernel, nseg=len(ks), tks=tks, J=J, R=R, dv=dv, diff=bool(diff)),
        grid=(B, Hs, Sq // tq),
        in_specs=in_specs,
        out_specs=pl.BlockSpec((1, tq, LANES), lambda b, h, i: (b, i, h)),
        out_shape=jax.ShapeDtypeStruct((B, Sq, Hs * LANES), BF16),
        scratch_shapes=[pltpu.VMEM((J, R * tq, LANES), F32), pltpu.VMEM((J, R * tq, LANES), F32)]
        + ([pltpu.VMEM((R * tq, tks[1]), F32)] * (2 * J) if len(ks) == 2 else []),
        compiler_params=_cparams(("parallel", "parallel", "arbitrary")),
        name="flash_diff" if diff else "flash",
    )(*(diff or ()), q, *ks, *vs)


NBR_HEADS = 2


def _nbr_kernel(q_ref, k0, k1, k2, k3, v0, v1, v2, v3, kc_ref, vc_ref, bias_ref, o_ref):
    outs = []
    for j in range(NBR_HEADS):
        q = q_ref[0, j]
        k = jnp.concatenate([k0[0, j], k1[0, j], k2[0, j], k3[0, j]], axis=0)
        v = jnp.concatenate([v0[0, j], v1[0, j], v2[0, j], v3[0, j]], axis=0)
        s_loc = _dot_nt(q, k) + bias_ref[0, j]
        s_ctx = _dot_nt(q, kc_ref[0, j])
        m = jnp.maximum(jnp.max(s_loc, axis=1, keepdims=True), jnp.max(s_ctx, axis=1, keepdims=True))
        p_loc = jnp.exp(s_loc - m)
        p_ctx = jnp.exp(s_ctx - m)
        l = jnp.sum(p_loc, axis=1, keepdims=True) + jnp.sum(p_ctx, axis=1, keepdims=True)
        o = (jnp.dot(p_ctx.astype(BF16), vc_ref[0, j], preferred_element_type=F32)
             + jnp.dot(p_loc.astype(BF16), v, preferred_element_type=F32))
        outs.append(o / l)
    o_ref[0] = jnp.concatenate(outs, axis=1).astype(o_ref.dtype)


def _nbr_bias(rpb, rows):
    kh, kw = WIN_H, WIN_W
    qc = jnp.arange(GRID_W)
    kc = jnp.arange(GRID_W)
    c0 = jnp.clip(qc - kw // 2, 0, GRID_W - kw)
    col_ok = (kc[None, :] >= c0[:, None]) & (kc[None, :] < c0[:, None] + kw)
    dj = kc[None, :] - qc[:, None] + WIN_W - 1
    oh_c = ((dj[..., None] == jnp.arange(2 * WIN_W - 1)) & col_ok[..., None]).astype(F32)

    def variant(r_start, k_start):
        r = r_start + jnp.arange(NBR_ROWS)
        kr = k_start + jnp.arange(NBR_KROWS)
        r0 = jnp.clip(r - kh // 2, 0, rows - kh)
        row_ok = (kr[None, :] >= r0[:, None]) & (kr[None, :] < r0[:, None] + kh)
        di = kr[None, :] - r[:, None] + WIN_H - 1
        oh_r = ((di[..., None] == jnp.arange(2 * WIN_H - 1)) & row_ok[..., None]).astype(F32)
        b = jnp.einsum('jka,hab,qcb->hjqkc', oh_r, rpb.astype(F32), oh_c, precision=lax.Precision.HIGHEST)
        valid = row_ok[:, None, :, None] & col_ok[None, :, None, :]
        b = jnp.where(valid[None], b, NEG)
        return b.reshape(rpb.shape[0], NBR_ROWS * GRID_W, NBR_KROWS * GRID_W)

    return jnp.stack([variant(0, 0), variant(NBR_ROWS, NBR_ROWS - kh // 2),
                      variant(rows - NBR_ROWS, rows - NBR_KROWS)])


def _nbr(q, k, v, kc, vc, bias):
    B, H, S, d = q.shape
    L = kc.shape[2]
    tq = NBR_ROWS * GRID_W
    kb = tq // 2
    nb = S // tq
    nh = NBR_HEADS
    assert S % tq == 0 and nb >= 3 and nh * d == LANES

    def kmap(j):
        return lambda h, i, b: (b, h, jnp.clip(2 * i - 1, 0, 2 * nb - 4) + j, 0)

    def bmap(h, i, b):
        return (jnp.where(i == 0, 0, jnp.where(i == nb - 1, 2, 1)), h, 0, 0)

    kv_specs = [pl.BlockSpec((1, nh, kb, d), kmap(j)) for j in range(4)]
    return pl.pallas_call(
        _nbr_kernel,
        grid=(H // nh, nb, B),
        in_specs=[pl.BlockSpec((1, nh, tq, d), lambda h, i, b: (b, h, i, 0))] + kv_specs + kv_specs + [
            pl.BlockSpec((1, nh, L, d), lambda h, i, b: (b, h, 0, 0)),
            pl.BlockSpec((1, nh, L, d), lambda h, i, b: (b, h, 0, 0)),
            pl.BlockSpec((1, nh, tq, NBR_KROWS * GRID_W), bmap),
        ],
        out_specs=pl.BlockSpec((1, tq, LANES), lambda h, i, b: (b, i, h)),
        out_shape=jax.ShapeDtypeStruct((B, S, H * d), BF16),
        compiler_params=_cparams(("parallel", "parallel", "parallel")),
        name="nbr_attn",
    )(q, k, k, k, k, v, v, v, v, kc, vc, bias)


def _gelu(x):
    return 0.5 * x * (1.0 + jnp.tanh(math.sqrt(2.0 / math.pi) * (x + 0.044715 * (x * x * x))))


def _sgu_kernel(u_ref, v_ref, g_ref, w_ref, b_ref, o_ref, *, nchunk):
    grp = lax.broadcasted_iota(I32, (SGU_CHUNK, BRANCH_W), 1) // (BRANCH_W // SGU_GROUPS)
    for c in range(nchunk):
        rows = slice(c * SGU_CHUNK, (c + 1) * SGU_CHUNK)
        u = u_ref[0, rows, :].astype(F32)
        v = _gelu(v_ref[0, rows, :].astype(F32))
        mu = jnp.mean(v, axis=-1, keepdims=True)
        var = jnp.mean(jnp.square(v - mu), axis=-1, keepdims=True)
        vn = ((v - mu) * lax.rsqrt(var + EPS) * g_ref[...]).astype(BF16)
        mixed = b_ref[...]
        for g in range(SGU_GROUPS):
            mg = jnp.dot(w_ref[g], vn, preferred_element_type=F32)
            mixed = mixed + jnp.where(grp == g, mg, 0.0)
        o_ref[0, rows, :] = (_gelu(u) * mixed).astype(o_ref.dtype)


def _sgu(p, norm_g, w_s, b_full, tt):
    B, S, _ = p.shape
    tt = min(tt, S)
    return pl.pallas_call(
        functools.partial(_sgu_kernel, nchunk=tt // SGU_CHUNK),
        grid=(B, S // tt),
        in_specs=[
            pl.BlockSpec((1, tt, BRANCH_W), lambda b, i: (b, i, SEG_U)),
            pl.BlockSpec((1, tt, BRANCH_W), lambda b, i: (b, i, SEG_V)),
            pl.BlockSpec((1, BRANCH_W), lambda b, i: (0, 0)),
            pl.BlockSpec((SGU_GROUPS, SGU_CHUNK, SGU_CHUNK), lambda b, i: (0, 0, 0)),
            pl.BlockSpec((SGU_CHUNK, BRANCH_W), lambda b, i: (0, 0)),
        ],
        out_specs=pl.BlockSpec((1, tt, BRANCH_W), lambda b, i: (b, i, 0)),
        out_shape=jax.ShapeDtypeStruct((B, S, BRANCH_W), BF16),
        compiler_params=_cparams(("parallel", "parallel")),
        name="sgu",
    )(p, p, norm_g, w_s, b_full)


def _merge_kernel(ya, yb, yc, yd, g0, g1, g2, g3, wb_ref, wo_ref, x_ref, gate_ref, mult_ref, shift_ref,
                  wr_ref, xo_ref, h2_ref, lg_ref):
    mix = None
    for n, (y, g) in enumerate(((ya, g0), (yb, g1), (yc, g2), (yd, g3))):
        pr = jnp.dot(y[0], wb_ref[n], preferred_element_type=F32)
        t = g[0].astype(F32) * pr
        mix = t if mix is None else mix + t
    out = jnp.dot(mix.astype(BF16), wo_ref[...], preferred_element_type=F32)
    xn = x_ref[0] + gate_ref[0] * out
    xo_ref[0] = xn
    ms = jnp.mean(xn * xn, axis=-1, keepdims=True)
    h2 = xn * lax.rsqrt(ms + EPS) * mult_ref[0] + shift_ref[0]
    h2_ref[0] = h2.astype(BF16)
    lg_ref[0] = lax.dot_general(wr_ref[...], h2, (((1,), (1,)), ((), ())),
                                precision=lax.Precision.HIGHEST, preferred_element_type=F32)


def _merge(ys, p, w_branch, w_out, x, gate, mult2, shift2, w_router_t, tm):
    B, S, D = x.shape
    E = w_router_t.shape[0]
    tm = min(tm, S)
    y_spec = pl.BlockSpec((1, tm, BRANCH_W), lambda b, i: (b, i, 0))
    g_specs = [pl.BlockSpec((1, tm, D), functools.partial(lambda b, i, n: (b, i, n), n=n)) for n in range(N_BRANCH)]
    vec = pl.BlockSpec((1, 1, D), lambda b, i: (b, 0, 0))
    return pl.pallas_call(
        _merge_kernel,
        grid=(B, S // tm),
        in_specs=[y_spec] * 4 + g_specs + [
            pl.BlockSpec((N_BRANCH, BRANCH_W, D), lambda b, i: (0, 0, 0)),
            pl.BlockSpec((D, D), lambda b, i: (0, 0)),
            pl.BlockSpec((1, tm, D), lambda b, i: (b, i, 0)),
            vec, vec, vec,
            pl.BlockSpec((E, D), lambda b, i: (0, 0)),
        ],
        out_specs=[
            pl.BlockSpec((1, tm, D), lambda b, i: (b, i, 0)),
            pl.BlockSpec((1, tm, D), lambda b, i: (b, i, 0)),
            pl.BlockSpec((1, E, tm), lambda b, i: (b, 0, i)),
        ],
        out_shape=[
            jax.ShapeDtypeStruct((B, S, D), F32),
            jax.ShapeDtypeStruct((B, S, D), BF16),
            jax.ShapeDtypeStruct((B, E, S), F32),
        ],
        compiler_params=_cparams(("parallel", "parallel")),
        name="merge",
    )(*ys, p, p, p, p, w_branch, w_out, x, gate, mult2, shift2, w_router_t)


def _cumsum_excl(x, tri):
    n = x.shape[1]
    outs = []
    carry = jnp.zeros((x.shape[0], 1), F32)
    for c in range(n // 128):
        xc = x[:, c * 128:(c + 1) * 128]
        outs.append(jnp.dot(xc.astype(BF16), tri, preferred_element_type=F32) + carry)
        carry = carry + jnp.sum(xc, axis=1, keepdims=True)
    return jnp.concatenate(outs, axis=1)


def _route_kernel(lg_ref, rank_ref, score_ref, *, cap):
    lg = lg_ref[0]
    mx = jnp.max(lg, axis=0, keepdims=True)
    ex = jnp.exp(lg - mx)
    aff = ex / jnp.sum(ex, axis=0, keepdims=True)
    E = lg.shape[0]

    def bisect(i, thr_bits):
        cand = thr_bits | jnp.left_shift(jnp.int32(1), 30 - i)
        cnt = jnp.sum(jnp.where(aff >= pltpu.bitcast(cand, F32), 1, 0), axis=1, keepdims=True)
        return jnp.where(cnt >= cap, cand, thr_bits)

    thr = pltpu.bitcast(lax.fori_loop(0, 31, bisect, jnp.zeros((E, 1), I32)), F32)
    gt = aff > thr
    eq = aff == thr
    need = (cap - jnp.sum(jnp.where(gt, 1, 0), axis=1, keepdims=True)).astype(F32)
    ri = lax.broadcasted_iota(I32, (128, 128), 0)
    ci = lax.broadcasted_iota(I32, (128, 128), 1)
    tri = jnp.where(ri < ci, 1.0, 0.0).astype(BF16)
    eq_before = _cumsum_excl(jnp.where(eq, 1.0, 0.0), tri)
    sel = gt | (eq & (eq_before < need))
    rank = _cumsum_excl(jnp.where(sel, 1.0, 0.0), tri)
    sel = sel & (rank < cap)
    rank_ref[0] = jnp.where(sel, rank.astype(I32), -1)
    score_ref[0] = jnp.where(sel, aff, 0.0)


def _route(logits, cap):
    B, E, n = logits.shape
    spec = pl.BlockSpec((1, E, n), lambda b: (b, 0, 0))
    return pl.pallas_call(
        functools.partial(_route_kernel, cap=cap),
        grid=(B,),
        in_specs=[spec],
        out_specs=[spec, spec],
        out_shape=[jax.ShapeDtypeStruct((B, E, n), I32), jax.ShapeDtypeStruct((B, E, n), F32)],
        compiler_params=_cparams(("parallel",)),
        name="route",
    )(logits)


SLOT_ALIGN = 16


def _window_start(first, j, W, cap):
    return pl.multiple_of(jnp.minimum((first // SLOT_ALIGN) * SLOT_ALIGN + j * W, cap - W), SLOT_ALIGN)


def _windows_needed(lo, hi, W):
    return (hi - (lo // SLOT_ALIGN) * SLOT_ALIGN + W - 1) // W


def _gather_kernel(cnt_ref, rank_ref, h_ref, o_ref, *, nc, W, unroll):
    b = pl.program_id(0)
    e = pl.program_id(2)
    base = (b * pl.num_programs(2) + e) * (nc + 1)
    cap = o_ref.shape[2]
    o_ref[...] = jnp.zeros(o_ref.shape, o_ref.dtype)
    T = MOE_CHUNK

    def window(c, j):
        lo = cnt_ref[base + c]
        r = rank_ref[0, 0, pl.ds(c, 1), :]
        hc = h_ref[0, pl.ds(pl.multiple_of(c * T, T), T), :]
        st = _window_start(lo, j, W, cap)
        slot = lax.broadcasted_iota(I32, (W, T), 0) + st
        hit = (slot == r) & (slot >= (lo // SLOT_ALIGN) * SLOT_ALIGN + j * W)
        got = jnp.dot(jnp.where(hit, 1.0, 0.0).astype(BF16), hc, preferred_element_type=F32)
        o_ref[0, 0, pl.ds(st, W), :] = o_ref[0, 0, pl.ds(st, W), :] + got.astype(o_ref.dtype)

    def group(g, carry):
        for u in range(unroll):
            window(g * unroll + u, 0)
        return carry

    lax.fori_loop(0, nc // unroll, group, 0)

    def tail(c, carry):
        def more(j, carry2):
            window(c, j)
            return carry2

        lax.fori_loop(1, _windows_needed(cnt_ref[base + c], cnt_ref[base + c + 1], W), more, 0)
        return carry

    lax.fori_loop(0, nc, tail, 0)


def _gather(cnt, rank, h, cap):
    B, E, n = rank.shape
    D = h.shape[-1]
    nc = n // MOE_CHUNK
    W = min(128, cap)
    ds = 2
    return pl.pallas_call(
        functools.partial(_gather_kernel, nc=nc, W=W, unroll=min(4, nc)),
        grid_spec=pltpu.PrefetchScalarGridSpec(
            num_scalar_prefetch=1,
            grid=(B, ds, E),
            in_specs=[
                pl.BlockSpec((1, 1, nc, MOE_CHUNK), lambda b, d, e, cnt: (b, e, 0, 0)),
                pl.BlockSpec((1, n, D // ds), lambda b, d, e, cnt: (b, 0, d)),
            ],
            out_specs=pl.BlockSpec((1, 1, cap, D // ds), lambda b, d, e, cnt: (b, e, 0, d)),
        ),
        out_shape=jax.ShapeDtypeStruct((B, E, cap, D), BF16),
        compiler_params=_cparams(("parallel", "parallel", "arbitrary")),
        name="moe_gather",
    )(cnt, rank.reshape(B, E, nc, MOE_CHUNK), h)


def _ffn_kernel(*refs, nsets):
    x_refs = refs[:nsets]
    wg_ref, wu_ref, wd_ref = refs[nsets:nsets + 3]
    o_refs = refs[nsets + 3:2 * nsets + 3]
    acc_scrs = refs[2 * nsets + 3:]
    f = pl.program_id(1)

    @pl.when(f == 0)
    def _():
        for acc_scr in acc_scrs:
            acc_scr[...] = jnp.zeros(acc_scr.shape, F32)

    wg = wg_ref[0, 0].astype(BF16)
    wu = wu_ref[0, 0].astype(BF16)
    wd = wd_ref[0, 0].astype(BF16)
    for x_ref, acc_scr in zip(x_refs, acc_scrs):
        for b in range(x_ref.shape[0]):
            x = x_ref[b, 0]
            g = jnp.dot(x, wg, preferred_element_type=F32)
            u = jnp.dot(x, wu, preferred_element_type=F32)
            hid = (g * (1.0 / (1.0 + jnp.exp(-g))) * u).astype(BF16)
            acc_scr[b] = acc_scr[b] + jnp.dot(hid, wd, preferred_element_type=F32)

    @pl.when(f == pl.num_programs(1) - 1)
    def _():
        for o_ref, acc_scr in zip(o_refs, acc_scrs):
            o_ref[:, 0] = acc_scr[...].astype(o_ref.dtype)


def _ffn(xins, w_gate, w_up, w_down, layer, tf):
    E, D = xins[0].shape[1], xins[0].shape[3]
    Fh = w_gate.shape[-1]
    x_specs = [pl.BlockSpec((x.shape[0], 1, x.shape[2], D), lambda e, f: (0, e, 0, 0)) for x in xins]
    return pl.pallas_call(
        functools.partial(_ffn_kernel, nsets=len(xins)),
        grid=(E, Fh // tf),
        in_specs=x_specs + [
            pl.BlockSpec((1, 1, D, tf), lambda e, f: (layer, e, 0, f)),
            pl.BlockSpec((1, 1, D, tf), lambda e, f: (layer, e, 0, f)),
            pl.BlockSpec((1, 1, tf, D), lambda e, f: (layer, e, f, 0)),
        ],
        out_specs=x_specs,
        out_shape=[jax.ShapeDtypeStruct(x.shape, BF16) for x in xins],
        scratch_shapes=[pltpu.VMEM((x.shape[0], x.shape[2], D), F32) for x in xins],
        compiler_params=_cparams(("parallel", "arbitrary")),
        name="moe_ffn",
    )(*xins, w_gate, w_up, w_down)


def _scatter_kernel(cnt_ref, rank_ref, score_ref, y_ref, x_ref, g2_ref, fg_ref, o_ref, acc_scr, *, nc, W, final):
    b = pl.program_id(0)
    i = pl.program_id(1)
    e = pl.program_id(2)
    E = pl.num_programs(2)
    T = MOE_CHUNK
    nsub = acc_scr.shape[0] // T
    base = (b * E + e) * (nc + 1) + i * nsub

    @pl.when(e == 0)
    def _():
        acc_scr[...] = jnp.zeros(acc_scr.shape, F32)

    cap = y_ref.shape[2]
    lane = lax.broadcasted_iota(I32, (T, E), 1)

    def window(c, j):
        rows = slice(c * T, (c + 1) * T)
        lo = cnt_ref[base + c]
        rc = jnp.sum(jnp.where(lane == e, rank_ref[0, rows, :], 0), axis=1, keepdims=True)
        sc = jnp.sum(jnp.where(lane == e, score_ref[0, rows, :], 0.0), axis=1, keepdims=True)
        st = _window_start(lo, j, W, cap)
        slot = lax.broadcasted_iota(I32, (T, W), 1) + st
        hit = (slot == rc) & (slot >= (lo // SLOT_ALIGN) * SLOT_ALIGN + j * W)
        got = jnp.dot(jnp.where(hit, 1.0, 0.0).astype(BF16), y_ref[0, 0, pl.ds(st, W), :],
                      preferred_element_type=F32)
        acc_scr[rows, :] = acc_scr[rows, :] + sc * got

    for c in range(nsub):
        window(c, 0)
    for c in range(nsub):
        def more(j, carry, c=c):
            window(c, j)
            return carry

        lax.fori_loop(1, _windows_needed(cnt_ref[base + c], cnt_ref[base + c + 1], W), more, 0)

    @pl.when(e == E - 1)
    def _():
        xn = x_ref[0] + g2_ref[0] * acc_scr[...]
        if final:
            ms = jnp.mean(xn * xn, axis=-1, keepdims=True)
            xn = xn * lax.rsqrt(ms + EPS) * fg_ref[...]
        o_ref[0] = xn


def _scatter(cnt, rank_tm, score_tm, y, x, g2, final_g, tt, final):
    B, n, E = rank_tm.shape
    cap, D = y.shape[2], y.shape[3]
    tt = min(tt, n)
    nc = n // MOE_CHUNK
    W = min(128, cap)
    return pl.pallas_call(
        functools.partial(_scatter_kernel, nc=nc, W=W, final=final),
        grid_spec=pltpu.PrefetchScalarGridSpec(
            num_scalar_prefetch=1,
            grid=(B, n // tt, E),
            in_specs=[
                pl.BlockSpec((1, tt, E), lambda b, i, e, cnt: (b, i, 0)),
                pl.BlockSpec((1, tt, E), lambda b, i, e, cnt: (b, i, 0)),
                pl.BlockSpec((1, 1, cap, D), lambda b, i, e, cnt: (b, e, 0, 0)),
                pl.BlockSpec((1, tt, D), lambda b, i, e, cnt: (b, i, 0)),
                pl.BlockSpec((1, 1, D), lambda b, i, e, cnt: (b, 0, 0)),
                pl.BlockSpec((1, D), lambda b, i, e, cnt: (0, 0)),
            ],
            out_specs=pl.BlockSpec((1, tt, D), lambda b, i, e, cnt: (b, i, 0)),
            scratch_shapes=[pltpu.VMEM((tt, D), F32)],
        ),
        out_shape=jax.ShapeDtypeStruct((B, n, D), F32),
        compiler_params=_cparams(("parallel", "parallel", "arbitrary")),
        name="moe_scatter",
    )(cnt, rank_tm, score_tm, y, x, g2, final_g)


def _moe_dispatch(h2, logits):
    B, n, D = h2.shape
    E = logits.shape[1]
    cap = CAPACITY_FACTOR * n // N_EXPERTS
    rank, score = _route(logits, cap)
    nc = n // MOE_CHUNK
    per_chunk = jnp.sum((rank >= 0).reshape(B, E, nc, MOE_CHUNK), axis=-1, dtype=I32)
    cnt = jnp.concatenate([jnp.zeros((B, E, 1), I32), jnp.cumsum(per_chunk, axis=-1, dtype=I32)], axis=-1)
    cnt = cnt.reshape(-1)
    return (cnt, rank, score), _gather(cnt, rank, h2, cap)


def _moe_combine(routing, y, x, g2, final_g, final):
    cnt, rank, score = routing
    return _scatter(cnt, jnp.swapaxes(rank, 1, 2), jnp.swapaxes(score, 1, 2), y, x, g2, final_g,
                    tt=1024, final=final)


LOG2E = math.log2(math.e)
HEADS_PER_BLK = COL_BLK // HEAD_DIM


def _prep_kernel(*refs, rope, qa_scale):
    qa_r, qb_r, qd_r, ka_r, va_r, kb_r, vb_r, kvd_r = refs[:8]
    refs = refs[8:]
    if rope:
        cb_r, sb_r, cd_r, sd_r = refs[:4]
        refs = refs[4:]
    qg_r, kg_r, gm_r = refs[:3]
    qa_o, qb_o, qd_o, ka_o, va_o, kb_o, vb_o, kd_o, vd_o = refs[3:]
    ts = qa_r.shape[1]
    kvw = GQA_KV_HEADS * HEAD_DIM

    def partner(x, half):
        n = x.shape[1]
        lane = lax.broadcasted_iota(I32, x.shape, 1)
        return jnp.where((lane & half) == 0, pltpu.roll(x, n - half, 1), pltpu.roll(x, half, 1))

    def rot(x, cos, sin, half):
        return x * cos + partner(x, half) * sin

    def group_norm(x, gm, gain):
        ms = jnp.dot(x * x, gm, precision=lax.Precision.HIGHEST, preferred_element_type=F32)
        return x * lax.rsqrt(ms + EPS) * gain

    def head(x, h):
        return x[:, h * HEAD_DIM:(h + 1) * HEAD_DIM]

    lane64 = lax.broadcasted_iota(I32, (ts, HEAD_DIM), 1)
    ones_tail = jnp.where(lane64 == 0, 1.0, 0.0).astype(BF16)

    qa = qa_r[0].astype(F32) * qa_scale
    for h in range(HEADS_PER_BLK):
        qa_o[0, h] = head(qa, h).astype(BF16)
        ka_o[0, h] = head(ka_r[0], h)
        va_o[0, h] = head(va_r[0], h)

    qb = qb_r[0].astype(F32)
    kb = kb_r[0].astype(F32)
    if rope:
        qb = rot(qb, cb_r[...], sb_r[...], DIFF_QK_DIM // 2)
        kb = rot(kb, cb_r[...], sb_r[...], DIFF_QK_DIM // 2)
    qb = qb * (DIFF_QK_DIM ** -0.5 * LOG2E)
    for h in range(HEADS_PER_BLK):
        xh = head(qb, h)
        qb_o[0, h // 2, h % 2, 0] = jnp.where(lane64 < DIFF_QK_DIM, xh, 0.0).astype(BF16)
        qb_o[0, h // 2, h % 2, 1] = jnp.where(lane64 >= DIFF_QK_DIM, xh, 0.0).astype(BF16)
        kb_o[0, h // 2, h % 2] = head(kb, h).astype(BF16)
        vb_o[0, h // 2, h % 2] = jnp.concatenate([head(vb_r[0], h), ones_tail], axis=1)

    qd = group_norm(qd_r[0].astype(F32), gm_r[...], qg_r[...])
    kd = group_norm(kvd_r[0, :, :kvw].astype(F32), gm_r[:kvw, :kvw], kg_r[...])
    if rope:
        qd = rot(qd, cd_r[...], sd_r[...], HEAD_DIM // 2)
        kd = rot(kd, cd_r[:, :kvw], sd_r[:, :kvw], HEAD_DIM // 2)
    qd = qd * (HEAD_DIM ** -0.5 * LOG2E)
    grp = GQA_Q_HEADS // GQA_KV_HEADS
    for h in range(GQA_Q_HEADS):
        qd_o[0, h // grp, 0, h % grp] = head(qd, h).astype(BF16)
    vd = kvd_r[0, :, kvw:]
    for h in range(GQA_KV_HEADS):
        kd_o[0, h, 0] = head(kd, h).astype(BF16)
        vd_o[0, h, 0] = jnp.concatenate([head(vd, h), ones_tail], axis=1)


def _prep(p, tables, qg, kg, gm, qa_scale, ts):
    B, S, _ = p.shape
    ts = min(ts, S)

    def seg(blk):
        return pl.BlockSpec((1, ts, COL_BLK), lambda b, i: (b, i, blk))

    in_specs = [seg(s) for s in (SEG_QA, SEG_QB, SEG_QD, SEG_KA, SEG_VA, SEG_KB, SEG_VB, SEG_VB + 1)]
    args = [p] * 8
    if tables is not None:
        in_specs += [pl.BlockSpec((ts, COL_BLK), lambda b, i: (i, 0))] * 4
        args += list(tables)
    in_specs += [pl.BlockSpec(a.shape, lambda b, i: (0, 0)) for a in (qg, kg, gm)]
    args += [qg, kg, gm]

    def out(lead, width):
        shape = (B,) + lead + (S, width)
        nl = len(lead)
        spec = pl.BlockSpec((1,) + lead + (ts, width), lambda b, i: (b,) + (0,) * nl + (i, 0))
        return jax.ShapeDtypeStruct(shape, BF16), spec

    grp = GQA_Q_HEADS // GQA_KV_HEADS
    outs = [out((NA_HEADS,), HEAD_DIM), out((DIFF_HEADS // 2, 2, 2), HEAD_DIM),
            out((GQA_KV_HEADS, 1, grp), HEAD_DIM), out((NA_HEADS,), HEAD_DIM), out((NA_HEADS,), HEAD_DIM),
            out((DIFF_HEADS // 2, 2), HEAD_DIM), out((DIFF_HEADS // 2, 2), LANES),
            out((GQA_KV_HEADS, 1), HEAD_DIM), out((GQA_KV_HEADS, 1), LANES)]
    return pl.pallas_call(
        functools.partial(_prep_kernel, rope=tables is not None, qa_scale=qa_scale),
        grid=(B, S // ts),
        in_specs=in_specs,
        out_specs=[o[1] for o in outs],
        out_shape=[o[0] for o in outs],
        compiler_params=_cparams(("parallel", "parallel")),
        name="attn_prep",
    )(*args)


def _rope_lane_tables(n, dim):
    t = jnp.arange(n)
    row = (t // GRID_W).astype(F32)
    col = (t % GRID_W).astype(F32)
    n_pairs = dim // 4
    inv = ROPE_THETA ** (-jnp.arange(n_pairs, dtype=F32) / n_pairs)
    ang = jnp.concatenate([row[:, None] * inv, col[:, None] * inv], axis=-1)
    cos, sin = jnp.cos(ang), jnp.sin(ang)
    reps = COL_BLK // dim
    return jnp.tile(jnp.concatenate([cos, cos], -1), (1, reps)), jnp.tile(jnp.concatenate([-sin, sin], -1), (1, reps))


def _deinterleave(w, width):
    lead = w.shape[:-1]
    n = w.shape[-1]
    return jnp.swapaxes(w.reshape(lead + (n // width, width // 2, 2)), -1, -2).reshape(lead + (n,))


def _with_ones(v):
    pad = [(0, 0)] * (v.ndim - 1) + [(0, LANES - v.shape[-1] - 1)]
    return jnp.pad(jnp.concatenate([v, jnp.ones(v.shape[:-1] + (1,), v.dtype)], axis=-1), pad)


def _mixer_branches(p, ops, ops_ctx, bias_a, lam, lam_init, sub_g, sgu_g, sgu_w, sgu_bf):
    qa, qb, qd, ka, va, kb, vb, kd, vd = ops
    b, _, s, _ = qa.shape
    if ops_ctx is None:
        y_a = _flash(qa.reshape(b, NA_HEADS // 2, 2, 1, s, HEAD_DIM), [ka.reshape(b, NA_HEADS // 2, 2, s, HEAD_DIM)],
                     [_with_ones(va).reshape(b, NA_HEADS // 2, 2, s, LANES)], tq=256, tk=256)
        kbs, vbs, kds, vds = [kb], [vb], [kd], [vd]
    else:
        _, _, _, kac, vac, kbc, vbc, kdc, vdc = ops_ctx
        y_a = _nbr(qa, ka, va, kac, vac, bias_a)
        kbs, vbs, kds, vds = [kbc, kb], [vbc, vb], [kdc, kd], [vdc, vd]
    lam_v = jnp.full((1, DIFF_V_DIM), lam, F32)
    gain_v = (sub_g.astype(F32) * (1.0 - lam_init))[None, :]
    y_b = _flash(qb, kbs, vbs, tq=256, tk=512, diff=(lam_v, gain_v))
    y_d = _flash(qd, kds, vds, tq=512, tk=512)
    y_c = _sgu(p, sgu_g, sgu_w, sgu_bf, tt=1024)
    return [y_a, y_b, y_c, y_d]


def kernel(x, c, ctx, c_ctx, w_mod, b_mod, norm1_g, norm2_g, w_in, q_gain, k_gain, na_rpb, lambda_q1, lambda_k1, lambda_q2, lambda_k2, diff_sub_g, sgu_norm_g, sgu_w, sgu_b, gate_b, w_branch, w_out, w_router, w_e_gate, w_e_up, w_e_down, final_g):
    B, S, D = x.shape
    depth = w_mod.shape[0]
    rows = S // GRID_W
    tables = _rope_lane_tables(S, DIFF_QK_DIM) + _rope_lane_tables(S, HEAD_DIM)
    hp = lax.Precision.HIGHEST
    fg = final_g.astype(F32)[None, :]
    grp_id = jnp.arange(COL_BLK) // HEAD_DIM
    gm = (grp_id[:, None] == grp_id[None, :]).astype(F32) / HEAD_DIM
    for l in range(depth):
        last = l == depth - 1
        mod = jnp.dot(jax.nn.silu(c), w_mod[l], precision=hp) + b_mod[l]
        sh1, sc1, g1, sh2, sc2, g2 = (t[:, None, :] for t in jnp.split(mod, 6, axis=-1))
        mod_c = jnp.dot(jax.nn.silu(c_ctx), w_mod[l], precision=hp) + b_mod[l]
        csh1, csc1, cg1, csh2, csc2, cg2 = (jnp.broadcast_to(t[None, None, :], (B, 1, D))
                                            for t in jnp.split(mod_c, 6, axis=-1))
        lam_init = 0.8 - 0.6 * math.exp(-0.3 * l)
        lam = (jnp.exp(jnp.sum(lambda_q1[l].astype(F32) * lambda_k1[l].astype(F32)))
               - jnp.exp(jnp.sum(lambda_q2[l].astype(F32) * lambda_k2[l].astype(F32))) + lam_init)

        wl = w_in[l]
        q0, kv0 = 0, wl.shape[-1] - 5 * COL_BLK

        def cols(start, blk, n=1):
            return wl[:, start + blk * COL_BLK:start + (blk + n) * COL_BLK]

        kdw = GQA_KV_HEADS * HEAD_DIM
        w_l = jnp.concatenate([
            wl[:, 5 * COL_BLK:kv0],
            cols(q0, 0), _deinterleave(cols(q0, 1), DIFF_QK_DIM), _deinterleave(cols(q0, 2), HEAD_DIM),
            cols(q0, 3, 2),
            cols(kv0, 0, 2), _deinterleave(cols(kv0, 2), DIFF_QK_DIM), cols(kv0, 3),
            _deinterleave(wl[:, kv0 + 4 * COL_BLK:kv0 + 4 * COL_BLK + kdw], HEAD_DIM),
            wl[:, kv0 + 4 * COL_BLK + kdw:],
        ], axis=1).astype(BF16)
        qg = jnp.tile(_deinterleave(q_gain[l].astype(F32), HEAD_DIM), GQA_Q_HEADS)[None, :]
        kg = jnp.tile(_deinterleave(k_gain[l].astype(F32), HEAD_DIM), GQA_KV_HEADS)[None, :]
        bias_in = jnp.concatenate([gate_b[l].astype(F32), jnp.zeros((w_l.shape[1] - GATE_W,), F32)])[None, :]
        n1 = norm1_g[l].astype(F32)[None, None, :]
        n2 = norm2_g[l].astype(F32)[None, None, :]
        wb = w_branch[l].astype(BF16)
        wo = w_out[l].astype(BF16)
        wr_t = w_router[l].astype(F32).T
        sgu_g = sgu_norm_g[l].astype(F32)[None, :]
        sgu_wb = sgu_w[l].astype(BF16)
        sgu_bf = jnp.repeat(sgu_b[l].astype(F32).T, BRANCH_W // SGU_GROUPS, axis=1)
        bias_a = _nbr_bias(na_rpb[l], rows)

        tn = w_l.shape[1] // 4
        p = _inproj(x, n1 * (1.0 + sc1), sh1, w_l, bias_in, GATE_W, tm=1024, tn=tn)
        ops = _prep(p, tables, qg, kg, gm, HEAD_DIM ** -0.5, ts=512)
        pc = _inproj(ctx, n1 * (1.0 + csc1), csh1, w_l, bias_in, GATE_W, tm=256, tn=tn)
        ops_c = _prep(pc, None, qg, kg, gm, HEAD_DIM ** -0.5 * LOG2E, ts=256)
        if not last:
            ys_c = _mixer_branches(pc, ops_c, None, None, lam, lam_init, diff_sub_g[l], sgu_g, sgu_wb, sgu_bf)
            ctx, hc2, lg_c = _merge(ys_c, pc, wb, wo, ctx, cg1, n2 * (1.0 + csc2), csh2, wr_t, tm=256)
        ys = _mixer_branches(p, ops, ops_c, bias_a, lam, lam_init, diff_sub_g[l], sgu_g, sgu_wb, sgu_bf)
        x, h2, lg = _merge(ys, p, wb, wo, x, g1, n2 * (1.0 + sc2), sh2, wr_t, tm=512)

        routings, xins = zip(*([_moe_dispatch(h2, lg)] + ([] if last else [_moe_dispatch(hc2, lg_c)])))
        ys_e = _ffn(list(xins), w_e_gate, w_e_up, w_e_down, l, tf=256)
        x = _moe_combine(routings[0], ys_e[0], x, g2, fg, final=last)
        if not last:
            ctx = _moe_combine(routings[1], ys_e[1], ctx, cg2, fg, final=False)
    return x
```

```python
import functools
import math

import jax
import jax.numpy as jnp
from jax import lax
from jax.experimental import pallas as pl
from jax.experimental.pallas import tpu as pltpu

F32 = jnp.float32
BF16 = jnp.bfloat16
I32 = jnp.int32

GRID_W = 64
HEAD_DIM = 64
N_BRANCH = 4
BRANCH_W = 256
NA_HEADS = 4
WIN_H = 8
WIN_W = 16
DIFF_HEADS = 4
DIFF_QK_DIM = 32
DIFF_V_DIM = 64
SGU_GROUPS = 4
SGU_CHUNK = 128
GQA_Q_HEADS = 4
GQA_KV_HEADS = 2
ROPE_THETA = 10000.0
N_EXPERTS = 16
CAPACITY_FACTOR = 2
EPS = 1e-6
NEG = -1e30

GATE_W = N_BRANCH * 1024
COL_BLK = 256
SEG_QA, SEG_QB, SEG_QD, SEG_U, SEG_V, SEG_KA, SEG_VA, SEG_KB, SEG_VB = range(16, 25)
COL_KD = GATE_W + 9 * COL_BLK
KV_COL0 = GATE_W + 5 * COL_BLK

VMEM_LIMIT = 56 * 1024 * 1024

NBR_ROWS = 8
NBR_KROWS = 16
MOE_CHUNK = 256


def _cparams(sem):
    return pltpu.CompilerParams(dimension_semantics=sem, vmem_limit_bytes=VMEM_LIMIT)


def _dot_nt(a, b):
    return lax.dot_general(a, b, (((1,), (1,)), ((), ())), preferred_element_type=F32)


def _inproj_kernel(x_ref, mult_ref, shift_ref, w_ref, bias_ref, o_ref, h_scr, *, n_gate_cols):
    j = pl.program_id(2)
    tn = o_ref.shape[-1]

    @pl.when(j == 0)
    def _():
        x = x_ref[0]
        ms = jnp.mean(x * x, axis=-1, keepdims=True)
        h = x * lax.rsqrt(ms + EPS) * mult_ref[0] + shift_ref[0]
        h_scr[...] = h.astype(BF16)

    acc = jnp.dot(h_scr[...], w_ref[...], preferred_element_type=F32) + bias_ref[...]

    @pl.when(j * tn < n_gate_cols)
    def _():
        col = j * tn + lax.broadcasted_iota(I32, acc.shape, 1)
        o_ref[0] = jnp.where(col < n_gate_cols, 0.5 + 0.5 * jnp.tanh(0.5 * acc), acc).astype(o_ref.dtype)

    @pl.when(j * tn >= n_gate_cols)
    def _():
        o_ref[0] = acc.astype(o_ref.dtype)


def _inproj(x, mult, shift, w, bias, n_gate_cols, tm, tn):
    B, S, D = x.shape
    N = w.shape[1]
    tm = min(tm, S)
    return pl.pallas_call(
        functools.partial(_inproj_kernel, n_gate_cols=n_gate_cols),
        grid=(B, S // tm, N // tn),
        in_specs=[
            pl.BlockSpec((1, tm, D), lambda b, i, j: (b, i, 0)),
            pl.BlockSpec((1, 1, D), lambda b, i, j: (b, 0, 0)),
            pl.BlockSpec((1, 1, D), lambda b, i, j: (b, 0, 0)),
            pl.BlockSpec((D, tn), lambda b, i, j: (0, j)),
            pl.BlockSpec((1, tn), lambda b, i, j: (0, j)),
        ],
        out_specs=pl.BlockSpec((1, tm, tn), lambda b, i, j: (b, i, j)),
        out_shape=jax.ShapeDtypeStruct((B, S, N), BF16),
        scratch_shapes=[pltpu.VMEM((tm, D), BF16)],
        compiler_params=_cparams(("parallel", "parallel", "arbitrary")),
        name="inproj",
    )(x, mult, shift, w, bias)


LANES = 128


def _flash_kernel(*refs, nseg, tks, J, R, dv, diff):
    if diff:
        lam_ref, subg_ref = refs[:2]
        refs = refs[2:]
    q_ref = refs[0]
    k_refs = refs[1:1 + nseg]
    v_refs = refs[1 + nseg:1 + 2 * nseg]
    o_ref = refs[1 + 2 * nseg]
    m_scr, acc_scr = refs[2 + 2 * nseg:4 + 2 * nseg]
    s_scrs = refs[4 + 2 * nseg:]
    tq, d = q_ref.shape[-2:]
    rows = R * tq
    qs = [q_ref[0, 0, j].reshape(rows, d) for j in range(J)]
    m_scr[...] = jnp.full(m_scr.shape, NEG, F32)
    acc_scr[...] = jnp.zeros(acc_scr.shape, F32)

    def scores(j, k_ref, c, tk):
        return _dot_nt(qs[j], k_ref[0, 0, j, pl.ds(pl.multiple_of(c * tk, tk), tk), :])

    def accumulate(j, s, v_ref, c, tk):
        vc = v_ref[0, 0, j, pl.ds(pl.multiple_of(c * tk, tk), tk), :]
        slabs = [s[:, t * LANES:(t + 1) * LANES] for t in range(tk // LANES)]
        m_cur = functools.reduce(jnp.maximum, slabs)
        m_prev = m_scr[j]
        m_new = jnp.maximum(m_prev, jnp.max(m_cur, axis=1, keepdims=True))
        alpha = jnp.exp2(m_prev - m_new)
        p = jnp.concatenate([jnp.exp2(sl - m_new) for sl in slabs], axis=1).astype(BF16)
        acc_scr[j] = alpha * acc_scr[j] + jnp.dot(p, vc, preferred_element_type=F32)
        m_scr[j] = m_new

    if nseg == 1:
        k_ref, v_ref, tk = k_refs[0], v_refs[0], tks[0]

        def body(c, carry):
            for j in range(J):
                accumulate(j, scores(j, k_ref, c, tk), v_ref, c, tk)
            return carry

        lax.fori_loop(0, k_ref.shape[3] // tk, body, 0)
    else:
        (kc_ref, k_ref), (vc_ref, v_ref), (tkc, tk) = k_refs, v_refs, tks
        nchunks = k_ref.shape[3] // tk
        assert kc_ref.shape[3] == tkc and nchunks % 2 == 0
        s_a = s_scrs[:J]
        s_b = s_scrs[J:]
        for j in range(J):
            s_b[j][...] = scores(j, k_ref, 0, tk)
            accumulate(j, scores(j, kc_ref, 0, tkc), vc_ref, 0, tkc)

        def body(i, carry):
            c = 2 * i
            for j in range(J):
                s_a[j][...] = scores(j, k_ref, c + 1, tk)
                accumulate(j, s_b[j][...], v_ref, c, tk)
            for j in range(J):
                s_b[j][...] = scores(j, k_ref, c + 2, tk)
                accumulate(j, s_a[j][...], v_ref, c + 1, tk)
            return carry

        lax.fori_loop(0, nchunks // 2 - 1, body, 0)
        for j in range(J):
            s_a[j][...] = scores(j, k_ref, nchunks - 1, tk)
            accumulate(j, s_b[j][...], v_ref, nchunks - 2, tk)
        for j in range(J):
            accumulate(j, s_a[j][...], v_ref, nchunks - 1, tk)
    pieces = []
    for j in range(J):
        acc = acc_scr[j]
        o = acc[:, :dv] / acc[:, dv:dv + 1]
        parts = [o[r * tq:(r + 1) * tq] for r in range(R)]
        if diff:
            y = parts[0] - lam_ref[...] * parts[1]
            y = y * lax.rsqrt(jnp.mean(y * y, axis=-1, keepdims=True) + EPS) * subg_ref[...]
            pieces.append(y)
        else:
            pieces.extend(parts)
    o_ref[0] = jnp.concatenate(pieces, axis=1).astype(o_ref.dtype)


def _flash(q, ks, vs, tq, tk, diff=None):
    B, Hs, J, R, Sq, d = q.shape
    dv = LANES // 2
    assert (J if diff else J * R) * dv == LANES
    tq = min(tq, Sq)
    tks = tuple(min(tk, k.shape[3]) for k in ks)
    in_specs = []
    if diff:
        in_specs += [pl.BlockSpec((1, dv), lambda b, h, i: (0, 0))] * 2
    in_specs.append(pl.BlockSpec((1, 1, J, R, tq, d), lambda b, h, i: (b, h, 0, 0, i, 0)))
    for k in ks:
        in_specs.append(pl.BlockSpec((1, 1, J, k.shape[3], d), lambda b, h, i: (b, h, 0, 0, 0)))
    for v in vs:
        in_specs.append(pl.BlockSpec((1, 1, J, v.shape[3], LANES), lambda b, h, i: (b, h, 0, 0, 0)))
    return pl.pallas_call(
        functools.partial(_flash_kernel, nseg=len(ks), tks=tks, J=J, R=R, dv=dv, diff=bool(diff)),
        grid=(B, Hs, Sq // tq),
        in_specs=in_specs,
        out_specs=pl.BlockSpec((1, tq, LANES), lambda b, h, i: (b, i, h)),
        out_shape=jax.ShapeDtypeStruct((B, Sq, Hs * LANES), BF16),
        scratch_shapes=[pltpu.VMEM((J, R * tq, LANES), F32), pltpu.VMEM((J, R * tq, LANES), F32)]
        + ([pltpu.VMEM((R * tq, tks[1]), F32)] * (2 * J) if len(ks) == 2 else []),
        compiler_params=_cparams(("parallel", "parallel", "arbitrary")),
        name="flash_diff" if diff else "flash",
    )(*(diff or ()), q, *ks, *vs)


NBR_HEADS = 2


def _nbr_kernel(q_ref, k0, k1, k2, k3, v0, v1, v2, v3, kc_ref, vc_ref, bias_ref, o_ref):
    outs = []
    for j in range(NBR_HEADS):
        q = q_ref[0, j]
        k = jnp.concatenate([k0[0, j], k1[0, j], k2[0, j], k3[0, j]], axis=0)
        v = jnp.concatenate([v0[0, j], v1[0, j], v2[0, j], v3[0, j]], axis=0)
        s_loc = _dot_nt(q, k) + bias_ref[0, j]
        s_ctx = _dot_nt(q, kc_ref[0, j])
        m = jnp.maximum(jnp.max(s_loc, axis=1, keepdims=True), jnp.max(s_ctx, axis=1, keepdims=True))
        p_loc = jnp.exp(s_loc - m)
        p_ctx = jnp.exp(s_ctx - m)
        l = jnp.sum(p_loc, axis=1, keepdims=True) + jnp.sum(p_ctx, axis=1, keepdims=True)
        o = (jnp.dot(p_ctx.astype(BF16), vc_ref[0, j], preferred_element_type=F32)
             + jnp.dot(p_loc.astype(BF16), v, preferred_element_type=F32))
        outs.append(o / l)
    o_ref[0] = jnp.concatenate(outs, axis=1).astype(o_ref.dtype)


def _nbr_bias(rpb, rows):
    kh, kw = WIN_H, WIN_W
    qc = jnp.arange(GRID_W)
    kc = jnp.arange(GRID_W)
    c0 = jnp.clip(qc - kw // 2, 0, GRID_W - kw)
    col_ok = (kc[None, :] >= c0[:, None]) & (kc[None, :] < c0[:, None] + kw)
    dj = kc[None, :] - qc[:, None] + WIN_W - 1
    oh_c = ((dj[..., None] == jnp.arange(2 * WIN_W - 1)) & col_ok[..., None]).astype(F32)

    def variant(r_start, k_start):
        r = r_start + jnp.arange(NBR_ROWS)
        kr = k_start + jnp.arange(NBR_KROWS)
        r0 = jnp.clip(r - kh // 2, 0, rows - kh)
        row_ok = (kr[None, :] >= r0[:, None]) & (kr[None, :] < r0[:, None] + kh)
        di = kr[None, :] - r[:, None] + WIN_H - 1
        oh_r = ((di[..., None] == jnp.arange(2 * WIN_H - 1)) & row_ok[..., None]).astype(F32)
        b = jnp.einsum('jka,hab,qcb->hjqkc', oh_r, rpb.astype(F32), oh_c, precision=lax.Precision.HIGHEST)
        valid = row_ok[:, None, :, None] & col_ok[None, :, None, :]
        b = jnp.where(valid[None], b, NEG)
        return b.reshape(rpb.shape[0], NBR_ROWS * GRID_W, NBR_KROWS * GRID_W)

    return jnp.stack([variant(0, 0), variant(NBR_ROWS, NBR_ROWS - kh // 2),
                      variant(rows - NBR_ROWS, rows - NBR_KROWS)])


def _nbr(q, k, v, kc, vc, bias):
    B, H, S, d = q.shape
    L = kc.shape[2]
    tq = NBR_ROWS * GRID_W
    kb = tq // 2
    nb = S // tq
    nh = NBR_HEADS
    assert S % tq == 0 and nb >= 3 and nh * d == LANES

    def kmap(j):
        return lambda h, i, b: (b, h, jnp.clip(2 * i - 1, 0, 2 * nb - 4) + j, 0)

    def bmap(h, i, b):
        return (jnp.where(i == 0, 0, jnp.where(i == nb - 1, 2, 1)), h, 0, 0)

    kv_specs = [pl.BlockSpec((1, nh, kb, d), kmap(j)) for j in range(4)]
    return pl.pallas_call(
        _nbr_kernel,
        grid=(H // nh, nb, B),
        in_specs=[pl.BlockSpec((1, nh, tq, d), lambda h, i, b: (b, h, i, 0))] + kv_specs + kv_specs + [
            pl.BlockSpec((1, nh, L, d), lambda h, i, b: (b, h, 0, 0)),
            pl.BlockSpec((1, nh, L, d), lambda h, i, b: (b, h, 0, 0)),
            pl.BlockSpec((1, nh, tq, NBR_KROWS * GRID_W), bmap),
        ],
        out_specs=pl.BlockSpec((1, tq, LANES), lambda h, i, b: (b, i, h)),
        out_shape=jax.ShapeDtypeStruct((B, S, H * d), BF16),
        compiler_params=_cparams(("parallel", "parallel", "parallel")),
        name="nbr_attn",
    )(q, k, k, k, k, v, v, v, v, kc, vc, bias)


def _gelu(x):
    return 0.5 * x * (1.0 + jnp.tanh(math.sqrt(2.0 / math.pi) * (x + 0.044715 * (x * x * x))))


def _sgu_kernel(u_ref, v_ref, g_ref, w_ref, b_ref, o_ref, *, nchunk):
    grp = lax.broadcasted_iota(I32, (SGU_CHUNK, BRANCH_W), 1) // (BRANCH_W // SGU_GROUPS)
    for c in range(nchunk):
        rows = slice(c * SGU_CHUNK, (c + 1) * SGU_CHUNK)
        u = u_ref[0, rows, :].astype(F32)
        v = _gelu(v_ref[0, rows, :].astype(F32))
        mu = jnp.mean(v, axis=-1, keepdims=True)
        var = jnp.mean(jnp.square(v - mu), axis=-1, keepdims=True)
        vn = ((v - mu) * lax.rsqrt(var + EPS) * g_ref[...]).astype(BF16)
        mixed = b_ref[...]
        for g in range(SGU_GROUPS):
            mg = jnp.dot(w_ref[g], vn, preferred_element_type=F32)
            mixed = mixed + jnp.where(grp == g, mg, 0.0)
        o_ref[0, rows, :] = (_gelu(u) * mixed).astype(o_ref.dtype)


def _sgu(p, norm_g, w_s, b_full, tt):
    B, S, _ = p.shape
    tt = min(tt, S)
    return pl.pallas_call(
        functools.partial(_sgu_kernel, nchunk=tt // SGU_CHUNK),
        grid=(B, S // tt),
        in_specs=[
            pl.BlockSpec((1, tt, BRANCH_W), lambda b, i: (b, i, SEG_U)),
            pl.BlockSpec((1, tt, BRANCH_W), lambda b, i: (b, i, SEG_V)),
            pl.BlockSpec((1, BRANCH_W), lambda b, i: (0, 0)),
            pl.BlockSpec((SGU_GROUPS, SGU_CHUNK, SGU_CHUNK), lambda b, i: (0, 0, 0)),
            pl.BlockSpec((SGU_CHUNK, BRANCH_W), lambda b, i: (0, 0)),
        ],
        out_specs=pl.BlockSpec((1, tt, BRANCH_W), lambda b, i: (b, i, 0)),
        out_shape=jax.ShapeDtypeStruct((B, S, BRANCH_W), BF16),
        compiler_params=_cparams(("parallel", "parallel")),
        name="sgu",
    )(p, p, norm_g, w_s, b_full)


def _merge_kernel(ya, yb, yc, yd, g0, g1, g2, g3, wb_ref, wo_ref, x_ref, gate_ref, mult_ref, shift_ref,
                  wr_ref, xo_ref, h2_ref, lg_ref):
    mix = None
    for n, (y, g) in enumerate(((ya, g0), (yb, g1), (yc, g2), (yd, g3))):
        pr = jnp.dot(y[0], wb_ref[n], preferred_element_type=F32)
        t = g[0].astype(F32) * pr
        mix = t if mix is None else mix + t
    out = jnp.dot(mix.astype(BF16), wo_ref[...], preferred_element_type=F32)
    xn = x_ref[0] + gate_ref[0] * out
    xo_ref[0] = xn
    ms = jnp.mean(xn * xn, axis=-1, keepdims=True)
    h2 = xn * lax.rsqrt(ms + EPS) * mult_ref[0] + shift_ref[0]
    h2_ref[0] = h2.astype(BF16)
    lg_ref[0] = lax.dot_general(wr_ref[...], h2, (((1,), (1,)), ((), ())),
                                precision=lax.Precision.HIGHEST, preferred_element_type=F32)


def _merge(ys, p, w_branch, w_out, x, gate, mult2, shift2, w_router_t, tm):
    B, S, D = x.shape
    E = w_router_t.shape[0]
    tm = min(tm, S)
    y_spec = pl.BlockSpec((1, tm, BRANCH_W), lambda b, i: (b, i, 0))
    g_specs = [pl.BlockSpec((1, tm, D), functools.partial(lambda b, i, n: (b, i, n), n=n)) for n in range(N_BRANCH)]
    vec = pl.BlockSpec((1, 1, D), lambda b, i: (b, 0, 0))
    return pl.pallas_call(
        _merge_kernel,
        grid=(B, S // tm),
        in_specs=[y_spec] * 4 + g_specs + [
            pl.BlockSpec((N_BRANCH, BRANCH_W, D), lambda b, i: (0, 0, 0)),
            pl.BlockSpec((D, D), lambda b, i: (0, 0)),
            pl.BlockSpec((1, tm, D), lambda b, i: (b, i, 0)),
            vec, vec, vec,
            pl.BlockSpec((E, D), lambda b, i: (0, 0)),
        ],
        out_specs=[
            pl.BlockSpec((1, tm, D), lambda b, i: (b, i, 0)),
            pl.BlockSpec((1, tm, D), lambda b, i: (b, i, 0)),
            pl.BlockSpec((1, E, tm), lambda b, i: (b, 0, i)),
        ],
        out_shape=[
            jax.ShapeDtypeStruct((B, S, D), F32),
            jax.ShapeDtypeStruct((B, S, D), BF16),
            jax.ShapeDtypeStruct((B, E, S), F32),
        ],
        compiler_params=_cparams(("parallel", "parallel")),
        name="merge",
    )(*ys, p, p, p, p, w_branch, w_out, x, gate, mult2, shift2, w_router_t)


def _cumsum_excl(x, tri):
    n = x.shape[1]
    outs = []
    carry = jnp.zeros((x.shape[0], 1), F32)
    for c in range(n // 128):
        xc = x[:, c * 128:(c + 1) * 128]
        outs.append(jnp.dot(xc.astype(BF16), tri, preferred_element_type=F32) + carry)
        carry = carry + jnp.sum(xc, axis=1, keepdims=True)
    return jnp.concatenate(outs, axis=1)


def _route_kernel(lg_ref, rank_ref, score_ref, *, cap):
    lg = lg_ref[0]
    mx = jnp.max(lg, axis=0, keepdims=True)
    ex = jnp.exp(lg - mx)
    aff = ex / jnp.sum(ex, axis=0, keepdims=True)
    E = lg.shape[0]

    def bisect(i, thr_bits):
        cand = thr_bits | jnp.left_shift(jnp.int32(1), 30 - i)
        cnt = jnp.sum(jnp.where(aff >= pltpu.bitcast(cand, F32), 1, 0), axis=1, keepdims=True)
        return jnp.where(cnt >= cap, cand, thr_bits)

    thr = pltpu.bitcast(lax.fori_loop(0, 31, bisect, jnp.zeros((E, 1), I32)), F32)
    gt = aff > thr
    eq = aff == thr
    need = (cap - jnp.sum(jnp.where(gt, 1, 0), axis=1, keepdims=True)).astype(F32)
    ri = lax.broadcasted_iota(I32, (128, 128), 0)
    ci = lax.broadcasted_iota(I32, (128, 128), 1)
    tri = jnp.where(ri < ci, 1.0, 0.0).astype(BF16)
    eq_before = _cumsum_excl(jnp.where(eq, 1.0, 0.0), tri)
    sel = gt | (eq & (eq_before < need))
    rank = _cumsum_excl(jnp.where(sel, 1.0, 0.0), tri)
    sel = sel & (rank < cap)
    rank_ref[0] = jnp.where(sel, rank.astype(I32), -1)
    score_ref[0] = jnp.where(sel, aff, 0.0)


def _route(logits, cap):
    B, E, n = logits.shape
    spec = pl.BlockSpec((1, E, n), lambda b: (b, 0, 0))
    return pl.pallas_call(
        functools.partial(_route_kernel, cap=cap),
        grid=(B,),
        in_specs=[spec],
        out_specs=[spec, spec],
        out_shape=[jax.ShapeDtypeStruct((B, E, n), I32), jax.ShapeDtypeStruct((B, E, n), F32)],
        compiler_params=_cparams(("parallel",)),
        name="route",
    )(logits)


SLOT_ALIGN = 16


def _window_start(first, j, W, cap):
    return pl.multiple_of(jnp.minimum((first // SLOT_ALIGN) * SLOT_ALIGN + j * W, cap - W), SLOT_ALIGN)


def _windows_needed(lo, hi, W):
    return (hi - (lo // SLOT_ALIGN) * SLOT_ALIGN + W - 1) // W


def _gather_kernel(cnt_ref, rank_ref, h_ref, o_ref, *, nc, W, unroll, flag_off):
    b = pl.program_id(0)
    e = pl.program_id(1)
    be = b * pl.num_programs(1) + e
    base = be * (nc + 1)
    cap = o_ref.shape[2]
    o_ref[...] = jnp.zeros(o_ref.shape, o_ref.dtype)
    T = MOE_CHUNK

    def window(c, j):
        lo = cnt_ref[base + c]
        r = rank_ref[0, 0, pl.ds(c, 1), :]
        hc = h_ref[0, pl.ds(pl.multiple_of(c * T, T), T), :]
        st = _window_start(lo, j, W, cap)
        slot = lax.broadcasted_iota(I32, (W, T), 0) + st
        hit = (slot == r) & (slot >= (lo // SLOT_ALIGN) * SLOT_ALIGN + j * W)
        got = jnp.dot(jnp.where(hit, 1.0, 0.0).astype(BF16), hc, preferred_element_type=F32)
        o_ref[0, 0, pl.ds(st, W), :] = o_ref[0, 0, pl.ds(st, W), :] + got.astype(o_ref.dtype)

    def group(g, carry):
        for u in range(unroll):
            window(g * unroll + u, 0)
        return carry

    lax.fori_loop(0, nc // unroll, group, 0)

    def tail(c, carry):
        def more(j, carry2):
            window(c, j)
            return carry2

        lax.fori_loop(1, _windows_needed(cnt_ref[base + c], cnt_ref[base + c + 1], W), more, 0)
        return carry

    @pl.when(cnt_ref[flag_off + be] > 0)
    def _():
        lax.fori_loop(0, nc, tail, 0)


def _moe_window(cap):
    return min(128, cap)


def _moe_tables(rank, cap, tt, eg):
    B, E, n = rank.shape
    nc = n // MOE_CHUNK
    W = _moe_window(cap)
    per_chunk = jnp.sum((rank >= 0).reshape(B, E, nc, MOE_CHUNK), axis=-1, dtype=I32)
    cnt = jnp.concatenate([jnp.zeros((B, E, 1), I32), jnp.cumsum(per_chunk, axis=-1, dtype=I32)], axis=-1)
    extra = _windows_needed(cnt[..., :-1], cnt[..., 1:], W) > 1
    g_flag = jnp.any(extra, axis=-1)
    nsub = min(tt, n) // MOE_CHUNK
    s_flag = jnp.any(extra.reshape(B, E // eg, eg, nc // nsub, nsub), axis=(2, 4))
    tbl = jnp.concatenate([cnt.reshape(-1), g_flag.reshape(-1).astype(I32), s_flag.reshape(-1).astype(I32)])
    return tbl, cnt.size, cnt.size + g_flag.size


def _gather(tbl, flag_off, rank, h, cap):
    B, E, n = rank.shape
    D = h.shape[-1]
    nc = n // MOE_CHUNK
    return pl.pallas_call(
        functools.partial(_gather_kernel, nc=nc, W=_moe_window(cap), unroll=min(4, nc), flag_off=flag_off),
        grid_spec=pltpu.PrefetchScalarGridSpec(
            num_scalar_prefetch=1,
            grid=(B, E),
            in_specs=[
                pl.BlockSpec((1, 1, nc, MOE_CHUNK), lambda b, e, tbl: (b, e, 0, 0)),
                pl.BlockSpec((1, n, D), lambda b, e, tbl: (b, 0, 0)),
            ],
            out_specs=pl.BlockSpec((1, 1, cap, D), lambda b, e, tbl: (b, e, 0, 0)),
        ),
        out_shape=jax.ShapeDtypeStruct((B, E, cap, D), BF16),
        compiler_params=_cparams(("parallel", "arbitrary")),
        name="moe_gather",
    )(tbl, rank.reshape(B, E, nc, MOE_CHUNK), h)


def _ffn_kernel(*refs, nsets):
    x_refs = refs[:nsets]
    wg_ref, wu_ref, wd_ref = refs[nsets:nsets + 3]
    o_refs = refs[nsets + 3:2 * nsets + 3]
    acc_scrs = refs[2 * nsets + 3:]
    f = pl.program_id(1)

    @pl.when(f == 0)
    def _():
        for acc_scr in acc_scrs:
            acc_scr[...] = jnp.zeros(acc_scr.shape, F32)

    wg = wg_ref[0, 0].astype(BF16)
    wu = wu_ref[0, 0].astype(BF16)
    wd = wd_ref[0, 0].astype(BF16)
    for x_ref, acc_scr in zip(x_refs, acc_scrs):
        for b in range(x_ref.shape[0]):
            x = x_ref[b, 0]
            g = jnp.dot(x, wg, preferred_element_type=F32)
            u = jnp.dot(x, wu, preferred_element_type=F32)
            hid = (g * (0.5 + 0.5 * jnp.tanh(0.5 * g)) * u).astype(BF16)
            acc_scr[b] = acc_scr[b] + jnp.dot(hid, wd, preferred_element_type=F32)

    @pl.when(f == pl.num_programs(1) - 1)
    def _():
        for o_ref, acc_scr in zip(o_refs, acc_scrs):
            o_ref[:, 0] = acc_scr[...].astype(o_ref.dtype)


def _ffn(xins, w_gate, w_up, w_down, layer, tf):
    E, D = xins[0].shape[1], xins[0].shape[3]
    Fh = w_gate.shape[-1]
    x_specs = [pl.BlockSpec((x.shape[0], 1, x.shape[2], D), lambda e, f: (0, e, 0, 0)) for x in xins]
    return pl.pallas_call(
        functools.partial(_ffn_kernel, nsets=len(xins)),
        grid=(E, Fh // tf),
        in_specs=x_specs + [
            pl.BlockSpec((1, 1, D, tf), lambda e, f: (layer, e, 0, f)),
            pl.BlockSpec((1, 1, D, tf), lambda e, f: (layer, e, 0, f)),
            pl.BlockSpec((1, 1, tf, D), lambda e, f: (layer, e, f, 0)),
        ],
        out_specs=x_specs,
        out_shape=[jax.ShapeDtypeStruct(x.shape, BF16) for x in xins],
        scratch_shapes=[pltpu.VMEM((x.shape[0], x.shape[2], D), F32) for x in xins],
        compiler_params=_cparams(("parallel", "arbitrary")),
        name="moe_ffn",
    )(*xins, w_gate, w_up, w_down)


SCATTER_EG = 4
SCATTER_TT = 1024


def _scatter_kernel(cnt_ref, rank_ref, score_ref, y_ref, x_ref, g2_ref, fg_ref, o_ref, acc_scr, *,
                    nc, W, final, flag_off):
    b = pl.program_id(0)
    i = pl.program_id(1)
    g = pl.program_id(2)
    G = pl.num_programs(2)
    eg = y_ref.shape[1]
    T = MOE_CHUNK
    nsub = acc_scr.shape[0] // T
    cap = y_ref.shape[2]

    @pl.when(g == 0)
    def _():
        acc_scr[...] = jnp.zeros(acc_scr.shape, F32)

    def window(c, ee, j):
        rows = slice(c * T, (c + 1) * T)
        lo = cnt_ref[((b * G + g) * eg + ee) * (nc + 1) + i * nsub + c]
        rc = rank_ref[0, 0, rows, ee:ee + 1]
        sc = score_ref[0, 0, rows, ee:ee + 1]
        st = _window_start(lo, j, W, cap)
        slot = lax.broadcasted_iota(I32, (T, W), 1) + st
        hit = (slot == rc) & (slot >= (lo // SLOT_ALIGN) * SLOT_ALIGN + j * W)
        got = jnp.dot(jnp.where(hit, 1.0, 0.0).astype(BF16), y_ref[0, ee, pl.ds(st, W), :],
                      preferred_element_type=F32)
        return sc * got

    for c in range(nsub):
        rows = slice(c * T, (c + 1) * T)
        acc_scr[rows, :] = acc_scr[rows, :] + functools.reduce(jnp.add, [window(c, ee, 0) for ee in range(eg)])

    @pl.when(cnt_ref[flag_off + (b * G + g) * pl.num_programs(1) + i] > 0)
    def _():
        for c in range(nsub):
            for ee in range(eg):
                def more(j, carry, c=c, ee=ee):
                    rows = slice(c * T, (c + 1) * T)
                    acc_scr[rows, :] = acc_scr[rows, :] + window(c, ee, j)
                    return carry

                base = ((b * G + g) * eg + ee) * (nc + 1) + i * nsub + c
                lax.fori_loop(1, _windows_needed(cnt_ref[base], cnt_ref[base + 1], W), more, 0)

    @pl.when(g == G - 1)
    def _():
        xn = x_ref[0] + g2_ref[0] * acc_scr[...]
        if final:
            ms = jnp.mean(xn * xn, axis=-1, keepdims=True)
            xn = xn * lax.rsqrt(ms + EPS) * fg_ref[...]
        o_ref[0] = xn


def _scatter(tbl, flag_off, rank, score, y, x, g2, final_g, final):
    B, E, n = rank.shape
    cap, D = y.shape[2], y.shape[3]
    tt = min(SCATTER_TT, n)
    eg = SCATTER_EG
    nc = n // MOE_CHUNK

    def token_major(t):
        return t.reshape(B, E // eg, eg, n).transpose(0, 1, 3, 2)

    return pl.pallas_call(
        functools.partial(_scatter_kernel, nc=nc, W=_moe_window(cap), final=final, flag_off=flag_off),
        grid_spec=pltpu.PrefetchScalarGridSpec(
            num_scalar_prefetch=1,
            grid=(B, n // tt, E // eg),
            in_specs=[
                pl.BlockSpec((1, 1, tt, eg), lambda b, i, g, tbl: (b, g, i, 0)),
                pl.BlockSpec((1, 1, tt, eg), lambda b, i, g, tbl: (b, g, i, 0)),
                pl.BlockSpec((1, eg, cap, D), lambda b, i, g, tbl: (b, g, 0, 0)),
                pl.BlockSpec((1, tt, D), lambda b, i, g, tbl: (b, i, 0)),
                pl.BlockSpec((1, 1, D), lambda b, i, g, tbl: (b, 0, 0)),
                pl.BlockSpec((1, D), lambda b, i, g, tbl: (0, 0)),
            ],
            out_specs=pl.BlockSpec((1, tt, D), lambda b, i, g, tbl: (b, i, 0)),
            scratch_shapes=[pltpu.VMEM((tt, D), F32)],
        ),
        out_shape=jax.ShapeDtypeStruct((B, n, D), F32),
        compiler_params=_cparams(("parallel", "parallel", "arbitrary")),
        name="moe_scatter",
    )(tbl, token_major(rank), token_major(score), y, x, g2, final_g)


def _moe_dispatch(h2, logits):
    n = h2.shape[1]
    cap = CAPACITY_FACTOR * n // N_EXPERTS
    rank, score = _route(logits, cap)
    tbl, g_off, s_off = _moe_tables(rank, cap, SCATTER_TT, SCATTER_EG)
    return (tbl, s_off, rank, score), _gather(tbl, g_off, rank, h2, cap)


def _moe_combine(routing, y, x, g2, final_g, final):
    tbl, s_off, rank, score = routing
    return _scatter(tbl, s_off, rank, score, y, x, g2, final_g, final)


LOG2E = math.log2(math.e)
HEADS_PER_BLK = COL_BLK // HEAD_DIM


def _prep_kernel(*refs, rope, qa_scale):
    qa_r, qb_r, qd_r, ka_r, va_r, kb_r, vb_r, kvd_r = refs[:8]
    refs = refs[8:]
    if rope:
        cb_r, sb_r, cd_r, sd_r = refs[:4]
        refs = refs[4:]
    qg_r, kg_r, gm_r = refs[:3]
    qa_o, qb_o, qd_o, ka_o, va_o, kb_o, vb_o, kd_o, vd_o = refs[3:]
    ts = qa_r.shape[1]
    kvw = GQA_KV_HEADS * HEAD_DIM

    def partner(x, half):
        n = x.shape[1]
        lane = lax.broadcasted_iota(I32, x.shape, 1)
        return jnp.where((lane & half) == 0, pltpu.roll(x, n - half, 1), pltpu.roll(x, half, 1))

    def rot(x, cos, sin, half):
        return x * cos + partner(x, half) * sin

    def group_norm(x, gm, gain):
        ms = jnp.dot(x * x, gm, precision=lax.Precision.HIGHEST, preferred_element_type=F32)
        return x * lax.rsqrt(ms + EPS) * gain

    def head(x, h):
        return x[:, h * HEAD_DIM:(h + 1) * HEAD_DIM]

    lane64 = lax.broadcasted_iota(I32, (ts, HEAD_DIM), 1)
    ones_tail = jnp.where(lane64 == 0, 1.0, 0.0).astype(BF16)

    qa = qa_r[0].astype(F32) * qa_scale
    for h in range(HEADS_PER_BLK):
        qa_o[0, h] = head(qa, h).astype(BF16)
        ka_o[0, h] = head(ka_r[0], h)
        va_o[0, h] = head(va_r[0], h)

    qb = qb_r[0].astype(F32)
    kb = kb_r[0].astype(F32)
    if rope:
        qb = rot(qb, cb_r[...], sb_r[...], DIFF_QK_DIM // 2)
        kb = rot(kb, cb_r[...], sb_r[...], DIFF_QK_DIM // 2)
    qb = qb * (DIFF_QK_DIM ** -0.5 * LOG2E)
    for h in range(HEADS_PER_BLK):
        xh = head(qb, h)
        qb_o[0, h // 2, h % 2, 0] = jnp.where(lane64 < DIFF_QK_DIM, xh, 0.0).astype(BF16)
        qb_o[0, h // 2, h % 2, 1] = jnp.where(lane64 >= DIFF_QK_DIM, xh, 0.0).astype(BF16)
        kb_o[0, h // 2, h % 2] = head(kb, h).astype(BF16)
        vb_o[0, h // 2, h % 2] = jnp.concatenate([head(vb_r[0], h), ones_tail], axis=1)

    qd = group_norm(qd_r[0].astype(F32), gm_r[...], qg_r[...])
    kd = group_norm(kvd_r[0, :, :kvw].astype(F32), gm_r[:kvw, :kvw], kg_r[...])
    if rope:
        qd = rot(qd, cd_r[...], sd_r[...], HEAD_DIM // 2)
        kd = rot(kd, cd_r[:, :kvw], sd_r[:, :kvw], HEAD_DIM // 2)
    qd = qd * (HEAD_DIM ** -0.5 * LOG2E)
    grp = GQA_Q_HEADS // GQA_KV_HEADS
    for h in range(GQA_Q_HEADS):
        qd_o[0, h // grp, 0, h % grp] = head(qd, h).astype(BF16)
    vd = kvd_r[0, :, kvw:]
    for h in range(GQA_KV_HEADS):
        kd_o[0, h, 0] = head(kd, h).astype(BF16)
        vd_o[0, h, 0] = jnp.concatenate([head(vd, h), ones_tail], axis=1)


def _prep(p, tables, qg, kg, gm, qa_scale, ts):
    B, S, _ = p.shape
    ts = min(ts, S)

    def seg(blk):
        return pl.BlockSpec((1, ts, COL_BLK), lambda b, i: (b, i, blk))

    in_specs = [seg(s) for s in (SEG_QA, SEG_QB, SEG_QD, SEG_KA, SEG_VA, SEG_KB, SEG_VB, SEG_VB + 1)]
    args = [p] * 8
    if tables is not None:
        in_specs += [pl.BlockSpec((ts, COL_BLK), lambda b, i: (i, 0))] * 4
        args += list(tables)
    in_specs += [pl.BlockSpec(a.shape, lambda b, i: (0, 0)) for a in (qg, kg, gm)]
    args += [qg, kg, gm]

    def out(lead, width):
        shape = (B,) + lead + (S, width)
        nl = len(lead)
        spec = pl.BlockSpec((1,) + lead + (ts, width), lambda b, i: (b,) + (0,) * nl + (i, 0))
        return jax.ShapeDtypeStruct(shape, BF16), spec

    grp = GQA_Q_HEADS // GQA_KV_HEADS
    outs = [out((NA_HEADS,), HEAD_DIM), out((DIFF_HEADS // 2, 2, 2), HEAD_DIM),
            out((GQA_KV_HEADS, 1, grp), HEAD_DIM), out((NA_HEADS,), HEAD_DIM), out((NA_HEADS,), HEAD_DIM),
            out((DIFF_HEADS // 2, 2), HEAD_DIM), out((DIFF_HEADS // 2, 2), LANES),
            out((GQA_KV_HEADS, 1), HEAD_DIM), out((GQA_KV_HEADS, 1), LANES)]
    return pl.pallas_call(
        functools.partial(_prep_kernel, rope=tables is not None, qa_scale=qa_scale),
        grid=(B, S // ts),
        in_specs=in_specs,
        out_specs=[o[1] for o in outs],
        out_shape=[o[0] for o in outs],
        compiler_params=_cparams(("parallel", "parallel")),
        name="attn_prep",
    )(*args)


def _rope_lane_tables(n, dim):
    t = jnp.arange(n)
    row = (t // GRID_W).astype(F32)
    col = (t % GRID_W).astype(F32)
    n_pairs = dim // 4
    inv = ROPE_THETA ** (-jnp.arange(n_pairs, dtype=F32) / n_pairs)
    ang = jnp.concatenate([row[:, None] * inv, col[:, None] * inv], axis=-1)
    cos, sin = jnp.cos(ang), jnp.sin(ang)
    reps = COL_BLK // dim
    return jnp.tile(jnp.concatenate([cos, cos], -1), (1, reps)), jnp.tile(jnp.concatenate([-sin, sin], -1), (1, reps))


def _deinterleave(w, width):
    lead = w.shape[:-1]
    n = w.shape[-1]
    return jnp.swapaxes(w.reshape(lead + (n // width, width // 2, 2)), -1, -2).reshape(lead + (n,))


def _with_ones(v):
    pad = [(0, 0)] * (v.ndim - 1) + [(0, LANES - v.shape[-1] - 1)]
    return jnp.pad(jnp.concatenate([v, jnp.ones(v.shape[:-1] + (1,), v.dtype)], axis=-1), pad)


def _mixer_branches(p, ops, ops_ctx, bias_a, lam, lam_init, sub_g, sgu_g, sgu_w, sgu_bf):
    qa, qb, qd, ka, va, kb, vb, kd, vd = ops
    b, _, s, _ = qa.shape
    if ops_ctx is None:
        y_a = _flash(qa.reshape(b, NA_HEADS // 2, 2, 1, s, HEAD_DIM), [ka.reshape(b, NA_HEADS // 2, 2, s, HEAD_DIM)],
                     [_with_ones(va).reshape(b, NA_HEADS // 2, 2, s, LANES)], tq=256, tk=256)
        kbs, vbs, kds, vds = [kb], [vb], [kd], [vd]
    else:
        _, _, _, kac, vac, kbc, vbc, kdc, vdc = ops_ctx
        y_a = _nbr(qa, ka, va, kac, vac, bias_a)
        kbs, vbs, kds, vds = [kbc, kb], [vbc, vb], [kdc, kd], [vdc, vd]
    lam_v = jnp.full((1, DIFF_V_DIM), lam, F32)
    gain_v = (sub_g.astype(F32) * (1.0 - lam_init))[None, :]
    y_b = _flash(qb, kbs, vbs, tq=256, tk=512, diff=(lam_v, gain_v))
    y_d = _flash(qd, kds, vds, tq=512, tk=512)
    y_c = _sgu(p, sgu_g, sgu_w, sgu_bf, tt=1024)
    return [y_a, y_b, y_c, y_d]


def kernel(x, c, ctx, c_ctx, w_mod, b_mod, norm1_g, norm2_g, w_in, q_gain, k_gain, na_rpb, lambda_q1, lambda_k1, lambda_q2, lambda_k2, diff_sub_g, sgu_norm_g, sgu_w, sgu_b, gate_b, w_branch, w_out, w_router, w_e_gate, w_e_up, w_e_down, final_g):
    B, S, D = x.shape
    depth = w_mod.shape[0]
    rows = S // GRID_W
    tables = _rope_lane_tables(S, DIFF_QK_DIM) + _rope_lane_tables(S, HEAD_DIM)
    hp = lax.Precision.HIGHEST
    fg = final_g.astype(F32)[None, :]
    grp_id = jnp.arange(COL_BLK) // HEAD_DIM
    gm = (grp_id[:, None] == grp_id[None, :]).astype(F32) / HEAD_DIM
    for l in range(depth):
        last = l == depth - 1
        mod = jnp.dot(jax.nn.silu(c), w_mod[l], precision=hp) + b_mod[l]
        sh1, sc1, g1, sh2, sc2, g2 = (t[:, None, :] for t in jnp.split(mod, 6, axis=-1))
        mod_c = jnp.dot(jax.nn.silu(c_ctx), w_mod[l], precision=hp) + b_mod[l]
        csh1, csc1, cg1, csh2, csc2, cg2 = (jnp.broadcast_to(t[None, None, :], (B, 1, D))
                                            for t in jnp.split(mod_c, 6, axis=-1))
        lam_init = 0.8 - 0.6 * math.exp(-0.3 * l)
        lam = (jnp.exp(jnp.sum(lambda_q1[l].astype(F32) * lambda_k1[l].astype(F32)))
               - jnp.exp(jnp.sum(lambda_q2[l].astype(F32) * lambda_k2[l].astype(F32))) + lam_init)

        wl = w_in[l]
        q0, kv0 = 0, wl.shape[-1] - 5 * COL_BLK

        def cols(start, blk, n=1):
            return wl[:, start + blk * COL_BLK:start + (blk + n) * COL_BLK]

        kdw = GQA_KV_HEADS * HEAD_DIM
        w_l = jnp.concatenate([
            wl[:, 5 * COL_BLK:kv0],
            cols(q0, 0), _deinterleave(cols(q0, 1), DIFF_QK_DIM), _deinterleave(cols(q0, 2), HEAD_DIM),
            cols(q0, 3, 2),
            cols(kv0, 0, 2), _deinterleave(cols(kv0, 2), DIFF_QK_DIM), cols(kv0, 3),
            _deinterleave(wl[:, kv0 + 4 * COL_BLK:kv0 + 4 * COL_BLK + kdw], HEAD_DIM),
            wl[:, kv0 + 4 * COL_BLK + kdw:],
        ], axis=1).astype(BF16)
        qg = jnp.tile(_deinterleave(q_gain[l].astype(F32), HEAD_DIM), GQA_Q_HEADS)[None, :]
        kg = jnp.tile(_deinterleave(k_gain[l].astype(F32), HEAD_DIM), GQA_KV_HEADS)[None, :]
        bias_in = jnp.concatenate([gate_b[l].astype(F32), jnp.zeros((w_l.shape[1] - GATE_W,), F32)])[None, :]
        n1 = norm1_g[l].astype(F32)[None, None, :]
        n2 = norm2_g[l].astype(F32)[None, None, :]
        wb = w_branch[l].astype(BF16)
        wo = w_out[l].astype(BF16)
        wr_t = w_router[l].astype(F32).T
        sgu_g = sgu_norm_g[l].astype(F32)[None, :]
        sgu_wb = sgu_w[l].astype(BF16)
        sgu_bf = jnp.repeat(sgu_b[l].astype(F32).T, BRANCH_W // SGU_GROUPS, axis=1)
        bias_a = _nbr_bias(na_rpb[l], rows)

        tn = w_l.shape[1] // 4
        p = _inproj(x, n1 * (1.0 + sc1), sh1, w_l, bias_in, GATE_W, tm=1024, tn=tn)
        ops = _prep(p, tables, qg, kg, gm, HEAD_DIM ** -0.5, ts=512)
        pc = _inproj(ctx, n1 * (1.0 + csc1), csh1, w_l, bias_in, GATE_W, tm=256, tn=tn)
        ops_c = _prep(pc, None, qg, kg, gm, HEAD_DIM ** -0.5 * LOG2E, ts=256)
        if not last:
            ys_c = _mixer_branches(pc, ops_c, None, None, lam, lam_init, diff_sub_g[l], sgu_g, sgu_wb, sgu_bf)
            ctx, hc2, lg_c = _merge(ys_c, pc, wb, wo, ctx, cg1, n2 * (1.0 + csc2), csh2, wr_t, tm=256)
        ys = _mixer_branches(p, ops, ops_c, bias_a, lam, lam_init, diff_sub_g[l], sgu_g, sgu_wb, sgu_bf)
        x, h2, lg = _merge(ys, p, wb, wo, x, g1, n2 * (1.0 + sc2), sh2, wr_t, tm=512)

        routings, xins = zip(*([_moe_dispatch(h2, lg)] + ([] if last else [_moe_dispatch(hc2, lg_c)])))
        ys_e = _ffn(list(xins), w_e_gate, w_e_up, w_e_down, l, tf=256)
        x = _moe_combine(routings[0], ys_e[0], x, g2, fg, final=last)
        if not last:
            ctx = _moe_combine(routings[1], ys_e[1], ctx, cg2, fg, final=False)
    return x
```

```python
import functools
import math

import jax
import jax.numpy as jnp
from jax import lax
from jax.experimental import pallas as pl
from jax.experimental.pallas import tpu as pltpu

F32 = jnp.float32
BF16 = jnp.bfloat16
I32 = jnp.int32

GRID_W = 64
HEAD_DIM = 64
N_BRANCH = 4
BRANCH_W = 256
NA_HEADS = 4
WIN_H = 8
WIN_W = 16
DIFF_HEADS = 4
DIFF_QK_DIM = 32
DIFF_V_DIM = 64
SGU_GROUPS = 4
SGU_CHUNK = 128
GQA_Q_HEADS = 4
GQA_KV_HEADS = 2
ROPE_THETA = 10000.0
N_EXPERTS = 16
CAPACITY_FACTOR = 2
EPS = 1e-6
NEG = -1e30

GATE_W = N_BRANCH * 1024
COL_BLK = 256
SEG_QA, SEG_QB, SEG_QD, SEG_U, SEG_V, SEG_KA, SEG_VA, SEG_KB, SEG_VB = range(16, 25)
COL_KD = GATE_W + 9 * COL_BLK
KV_COL0 = GATE_W + 5 * COL_BLK

VMEM_LIMIT = 56 * 1024 * 1024

NBR_ROWS = 8
NBR_KROWS = 16
MOE_CHUNK = 256


def _cparams(sem):
    return pltpu.CompilerParams(dimension_semantics=sem, vmem_limit_bytes=VMEM_LIMIT)


def _dot_nt(a, b):
    return lax.dot_general(a, b, (((1,), (1,)), ((), ())), preferred_element_type=F32)


def _inproj_kernel(x_ref, mult_ref, shift_ref, w_ref, bias_ref, o_ref, h_scr, *, n_gate_cols):
    j = pl.program_id(2)
    tn = o_ref.shape[-1]

    @pl.when(j == 0)
    def _():
        x = x_ref[0]
        ms = jnp.mean(x * x, axis=-1, keepdims=True)
        h = x * lax.rsqrt(ms + EPS) * mult_ref[0] + shift_ref[0]
        h_scr[...] = h.astype(BF16)

    acc = jnp.dot(h_scr[...], w_ref[...], preferred_element_type=F32) + bias_ref[...]

    @pl.when(j * tn < n_gate_cols)
    def _():
        col = j * tn + lax.broadcasted_iota(I32, acc.shape, 1)
        o_ref[0] = jnp.where(col < n_gate_cols, 0.5 + 0.5 * jnp.tanh(0.5 * acc), acc).astype(o_ref.dtype)

    @pl.when(j * tn >= n_gate_cols)
    def _():
        o_ref[0] = acc.astype(o_ref.dtype)


def _inproj(x, mult, shift, w, bias, n_gate_cols, tm, tn):
    B, S, D = x.shape
    N = w.shape[1]
    tm = min(tm, S)
    return pl.pallas_call(
        functools.partial(_inproj_kernel, n_gate_cols=n_gate_cols),
        grid=(B, S // tm, N // tn),
        in_specs=[
            pl.BlockSpec((1, tm, D), lambda b, i, j: (b, i, 0)),
            pl.BlockSpec((1, 1, D), lambda b, i, j: (b, 0, 0)),
            pl.BlockSpec((1, 1, D), lambda b, i, j: (b, 0, 0)),
            pl.BlockSpec((D, tn), lambda b, i, j: (0, j)),
            pl.BlockSpec((1, tn), lambda b, i, j: (0, j)),
        ],
        out_specs=pl.BlockSpec((1, tm, tn), lambda b, i, j: (b, i, j)),
        out_shape=jax.ShapeDtypeStruct((B, S, N), BF16),
        scratch_shapes=[pltpu.VMEM((tm, D), BF16)],
        compiler_params=_cparams(("parallel", "parallel", "arbitrary")),
        name="inproj",
    )(x, mult, shift, w, bias)


LANES = 128


def _flash_kernel(*refs, nseg, tks, J, R, dv, diff):
    if diff:
        lam_ref, subg_ref = refs[:2]
        refs = refs[2:]
    q_ref = refs[0]
    k_refs = refs[1:1 + nseg]
    v_refs = refs[1 + nseg:1 + 2 * nseg]
    o_ref = refs[1 + 2 * nseg]
    m_scr, acc_scr = refs[2 + 2 * nseg:4 + 2 * nseg]
    s_scrs = refs[4 + 2 * nseg:]
    tq, d = q_ref.shape[-2:]
    rows = R * tq
    qs = [q_ref[0, 0, j].reshape(rows, d) for j in range(J)]
    m_scr[...] = jnp.full(m_scr.shape, NEG, F32)
    acc_scr[...] = jnp.zeros(acc_scr.shape, F32)

    def scores(j, k_ref, c, tk):
        return _dot_nt(qs[j], k_ref[0, 0, j, pl.ds(pl.multiple_of(c * tk, tk), tk), :])

    def accumulate(j, s, v_ref, c, tk):
        vc = v_ref[0, 0, j, pl.ds(pl.multiple_of(c * tk, tk), tk), :]
        slabs = [s[:, t * LANES:(t + 1) * LANES] for t in range(tk // LANES)]
        m_cur = functools.reduce(jnp.maximum, slabs)
        m_prev = m_scr[j]
        m_new = jnp.maximum(m_prev, jnp.max(m_cur, axis=1, keepdims=True))
        alpha = jnp.exp2(m_prev - m_new)
        p = jnp.concatenate([jnp.exp2(sl - m_new) for sl in slabs], axis=1).astype(BF16)
        acc_scr[j] = alpha * acc_scr[j] + jnp.dot(p, vc, preferred_element_type=F32)
        m_scr[j] = m_new

    if nseg == 1:
        k_ref, v_ref, tk = k_refs[0], v_refs[0], tks[0]

        def body(c, carry):
            for j in range(J):
                accumulate(j, scores(j, k_ref, c, tk), v_ref, c, tk)
            return carry

        lax.fori_loop(0, k_ref.shape[3] // tk, body, 0)
    else:
        (kc_ref, k_ref), (vc_ref, v_ref), (tkc, tk) = k_refs, v_refs, tks
        assert kc_ref.shape[3] == tkc
        stages = [(kc_ref, vc_ref, 0, tkc)] + [(k_ref, v_ref, c, tk) for c in range(k_ref.shape[3] // tk)]
        bufs = (s_scrs[:J], s_scrs[J:])
        cur = [scores(j, stages[0][0], 0, tkc) for j in range(J)]
        for idx, (_, vr, c, t) in enumerate(stages):
            nxt = stages[idx + 1] if idx + 1 < len(stages) else None
            for j in range(J):
                if nxt is not None:
                    bufs[idx % 2][j][...] = scores(j, nxt[0], nxt[2], nxt[3])
                accumulate(j, cur[j], vr, c, t)
            if nxt is not None:
                cur = [bufs[idx % 2][j][...] for j in range(J)]
    pieces = []
    for j in range(J):
        acc = acc_scr[j]
        o = acc[:, :dv] / acc[:, dv:dv + 1]
        parts = [o[r * tq:(r + 1) * tq] for r in range(R)]
        if diff:
            y = parts[0] - lam_ref[...] * parts[1]
            y = y * lax.rsqrt(jnp.mean(y * y, axis=-1, keepdims=True) + EPS) * subg_ref[...]
            pieces.append(y)
        else:
            pieces.extend(parts)
    o_ref[0] = jnp.concatenate(pieces, axis=1).astype(o_ref.dtype)


def _flash(q, ks, vs, tq, tk, diff=None):
    B, Hs, J, R, Sq, d = q.shape
    dv = LANES // 2
    assert (J if diff else J * R) * dv == LANES
    tq = min(tq, Sq)
    tks = tuple(min(tk, k.shape[3]) for k in ks)
    in_specs = []
    if diff:
        in_specs += [pl.BlockSpec((1, dv), lambda b, h, i: (0, 0))] * 2
    in_specs.append(pl.BlockSpec((1, 1, J, R, tq, d), lambda b, h, i: (b, h, 0, 0, i, 0)))
    for k in ks:
        in_specs.append(pl.BlockSpec((1, 1, J, k.shape[3], d), lambda b, h, i: (b, h, 0, 0, 0)))
    for v in vs:
        in_specs.append(pl.BlockSpec((1, 1, J, v.shape[3], LANES), lambda b, h, i: (b, h, 0, 0, 0)))
    return pl.pallas_call(
        functools.partial(_flash_kernel, nseg=len(ks), tks=tks, J=J, R=R, dv=dv, diff=bool(diff)),
        grid=(B, Hs, Sq // tq),
        in_specs=in_specs,
        out_specs=pl.BlockSpec((1, tq, LANES), lambda b, h, i: (b, i, h)),
        out_shape=jax.ShapeDtypeStruct((B, Sq, Hs * LANES), BF16),
        scratch_shapes=[pltpu.VMEM((J, R * tq, LANES), F32), pltpu.VMEM((J, R * tq, LANES), F32)]
        + ([pltpu.VMEM((R * tq, tks[1]), F32)] * (2 * J) if len(ks) == 2 else []),
        compiler_params=_cparams(("parallel", "parallel", "arbitrary")),
        name="flash_diff" if diff else "flash",
    )(*(diff or ()), q, *ks, *vs)


NBR_HEADS = 2


def _nbr_kernel(q_ref, k0, k1, k2, k3, v0, v1, v2, v3, kc_ref, vc_ref, bias_ref, o_ref):
    outs = []
    for j in range(NBR_HEADS):
        q = q_ref[0, j]
        k = jnp.concatenate([k0[0, j], k1[0, j], k2[0, j], k3[0, j]], axis=0)
        v = jnp.concatenate([v0[0, j], v1[0, j], v2[0, j], v3[0, j]], axis=0)
        s_loc = _dot_nt(q, k) + bias_ref[0, j]
        s_ctx = _dot_nt(q, kc_ref[0, j])
        m = jnp.maximum(jnp.max(s_loc, axis=1, keepdims=True), jnp.max(s_ctx, axis=1, keepdims=True))
        p_loc = jnp.exp(s_loc - m)
        p_ctx = jnp.exp(s_ctx - m)
        l = jnp.sum(p_loc, axis=1, keepdims=True) + jnp.sum(p_ctx, axis=1, keepdims=True)
        o = (jnp.dot(p_ctx.astype(BF16), vc_ref[0, j], preferred_element_type=F32)
             + jnp.dot(p_loc.astype(BF16), v, preferred_element_type=F32))
        outs.append(o / l)
    o_ref[0] = jnp.concatenate(outs, axis=1).astype(o_ref.dtype)


def _nbr_bias(rpb, rows):
    kh, kw = WIN_H, WIN_W
    qc = jnp.arange(GRID_W)
    kc = jnp.arange(GRID_W)
    c0 = jnp.clip(qc - kw // 2, 0, GRID_W - kw)
    col_ok = (kc[None, :] >= c0[:, None]) & (kc[None, :] < c0[:, None] + kw)
    dj = kc[None, :] - qc[:, None] + WIN_W - 1
    oh_c = ((dj[..., None] == jnp.arange(2 * WIN_W - 1)) & col_ok[..., None]).astype(F32)

    def variant(r_start, k_start):
        r = r_start + jnp.arange(NBR_ROWS)
        kr = k_start + jnp.arange(NBR_KROWS)
        r0 = jnp.clip(r - kh // 2, 0, rows - kh)
        row_ok = (kr[None, :] >= r0[:, None]) & (kr[None, :] < r0[:, None] + kh)
        di = kr[None, :] - r[:, None] + WIN_H - 1
        oh_r = ((di[..., None] == jnp.arange(2 * WIN_H - 1)) & row_ok[..., None]).astype(F32)
        b = jnp.einsum('jka,hab,qcb->hjqkc', oh_r, rpb.astype(F32), oh_c, precision=lax.Precision.HIGHEST)
        valid = row_ok[:, None, :, None] & col_ok[None, :, None, :]
        b = jnp.where(valid[None], b, NEG)
        return b.reshape(rpb.shape[0], NBR_ROWS * GRID_W, NBR_KROWS * GRID_W)

    return jnp.stack([variant(0, 0), variant(NBR_ROWS, NBR_ROWS - kh // 2),
                      variant(rows - NBR_ROWS, rows - NBR_KROWS)])


def _nbr(q, k, v, kc, vc, bias):
    B, H, S, d = q.shape
    L = kc.shape[2]
    tq = NBR_ROWS * GRID_W
    kb = tq // 2
    nb = S // tq
    nh = NBR_HEADS
    assert S % tq == 0 and nb >= 3 and nh * d == LANES

    def kmap(j):
        return lambda h, i, b: (b, h, jnp.clip(2 * i - 1, 0, 2 * nb - 4) + j, 0)

    def bmap(h, i, b):
        return (jnp.where(i == 0, 0, jnp.where(i == nb - 1, 2, 1)), h, 0, 0)

    kv_specs = [pl.BlockSpec((1, nh, kb, d), kmap(j)) for j in range(4)]
    return pl.pallas_call(
        _nbr_kernel,
        grid=(H // nh, nb, B),
        in_specs=[pl.BlockSpec((1, nh, tq, d), lambda h, i, b: (b, h, i, 0))] + kv_specs + kv_specs + [
            pl.BlockSpec((1, nh, L, d), lambda h, i, b: (b, h, 0, 0)),
            pl.BlockSpec((1, nh, L, d), lambda h, i, b: (b, h, 0, 0)),
            pl.BlockSpec((1, nh, tq, NBR_KROWS * GRID_W), bmap),
        ],
        out_specs=pl.BlockSpec((1, tq, LANES), lambda h, i, b: (b, i, h)),
        out_shape=jax.ShapeDtypeStruct((B, S, H * d), BF16),
        compiler_params=_cparams(("parallel", "parallel", "parallel")),
        name="nbr_attn",
    )(q, k, k, k, k, v, v, v, v, kc, vc, bias)


def _gelu(x):
    return 0.5 * x * (1.0 + jnp.tanh(math.sqrt(2.0 / math.pi) * (x + 0.044715 * (x * x * x))))


def _sgu_kernel(u_ref, v_ref, g_ref, w_ref, b_ref, o_ref, *, nchunk):
    grp = lax.broadcasted_iota(I32, (SGU_CHUNK, BRANCH_W), 1) // (BRANCH_W // SGU_GROUPS)
    for c in range(nchunk):
        rows = slice(c * SGU_CHUNK, (c + 1) * SGU_CHUNK)
        u = u_ref[0, rows, :].astype(F32)
        v = _gelu(v_ref[0, rows, :].astype(F32))
        mu = jnp.mean(v, axis=-1, keepdims=True)
        var = jnp.mean(jnp.square(v - mu), axis=-1, keepdims=True)
        vn = ((v - mu) * lax.rsqrt(var + EPS) * g_ref[...]).astype(BF16)
        mixed = b_ref[...]
        for g in range(SGU_GROUPS):
            mg = jnp.dot(w_ref[g], vn, preferred_element_type=F32)
            mixed = mixed + jnp.where(grp == g, mg, 0.0)
        o_ref[0, rows, :] = (_gelu(u) * mixed).astype(o_ref.dtype)


def _sgu(p, norm_g, w_s, b_full, tt):
    B, S, _ = p.shape
    tt = min(tt, S)
    return pl.pallas_call(
        functools.partial(_sgu_kernel, nchunk=tt // SGU_CHUNK),
        grid=(B, S // tt),
        in_specs=[
            pl.BlockSpec((1, tt, BRANCH_W), lambda b, i: (b, i, SEG_U)),
            pl.BlockSpec((1, tt, BRANCH_W), lambda b, i: (b, i, SEG_V)),
            pl.BlockSpec((1, BRANCH_W), lambda b, i: (0, 0)),
            pl.BlockSpec((SGU_GROUPS, SGU_CHUNK, SGU_CHUNK), lambda b, i: (0, 0, 0)),
            pl.BlockSpec((SGU_CHUNK, BRANCH_W), lambda b, i: (0, 0)),
        ],
        out_specs=pl.BlockSpec((1, tt, BRANCH_W), lambda b, i: (b, i, 0)),
        out_shape=jax.ShapeDtypeStruct((B, S, BRANCH_W), BF16),
        compiler_params=_cparams(("parallel", "parallel")),
        name="sgu",
    )(p, p, norm_g, w_s, b_full)


def _merge_kernel(ya, yb, yc, yd, g0, g1, g2, g3, wb_ref, wo_ref, x_ref, gate_ref, mult_ref, shift_ref,
                  wr_ref, xo_ref, h2_ref, lg_ref):
    mix = None
    for n, (y, g) in enumerate(((ya, g0), (yb, g1), (yc, g2), (yd, g3))):
        pr = jnp.dot(y[0], wb_ref[n], preferred_element_type=F32)
        t = g[0].astype(F32) * pr
        mix = t if mix is None else mix + t
    out = jnp.dot(mix.astype(BF16), wo_ref[...], preferred_element_type=F32)
    xn = x_ref[0] + gate_ref[0] * out
    xo_ref[0] = xn
    ms = jnp.mean(xn * xn, axis=-1, keepdims=True)
    h2 = xn * lax.rsqrt(ms + EPS) * mult_ref[0] + shift_ref[0]
    h2_ref[0] = h2.astype(BF16)
    lg_ref[0] = lax.dot_general(wr_ref[...], h2, (((1,), (1,)), ((), ())),
                                precision=lax.Precision.HIGHEST, preferred_element_type=F32)


def _merge(ys, p, w_branch, w_out, x, gate, mult2, shift2, w_router_t, tm):
    B, S, D = x.shape
    E = w_router_t.shape[0]
    tm = min(tm, S)
    y_spec = pl.BlockSpec((1, tm, BRANCH_W), lambda b, i: (b, i, 0))
    g_specs = [pl.BlockSpec((1, tm, D), functools.partial(lambda b, i, n: (b, i, n), n=n)) for n in range(N_BRANCH)]
    vec = pl.BlockSpec((1, 1, D), lambda b, i: (b, 0, 0))
    return pl.pallas_call(
        _merge_kernel,
        grid=(B, S // tm),
        in_specs=[y_spec] * 4 + g_specs + [
            pl.BlockSpec((N_BRANCH, BRANCH_W, D), lambda b, i: (0, 0, 0)),
            pl.BlockSpec((D, D), lambda b, i: (0, 0)),
            pl.BlockSpec((1, tm, D), lambda b, i: (b, i, 0)),
            vec, vec, vec,
            pl.BlockSpec((E, D), lambda b, i: (0, 0)),
        ],
        out_specs=[
            pl.BlockSpec((1, tm, D), lambda b, i: (b, i, 0)),
            pl.BlockSpec((1, tm, D), lambda b, i: (b, i, 0)),
            pl.BlockSpec((1, E, tm), lambda b, i: (b, 0, i)),
        ],
        out_shape=[
            jax.ShapeDtypeStruct((B, S, D), F32),
            jax.ShapeDtypeStruct((B, S, D), BF16),
            jax.ShapeDtypeStruct((B, E, S), F32),
        ],
        compiler_params=_cparams(("parallel", "parallel")),
        name="merge",
    )(*ys, p, p, p, p, w_branch, w_out, x, gate, mult2, shift2, w_router_t)


def _cumsum_excl(x, tri):
    n = x.shape[1]
    outs = []
    carry = jnp.zeros((x.shape[0], 1), F32)
    for c in range(n // 128):
        xc = x[:, c * 128:(c + 1) * 128]
        outs.append(jnp.dot(xc.astype(BF16), tri, preferred_element_type=F32) + carry)
        carry = carry + jnp.sum(xc, axis=1, keepdims=True)
    return jnp.concatenate(outs, axis=1)


def _route_kernel(lg_ref, rank_ref, score_ref, count_ref, *, cap):
    lg = lg_ref[0]
    mx = jnp.max(lg, axis=0, keepdims=True)
    ex = jnp.exp(lg - mx)
    aff = ex / jnp.sum(ex, axis=0, keepdims=True)
    E = lg.shape[0]

    def bisect(i, thr_bits):
        cand = thr_bits | jnp.left_shift(jnp.int32(1), 30 - i)
        cnt = jnp.sum(jnp.where(aff >= pltpu.bitcast(cand, F32), 1, 0), axis=1, keepdims=True)
        return jnp.where(cnt >= cap, cand, thr_bits)

    thr = pltpu.bitcast(lax.fori_loop(0, 31, bisect, jnp.zeros((E, 1), I32)), F32)
    gt = aff > thr
    eq = aff == thr
    need = (cap - jnp.sum(jnp.where(gt, 1, 0), axis=1, keepdims=True)).astype(F32)
    ri = lax.broadcasted_iota(I32, (128, 128), 0)
    ci = lax.broadcasted_iota(I32, (128, 128), 1)
    tri = jnp.where(ri < ci, 1.0, 0.0).astype(BF16)
    eq_before = _cumsum_excl(jnp.where(eq, 1.0, 0.0), tri)
    sel = gt | (eq & (eq_before < need))
    rank = _cumsum_excl(jnp.where(sel, 1.0, 0.0), tri)
    sel = sel & (rank < cap)
    rank_ref[0] = jnp.where(sel, rank.astype(I32), -1)
    score_ref[0] = jnp.where(sel, aff, 0.0)
    n = lg.shape[1]
    chunk_of_token = lax.broadcasted_iota(I32, (n, LANES), 0) // MOE_CHUNK
    member = jnp.where(chunk_of_token == lax.broadcasted_iota(I32, (n, LANES), 1), 1.0, 0.0).astype(BF16)
    count_ref[0] = jnp.dot(jnp.where(sel, 1.0, 0.0).astype(BF16), member, preferred_element_type=F32).astype(I32)


def _route(logits, cap):
    B, E, n = logits.shape
    assert n // MOE_CHUNK <= LANES
    spec = pl.BlockSpec((1, E, n), lambda b: (b, 0, 0))
    rank, score, count = pl.pallas_call(
        functools.partial(_route_kernel, cap=cap),
        grid=(B,),
        in_specs=[spec],
        out_specs=[spec, spec, pl.BlockSpec((1, E, LANES), lambda b: (b, 0, 0))],
        out_shape=[jax.ShapeDtypeStruct((B, E, n), I32), jax.ShapeDtypeStruct((B, E, n), F32),
                   jax.ShapeDtypeStruct((B, E, LANES), I32)],
        compiler_params=_cparams(("parallel",)),
        name="route",
    )(logits)
    return rank, score, count[..., :n // MOE_CHUNK]


SLOT_ALIGN = 16


def _window_start(first, j, W, cap):
    return pl.multiple_of(jnp.minimum((first // SLOT_ALIGN) * SLOT_ALIGN + j * W, cap - W), SLOT_ALIGN)


def _windows_needed(lo, hi, W):
    return (hi - (lo // SLOT_ALIGN) * SLOT_ALIGN + W - 1) // W


def _gather_kernel(cnt_ref, rank_ref, h_ref, o_ref, *, nc, W, unroll, flag_off):
    b = pl.program_id(0)
    e = pl.program_id(1)
    be = b * pl.num_programs(1) + e
    base = be * (nc + 1)
    cap = o_ref.shape[2]
    o_ref[...] = jnp.zeros(o_ref.shape, o_ref.dtype)
    T = MOE_CHUNK

    def window(c, j):
        lo = cnt_ref[base + c]
        r = rank_ref[0, 0, pl.ds(c, 1), :]
        hc = h_ref[0, pl.ds(pl.multiple_of(c * T, T), T), :]
        st = _window_start(lo, j, W, cap)
        slot = lax.broadcasted_iota(I32, (W, T), 0) + st
        hit = (slot == r) & (slot >= (lo // SLOT_ALIGN) * SLOT_ALIGN + j * W)
        got = jnp.dot(jnp.where(hit, 1.0, 0.0).astype(BF16), hc, preferred_element_type=F32)
        o_ref[0, 0, pl.ds(st, W), :] = o_ref[0, 0, pl.ds(st, W), :] + got.astype(o_ref.dtype)

    def group(g, carry):
        for u in range(unroll):
            window(g * unroll + u, 0)
        return carry

    lax.fori_loop(0, nc // unroll, group, 0)

    def tail(c, carry):
        def more(j, carry2):
            window(c, j)
            return carry2

        lax.fori_loop(1, _windows_needed(cnt_ref[base + c], cnt_ref[base + c + 1], W), more, 0)
        return carry

    @pl.when(cnt_ref[flag_off + be] > 0)
    def _():
        lax.fori_loop(0, nc, tail, 0)


def _moe_window(cap):
    return min(128, cap)


def _moe_tables(per_chunk, cap, tt, eg):
    B, E, nc = per_chunk.shape
    n = nc * MOE_CHUNK
    W = _moe_window(cap)
    cnt = jnp.concatenate([jnp.zeros((B, E, 1), I32), jnp.cumsum(per_chunk, axis=-1, dtype=I32)], axis=-1)
    extra = _windows_needed(cnt[..., :-1], cnt[..., 1:], W) > 1
    g_flag = jnp.any(extra, axis=-1)
    nsub = min(tt, n) // MOE_CHUNK
    s_flag = jnp.any(extra.reshape(B, E // eg, eg, nc // nsub, nsub), axis=(2, 4))
    tbl = jnp.concatenate([cnt.reshape(-1), g_flag.reshape(-1).astype(I32), s_flag.reshape(-1).astype(I32)])
    return tbl, cnt.size, cnt.size + g_flag.size


def _gather(tbl, flag_off, rank, h, cap):
    B, E, n = rank.shape
    D = h.shape[-1]
    nc = n // MOE_CHUNK
    return pl.pallas_call(
        functools.partial(_gather_kernel, nc=nc, W=_moe_window(cap), unroll=min(4, nc), flag_off=flag_off),
        grid_spec=pltpu.PrefetchScalarGridSpec(
            num_scalar_prefetch=1,
            grid=(B, E),
            in_specs=[
                pl.BlockSpec((1, 1, nc, MOE_CHUNK), lambda b, e, tbl: (b, e, 0, 0)),
                pl.BlockSpec((1, n, D), lambda b, e, tbl: (b, 0, 0)),
            ],
            out_specs=pl.BlockSpec((1, 1, cap, D), lambda b, e, tbl: (b, e, 0, 0)),
        ),
        out_shape=jax.ShapeDtypeStruct((B, E, cap, D), BF16),
        compiler_params=_cparams(("parallel", "arbitrary")),
        name="moe_gather",
    )(tbl, rank.reshape(B, E, nc, MOE_CHUNK), h)


def _ffn_kernel(*refs, nsets):
    x_refs = refs[:nsets]
    wg_ref, wu_ref, wd_ref = refs[nsets:nsets + 3]
    o_refs = refs[nsets + 3:2 * nsets + 3]
    acc_scrs = refs[2 * nsets + 3:]
    f = pl.program_id(1)

    @pl.when(f == 0)
    def _():
        for acc_scr in acc_scrs:
            acc_scr[...] = jnp.zeros(acc_scr.shape, F32)

    wg = wg_ref[0, 0].astype(BF16)
    wu = wu_ref[0, 0].astype(BF16)
    wd = wd_ref[0, 0].astype(BF16)
    for x_ref, acc_scr in zip(x_refs, acc_scrs):
        for b in range(x_ref.shape[0]):
            x = x_ref[b, 0]
            g = jnp.dot(x, wg, preferred_element_type=F32)
            u = jnp.dot(x, wu, preferred_element_type=F32)
            hid = (g * (0.5 + 0.5 * jnp.tanh(0.5 * g)) * u).astype(BF16)
            acc_scr[b] = acc_scr[b] + jnp.dot(hid, wd, preferred_element_type=F32)

    @pl.when(f == pl.num_programs(1) - 1)
    def _():
        for o_ref, acc_scr in zip(o_refs, acc_scrs):
            o_ref[:, 0] = acc_scr[...].astype(o_ref.dtype)


def _ffn(xins, w_gate, w_up, w_down, layer, tf):
    E, D = xins[0].shape[1], xins[0].shape[3]
    Fh = w_gate.shape[-1]
    x_specs = [pl.BlockSpec((x.shape[0], 1, x.shape[2], D), lambda e, f: (0, e, 0, 0)) for x in xins]
    return pl.pallas_call(
        functools.partial(_ffn_kernel, nsets=len(xins)),
        grid=(E, Fh // tf),
        in_specs=x_specs + [
            pl.BlockSpec((1, 1, D, tf), lambda e, f: (layer, e, 0, f)),
            pl.BlockSpec((1, 1, D, tf), lambda e, f: (layer, e, 0, f)),
            pl.BlockSpec((1, 1, tf, D), lambda e, f: (layer, e, f, 0)),
        ],
        out_specs=x_specs,
        out_shape=[jax.ShapeDtypeStruct(x.shape, BF16) for x in xins],
        scratch_shapes=[pltpu.VMEM((x.shape[0], x.shape[2], D), F32) for x in xins],
        compiler_params=_cparams(("parallel", "arbitrary")),
        name="moe_ffn",
    )(*xins, w_gate, w_up, w_down)


SCATTER_EG = 4
SCATTER_TT = 1024


def _scatter_kernel(cnt_ref, rank_ref, score_ref, y_ref, x_ref, g2_ref, fg_ref, o_ref, acc_scr, *,
                    nc, W, final, flag_off):
    b = pl.program_id(0)
    i = pl.program_id(1)
    g = pl.program_id(2)
    G = pl.num_programs(2)
    eg = y_ref.shape[1]
    T = MOE_CHUNK
    nsub = acc_scr.shape[0] // T
    cap = y_ref.shape[2]

    @pl.when(g == 0)
    def _():
        acc_scr[...] = jnp.zeros(acc_scr.shape, F32)

    def window(c, ee, j):
        rows = slice(c * T, (c + 1) * T)
        lo = cnt_ref[((b * G + g) * eg + ee) * (nc + 1) + i * nsub + c]
        rc = rank_ref[0, 0, rows, ee:ee + 1]
        sc = score_ref[0, 0, rows, ee:ee + 1]
        st = _window_start(lo, j, W, cap)
        slot = lax.broadcasted_iota(I32, (T, W), 1) + st
        hit = (slot == rc) & (slot >= (lo // SLOT_ALIGN) * SLOT_ALIGN + j * W)
        got = jnp.dot(jnp.where(hit, 1.0, 0.0).astype(BF16), y_ref[0, ee, pl.ds(st, W), :],
                      preferred_element_type=F32)
        return sc * got

    for c in range(nsub):
        rows = slice(c * T, (c + 1) * T)
        acc_scr[rows, :] = acc_scr[rows, :] + functools.reduce(jnp.add, [window(c, ee, 0) for ee in range(eg)])

    @pl.when(cnt_ref[flag_off + (b * G + g) * pl.num_programs(1) + i] > 0)
    def _():
        for c in range(nsub):
            for ee in range(eg):
                def more(j, carry, c=c, ee=ee):
                    rows = slice(c * T, (c + 1) * T)
                    acc_scr[rows, :] = acc_scr[rows, :] + window(c, ee, j)
                    return carry

                base = ((b * G + g) * eg + ee) * (nc + 1) + i * nsub + c
                lax.fori_loop(1, _windows_needed(cnt_ref[base], cnt_ref[base + 1], W), more, 0)

    @pl.when(g == G - 1)
    def _():
        xn = x_ref[0] + g2_ref[0] * acc_scr[...]
        if final:
            ms = jnp.mean(xn * xn, axis=-1, keepdims=True)
            xn = xn * lax.rsqrt(ms + EPS) * fg_ref[...]
        o_ref[0] = xn


def _scatter(tbl, flag_off, rank, score, y, x, g2, final_g, final):
    B, E, n = rank.shape
    cap, D = y.shape[2], y.shape[3]
    tt = min(SCATTER_TT, n)
    eg = SCATTER_EG
    nc = n // MOE_CHUNK

    def token_major(t):
        return t.reshape(B, E // eg, eg, n).transpose(0, 1, 3, 2)

    return pl.pallas_call(
        functools.partial(_scatter_kernel, nc=nc, W=_moe_window(cap), final=final, flag_off=flag_off),
        grid_spec=pltpu.PrefetchScalarGridSpec(
            num_scalar_prefetch=1,
            grid=(B, n // tt, E // eg),
            in_specs=[
                pl.BlockSpec((1, 1, tt, eg), lambda b, i, g, tbl: (b, g, i, 0)),
                pl.BlockSpec((1, 1, tt, eg), lambda b, i, g, tbl: (b, g, i, 0)),
                pl.BlockSpec((1, eg, cap, D), lambda b, i, g, tbl: (b, g, 0, 0)),
                pl.BlockSpec((1, tt, D), lambda b, i, g, tbl: (b, i, 0)),
                pl.BlockSpec((1, 1, D), lambda b, i, g, tbl: (b, 0, 0)),
                pl.BlockSpec((1, D), lambda b, i, g, tbl: (0, 0)),
            ],
            out_specs=pl.BlockSpec((1, tt, D), lambda b, i, g, tbl: (b, i, 0)),
            scratch_shapes=[pltpu.VMEM((tt, D), F32)],
        ),
        out_shape=jax.ShapeDtypeStruct((B, n, D), F32),
        compiler_params=_cparams(("parallel", "parallel", "arbitrary")),
        name="moe_scatter",
    )(tbl, token_major(rank), token_major(score), y, x, g2, final_g)


def _moe_dispatch(h2, logits):
    n = h2.shape[1]
    cap = CAPACITY_FACTOR * n // N_EXPERTS
    rank, score, per_chunk = _route(logits, cap)
    tbl, g_off, s_off = _moe_tables(per_chunk, cap, SCATTER_TT, SCATTER_EG)
    return (tbl, s_off, rank, score), _gather(tbl, g_off, rank, h2, cap)


def _moe_combine(routing, y, x, g2, final_g, final):
    tbl, s_off, rank, score = routing
    return _scatter(tbl, s_off, rank, score, y, x, g2, final_g, final)


LOG2E = math.log2(math.e)
HEADS_PER_BLK = COL_BLK // HEAD_DIM


def _prep_kernel(*refs, rope, qa_scale):
    qa_r, qb_r, qd_r, ka_r, va_r, kb_r, vb_r, kvd_r = refs[:8]
    refs = refs[8:]
    if rope:
        cb_r, sb_r, cd_r, sd_r = refs[:4]
        refs = refs[4:]
    qg_r, kg_r, gm_r = refs[:3]
    qa_o, qb_o, qd_o, ka_o, va_o, kb_o, vb_o, kd_o, vd_o = refs[3:]
    ts = qa_r.shape[1]
    kvw = GQA_KV_HEADS * HEAD_DIM

    def partner(x, half):
        n = x.shape[1]
        lane = lax.broadcasted_iota(I32, x.shape, 1)
        return jnp.where((lane & half) == 0, pltpu.roll(x, n - half, 1), pltpu.roll(x, half, 1))

    def rot(x, cos, sin, half):
        return x * cos + partner(x, half) * sin

    def group_norm(x, gm, gain):
        ms = jnp.dot(x * x, gm, precision=lax.Precision.HIGHEST, preferred_element_type=F32)
        return x * lax.rsqrt(ms + EPS) * gain

    def head(x, h):
        return x[:, h * HEAD_DIM:(h + 1) * HEAD_DIM]

    lane64 = lax.broadcasted_iota(I32, (ts, HEAD_DIM), 1)
    ones_tail = jnp.where(lane64 == 0, 1.0, 0.0).astype(BF16)

    qa = qa_r[0].astype(F32) * qa_scale
    for h in range(HEADS_PER_BLK):
        qa_o[0, h] = head(qa, h).astype(BF16)
        ka_o[0, h] = head(ka_r[0], h)
        va_o[0, h] = head(va_r[0], h)

    qb = qb_r[0].astype(F32)
    kb = kb_r[0].astype(F32)
    if rope:
        qb = rot(qb, cb_r[...], sb_r[...], DIFF_QK_DIM // 2)
        kb = rot(kb, cb_r[...], sb_r[...], DIFF_QK_DIM // 2)
    qb = qb * (DIFF_QK_DIM ** -0.5 * LOG2E)
    for h in range(HEADS_PER_BLK):
        xh = head(qb, h)
        qb_o[0, h // 2, h % 2, 0] = jnp.where(lane64 < DIFF_QK_DIM, xh, 0.0).astype(BF16)
        qb_o[0, h // 2, h % 2, 1] = jnp.where(lane64 >= DIFF_QK_DIM, xh, 0.0).astype(BF16)
        kb_o[0, h // 2, h % 2] = head(kb, h).astype(BF16)
        vb_o[0, h // 2, h % 2] = jnp.concatenate([head(vb_r[0], h), ones_tail], axis=1)

    qd = group_norm(qd_r[0].astype(F32), gm_r[...], qg_r[...])
    kd = group_norm(kvd_r[0, :, :kvw].astype(F32), gm_r[:kvw, :kvw], kg_r[...])
    if rope:
        qd = rot(qd, cd_r[...], sd_r[...], HEAD_DIM // 2)
        kd = rot(kd, cd_r[:, :kvw], sd_r[:, :kvw], HEAD_DIM // 2)
    qd = qd * (HEAD_DIM ** -0.5 * LOG2E)
    grp = GQA_Q_HEADS // GQA_KV_HEADS
    for h in range(GQA_Q_HEADS):
        qd_o[0, h // grp, 0, h % grp] = head(qd, h).astype(BF16)
    vd = kvd_r[0, :, kvw:]
    for h in range(GQA_KV_HEADS):
        kd_o[0, h, 0] = head(kd, h).astype(BF16)
        vd_o[0, h, 0] = jnp.concatenate([head(vd, h), ones_tail], axis=1)


def _prep(p, tables, qg, kg, gm, qa_scale, ts):
    B, S, _ = p.shape
    ts = min(ts, S)

    def seg(blk):
        return pl.BlockSpec((1, ts, COL_BLK), lambda b, i: (b, i, blk))

    in_specs = [seg(s) for s in (SEG_QA, SEG_QB, SEG_QD, SEG_KA, SEG_VA, SEG_KB, SEG_VB, SEG_VB + 1)]
    args = [p] * 8
    if tables is not None:
        in_specs += [pl.BlockSpec((ts, COL_BLK), lambda b, i: (i, 0))] * 4
        args += list(tables)
    in_specs += [pl.BlockSpec(a.shape, lambda b, i: (0, 0)) for a in (qg, kg, gm)]
    args += [qg, kg, gm]

    def out(lead, width):
        shape = (B,) + lead + (S, width)
        nl = len(lead)
        spec = pl.BlockSpec((1,) + lead + (ts, width), lambda b, i: (b,) + (0,) * nl + (i, 0))
        return jax.ShapeDtypeStruct(shape, BF16), spec

    grp = GQA_Q_HEADS // GQA_KV_HEADS
    outs = [out((NA_HEADS,), HEAD_DIM), out((DIFF_HEADS // 2, 2, 2), HEAD_DIM),
            out((GQA_KV_HEADS, 1, grp), HEAD_DIM), out((NA_HEADS,), HEAD_DIM), out((NA_HEADS,), HEAD_DIM),
            out((DIFF_HEADS // 2, 2), HEAD_DIM), out((DIFF_HEADS // 2, 2), LANES),
            out((GQA_KV_HEADS, 1), HEAD_DIM), out((GQA_KV_HEADS, 1), LANES)]
    return pl.pallas_call(
        functools.partial(_prep_kernel, rope=tables is not None, qa_scale=qa_scale),
        grid=(B, S // ts),
        in_specs=in_specs,
        out_specs=[o[1] for o in outs],
        out_shape=[o[0] for o in outs],
        compiler_params=_cparams(("parallel", "parallel")),
        name="attn_prep",
    )(*args)


def _rope_lane_tables(n, dim):
    t = jnp.arange(n)
    row = (t // GRID_W).astype(F32)
    col = (t % GRID_W).astype(F32)
    n_pairs = dim // 4
    inv = ROPE_THETA ** (-jnp.arange(n_pairs, dtype=F32) / n_pairs)
    ang = jnp.concatenate([row[:, None] * inv, col[:, None] * inv], axis=-1)
    cos, sin = jnp.cos(ang), jnp.sin(ang)
    reps = COL_BLK // dim
    return jnp.tile(jnp.concatenate([cos, cos], -1), (1, reps)), jnp.tile(jnp.concatenate([-sin, sin], -1), (1, reps))


def _deinterleave(w, width):
    lead = w.shape[:-1]
    n = w.shape[-1]
    return jnp.swapaxes(w.reshape(lead + (n // width, width // 2, 2)), -1, -2).reshape(lead + (n,))


def _with_ones(v):
    pad = [(0, 0)] * (v.ndim - 1) + [(0, LANES - v.shape[-1] - 1)]
    return jnp.pad(jnp.concatenate([v, jnp.ones(v.shape[:-1] + (1,), v.dtype)], axis=-1), pad)


def _mixer_branches(p, ops, ops_ctx, bias_a, lam, lam_init, sub_g, sgu_g, sgu_w, sgu_bf):
    qa, qb, qd, ka, va, kb, vb, kd, vd = ops
    b, _, s, _ = qa.shape
    if ops_ctx is None:
        y_a = _flash(qa.reshape(b, NA_HEADS // 2, 2, 1, s, HEAD_DIM), [ka.reshape(b, NA_HEADS // 2, 2, s, HEAD_DIM)],
                     [_with_ones(va).reshape(b, NA_HEADS // 2, 2, s, LANES)], tq=256, tk=256)
        kbs, vbs, kds, vds = [kb], [vb], [kd], [vd]
    else:
        _, _, _, kac, vac, kbc, vbc, kdc, vdc = ops_ctx
        y_a = _nbr(qa, ka, va, kac, vac, bias_a)
        kbs, vbs, kds, vds = [kbc, kb], [vbc, vb], [kdc, kd], [vdc, vd]
    lam_v = jnp.full((1, DIFF_V_DIM), lam, F32)
    gain_v = (sub_g.astype(F32) * (1.0 - lam_init))[None, :]
    y_b = _flash(qb, kbs, vbs, tq=512, tk=1024, diff=(lam_v, gain_v))
    y_d = _flash(qd, kds, vds, tq=1024, tk=1024)
    y_c = _sgu(p, sgu_g, sgu_w, sgu_bf, tt=1024)
    return [y_a, y_b, y_c, y_d]


def kernel(x, c, ctx, c_ctx, w_mod, b_mod, norm1_g, norm2_g, w_in, q_gain, k_gain, na_rpb, lambda_q1, lambda_k1, lambda_q2, lambda_k2, diff_sub_g, sgu_norm_g, sgu_w, sgu_b, gate_b, w_branch, w_out, w_router, w_e_gate, w_e_up, w_e_down, final_g):
    B, S, D = x.shape
    depth = w_mod.shape[0]
    rows = S // GRID_W
    tables = _rope_lane_tables(S, DIFF_QK_DIM) + _rope_lane_tables(S, HEAD_DIM)
    hp = lax.Precision.HIGHEST
    fg = final_g.astype(F32)[None, :]
    grp_id = jnp.arange(COL_BLK) // HEAD_DIM
    gm = (grp_id[:, None] == grp_id[None, :]).astype(F32) / HEAD_DIM
    for l in range(depth):
        last = l == depth - 1
        mod = jnp.dot(jax.nn.silu(c), w_mod[l], precision=hp) + b_mod[l]
        sh1, sc1, g1, sh2, sc2, g2 = (t[:, None, :] for t in jnp.split(mod, 6, axis=-1))
        mod_c = jnp.dot(jax.nn.silu(c_ctx), w_mod[l], precision=hp) + b_mod[l]
        csh1, csc1, cg1, csh2, csc2, cg2 = (jnp.broadcast_to(t[None, None, :], (B, 1, D))
                                            for t in jnp.split(mod_c, 6, axis=-1))
        lam_init = 0.8 - 0.6 * math.exp(-0.3 * l)
        lam = (jnp.exp(jnp.sum(lambda_q1[l].astype(F32) * lambda_k1[l].astype(F32)))
               - jnp.exp(jnp.sum(lambda_q2[l].astype(F32) * lambda_k2[l].astype(F32))) + lam_init)

        wl = w_in[l]
        q0, kv0 = 0, wl.shape[-1] - 5 * COL_BLK

        def cols(start, blk, n=1):
            return wl[:, start + blk * COL_BLK:start + (blk + n) * COL_BLK]

        kdw = GQA_KV_HEADS * HEAD_DIM
        w_l = jnp.concatenate([
            wl[:, 5 * COL_BLK:kv0],
            cols(q0, 0), _deinterleave(cols(q0, 1), DIFF_QK_DIM), _deinterleave(cols(q0, 2), HEAD_DIM),
            cols(q0, 3, 2),
            cols(kv0, 0, 2), _deinterleave(cols(kv0, 2), DIFF_QK_DIM), cols(kv0, 3),
            _deinterleave(wl[:, kv0 + 4 * COL_BLK:kv0 + 4 * COL_BLK + kdw], HEAD_DIM),
            wl[:, kv0 + 4 * COL_BLK + kdw:],
        ], axis=1).astype(BF16)
        qg = jnp.tile(_deinterleave(q_gain[l].astype(F32), HEAD_DIM), GQA_Q_HEADS)[None, :]
        kg = jnp.tile(_deinterleave(k_gain[l].astype(F32), HEAD_DIM), GQA_KV_HEADS)[None, :]
        bias_in = jnp.concatenate([gate_b[l].astype(F32), jnp.zeros((w_l.shape[1] - GATE_W,), F32)])[None, :]
        n1 = norm1_g[l].astype(F32)[None, None, :]
        n2 = norm2_g[l].astype(F32)[None, None, :]
        wb = w_branch[l].astype(BF16)
        wo = w_out[l].astype(BF16)
        wr_t = w_router[l].astype(F32).T
        sgu_g = sgu_norm_g[l].astype(F32)[None, :]
        sgu_wb = sgu_w[l].astype(BF16)
        sgu_bf = jnp.repeat(sgu_b[l].astype(F32).T, BRANCH_W // SGU_GROUPS, axis=1)
        bias_a = _nbr_bias(na_rpb[l], rows)

        tn = w_l.shape[1] // 4
        p = _inproj(x, n1 * (1.0 + sc1), sh1, w_l, bias_in, GATE_W, tm=2048, tn=tn)
        ops = _prep(p, tables, qg, kg, gm, HEAD_DIM ** -0.5, ts=512)
        pc = _inproj(ctx, n1 * (1.0 + csc1), csh1, w_l, bias_in, GATE_W, tm=256, tn=tn)
        ops_c = _prep(pc, None, qg, kg, gm, HEAD_DIM ** -0.5 * LOG2E, ts=256)
        if not last:
            ys_c = _mixer_branches(pc, ops_c, None, None, lam, lam_init, diff_sub_g[l], sgu_g, sgu_wb, sgu_bf)
            ctx, hc2, lg_c = _merge(ys_c, pc, wb, wo, ctx, cg1, n2 * (1.0 + csc2), csh2, wr_t, tm=256)
        ys = _mixer_branches(p, ops, ops_c, bias_a, lam, lam_init, diff_sub_g[l], sgu_g, sgu_wb, sgu_bf)
        x, h2, lg = _merge(ys, p, wb, wo, x, g1, n2 * (1.0 + sc2), sh2, wr_t, tm=512)

        routings, xins = zip(*([_moe_dispatch(h2, lg)] + ([] if last else [_moe_dispatch(hc2, lg_c)])))
        ys_e = _ffn(list(xins), w_e_gate, w_e_up, w_e_down, l, tf=256)
        x = _moe_combine(routings[0], ys_e[0], x, g2, fg, final=last)
        if not last:
            ctx = _moe_combine(routings[1], ys_e[1], ctx, cg2, fg, final=False)
    return x
```

```python
import functools
import math

import jax
import jax.numpy as jnp
from jax import lax
from jax.experimental import pallas as pl
from jax.experimental.pallas import tpu as pltpu

F32 = jnp.float32
BF16 = jnp.bfloat16
I32 = jnp.int32

GRID_W = 64
HEAD_DIM = 64
N_BRANCH = 4
BRANCH_W = 256
NA_HEADS = 4
WIN_H = 8
WIN_W = 16
DIFF_HEADS = 4
DIFF_QK_DIM = 32
DIFF_V_DIM = 64
SGU_GROUPS = 4
SGU_CHUNK = 128
GQA_Q_HEADS = 4
GQA_KV_HEADS = 2
ROPE_THETA = 10000.0
N_EXPERTS = 16
CAPACITY_FACTOR = 2
EPS = 1e-6
NEG = -1e30

GATE_W = N_BRANCH * 1024
COL_BLK = 256
SEG_QA, SEG_QB, SEG_QD, SEG_U, SEG_V, SEG_KA, SEG_VA, SEG_KB, SEG_VB = range(16, 25)
COL_KD = GATE_W + 9 * COL_BLK
KV_COL0 = GATE_W + 5 * COL_BLK

VMEM_LIMIT = 56 * 1024 * 1024

NBR_ROWS = 8
NBR_KROWS = 16
MOE_CHUNK = 256


def _cparams(sem):
    return pltpu.CompilerParams(dimension_semantics=sem, vmem_limit_bytes=VMEM_LIMIT)


def _dot_nt(a, b):
    return lax.dot_general(a, b, (((1,), (1,)), ((), ())), preferred_element_type=F32)


def _inproj_kernel(x_ref, mult_ref, shift_ref, w_ref, bias_ref, o_ref, h_scr, *, n_gate_cols):
    j = pl.program_id(2)
    tn = o_ref.shape[-1]

    @pl.when(j == 0)
    def _():
        x = x_ref[0]
        ms = jnp.mean(x * x, axis=-1, keepdims=True)
        h = x * lax.rsqrt(ms + EPS) * mult_ref[0] + shift_ref[0]
        h_scr[...] = h.astype(BF16)

    acc = jnp.dot(h_scr[...], w_ref[...], preferred_element_type=F32) + bias_ref[...]

    @pl.when(j * tn < n_gate_cols)
    def _():
        col = j * tn + lax.broadcasted_iota(I32, acc.shape, 1)
        o_ref[0] = jnp.where(col < n_gate_cols, 0.5 + 0.5 * jnp.tanh(0.5 * acc), acc).astype(o_ref.dtype)

    @pl.when(j * tn >= n_gate_cols)
    def _():
        o_ref[0] = acc.astype(o_ref.dtype)


def _inproj(x, mult, shift, w, bias, n_gate_cols, tm, tn):
    B, S, D = x.shape
    N = w.shape[1]
    tm = min(tm, S)
    return pl.pallas_call(
        functools.partial(_inproj_kernel, n_gate_cols=n_gate_cols),
        grid=(B, S // tm, N // tn),
        in_specs=[
            pl.BlockSpec((1, tm, D), lambda b, i, j: (b, i, 0)),
            pl.BlockSpec((1, 1, D), lambda b, i, j: (b, 0, 0)),
            pl.BlockSpec((1, 1, D), lambda b, i, j: (b, 0, 0)),
            pl.BlockSpec((D, tn), lambda b, i, j: (0, j)),
            pl.BlockSpec((1, tn), lambda b, i, j: (0, j)),
        ],
        out_specs=pl.BlockSpec((1, tm, tn), lambda b, i, j: (b, i, j)),
        out_shape=jax.ShapeDtypeStruct((B, S, N), BF16),
        scratch_shapes=[pltpu.VMEM((tm, D), BF16)],
        compiler_params=_cparams(("parallel", "parallel", "arbitrary")),
        name="inproj",
    )(x, mult, shift, w, bias)


LANES = 128


def _flash_kernel(*refs, nseg, tks, J, R, dv, diff):
    if diff:
        lam_ref, subg_ref = refs[:2]
        refs = refs[2:]
    q_ref = refs[0]
    k_refs = refs[1:1 + nseg]
    v_refs = refs[1 + nseg:1 + 2 * nseg]
    o_ref = refs[1 + 2 * nseg]
    m_scr, acc_scr = refs[2 + 2 * nseg:4 + 2 * nseg]
    s_scrs = refs[4 + 2 * nseg:]
    tq, d = q_ref.shape[-2:]
    rows = R * tq
    qs = [q_ref[0, 0, j].reshape(rows, d) for j in range(J)]
    m_scr[...] = jnp.full(m_scr.shape, NEG, F32)
    acc_scr[...] = jnp.zeros(acc_scr.shape, F32)

    def scores(j, k_ref, c, tk):
        return _dot_nt(qs[j], k_ref[0, 0, j, pl.ds(pl.multiple_of(c * tk, tk), tk), :])

    def accumulate(j, s, v_ref, c, tk):
        vc = v_ref[0, 0, j, pl.ds(pl.multiple_of(c * tk, tk), tk), :]
        slabs = [s[:, t * LANES:(t + 1) * LANES] for t in range(tk // LANES)]
        m_cur = functools.reduce(jnp.maximum, slabs)
        m_prev = m_scr[j]
        m_new = jnp.maximum(m_prev, jnp.max(m_cur, axis=1, keepdims=True))
        alpha = jnp.exp2(m_prev - m_new)
        p = jnp.concatenate([jnp.exp2((sl - m_new).astype(BF16)) for sl in slabs], axis=1)
        acc_scr[j] = alpha * acc_scr[j] + jnp.dot(p, vc, preferred_element_type=F32)
        m_scr[j] = m_new

    if nseg == 1:
        k_ref, v_ref, tk = k_refs[0], v_refs[0], tks[0]

        def body(c, carry):
            for j in range(J):
                accumulate(j, scores(j, k_ref, c, tk), v_ref, c, tk)
            return carry

        lax.fori_loop(0, k_ref.shape[3] // tk, body, 0)
    else:
        (kc_ref, k_ref), (vc_ref, v_ref), (tkc, tk) = k_refs, v_refs, tks
        assert kc_ref.shape[3] == tkc
        stages = [(kc_ref, vc_ref, 0, tkc)] + [(k_ref, v_ref, c, tk) for c in range(k_ref.shape[3] // tk)]
        bufs = (s_scrs[:J], s_scrs[J:])
        cur = [scores(j, stages[0][0], 0, tkc) for j in range(J)]
        for idx, (_, vr, c, t) in enumerate(stages):
            nxt = stages[idx + 1] if idx + 1 < len(stages) else None
            for j in range(J):
                if nxt is not None:
                    bufs[idx % 2][j][...] = scores(j, nxt[0], nxt[2], nxt[3])
                accumulate(j, cur[j], vr, c, t)
            if nxt is not None:
                cur = [bufs[idx % 2][j][...] for j in range(J)]
    pieces = []
    for j in range(J):
        acc = acc_scr[j]
        o = acc[:, :dv] / acc[:, dv:dv + 1]
        parts = [o[r * tq:(r + 1) * tq] for r in range(R)]
        if diff:
            y = parts[0] - lam_ref[...] * parts[1]
            y = y * lax.rsqrt(jnp.mean(y * y, axis=-1, keepdims=True) + EPS) * subg_ref[...]
            pieces.append(y)
        else:
            pieces.extend(parts)
    o_ref[0] = jnp.concatenate(pieces, axis=1).astype(o_ref.dtype)


def _flash(q, ks, vs, tq, tk, diff=None):
    B, Hs, J, R, Sq, d = q.shape
    dv = LANES // 2
    assert (J if diff else J * R) * dv == LANES
    tq = min(tq, Sq)
    tks = tuple(min(tk, k.shape[3]) for k in ks)
    in_specs = []
    if diff:
        in_specs += [pl.BlockSpec((1, dv), lambda b, h, i: (0, 0))] * 2
    in_specs.append(pl.BlockSpec((1, 1, J, R, tq, d), lambda b, h, i: (b, h, 0, 0, i, 0)))
    for k in ks:
        in_specs.append(pl.BlockSpec((1, 1, J, k.shape[3], d), lambda b, h, i: (b, h, 0, 0, 0)))
    for v in vs:
        in_specs.append(pl.BlockSpec((1, 1, J, v.shape[3], LANES), lambda b, h, i: (b, h, 0, 0, 0)))
    return pl.pallas_call(
        functools.partial(_flash_kernel, nseg=len(ks), tks=tks, J=J, R=R, dv=dv, diff=bool(diff)),
        grid=(B, Hs, Sq // tq),
        in_specs=in_specs,
        out_specs=pl.BlockSpec((1, tq, LANES), lambda b, h, i: (b, i, h)),
        out_shape=jax.ShapeDtypeStruct((B, Sq, Hs * LANES), BF16),
        scratch_shapes=[pltpu.VMEM((J, R * tq, LANES), F32), pltpu.VMEM((J, R * tq, LANES), F32)]
        + ([pltpu.VMEM((R * tq, tks[1]), F32)] * (2 * J) if len(ks) == 2 else []),
        compiler_params=_cparams(("parallel", "parallel", "arbitrary")),
        name="flash_diff" if diff else "flash",
    )(*(diff or ()), q, *ks, *vs)


NBR_HEADS = 2


def _nbr_kernel(q_ref, k0, k1, k2, k3, v0, v1, v2, v3, kc_ref, vc_ref, bias_ref, o_ref):
    outs = []
    for j in range(NBR_HEADS):
        q = q_ref[0, j]
        k = jnp.concatenate([k0[0, j], k1[0, j], k2[0, j], k3[0, j]], axis=0)
        v = jnp.concatenate([v0[0, j], v1[0, j], v2[0, j], v3[0, j]], axis=0)
        s_loc = _dot_nt(q, k) + bias_ref[0, j]
        s_ctx = _dot_nt(q, kc_ref[0, j])
        m = jnp.maximum(jnp.max(s_loc, axis=1, keepdims=True), jnp.max(s_ctx, axis=1, keepdims=True))
        p_loc = jnp.exp(s_loc - m)
        p_ctx = jnp.exp(s_ctx - m)
        l = jnp.sum(p_loc, axis=1, keepdims=True) + jnp.sum(p_ctx, axis=1, keepdims=True)
        o = (jnp.dot(p_ctx.astype(BF16), vc_ref[0, j], preferred_element_type=F32)
             + jnp.dot(p_loc.astype(BF16), v, preferred_element_type=F32))
        outs.append(o / l)
    o_ref[0] = jnp.concatenate(outs, axis=1).astype(o_ref.dtype)


def _nbr_bias(rpb, rows):
    kh, kw = WIN_H, WIN_W
    qc = jnp.arange(GRID_W)
    kc = jnp.arange(GRID_W)
    c0 = jnp.clip(qc - kw // 2, 0, GRID_W - kw)
    col_ok = (kc[None, :] >= c0[:, None]) & (kc[None, :] < c0[:, None] + kw)
    dj = kc[None, :] - qc[:, None] + WIN_W - 1
    oh_c = ((dj[..., None] == jnp.arange(2 * WIN_W - 1)) & col_ok[..., None]).astype(F32)

    def variant(r_start, k_start):
        r = r_start + jnp.arange(NBR_ROWS)
        kr = k_start + jnp.arange(NBR_KROWS)
        r0 = jnp.clip(r - kh // 2, 0, rows - kh)
        row_ok = (kr[None, :] >= r0[:, None]) & (kr[None, :] < r0[:, None] + kh)
        di = kr[None, :] - r[:, None] + WIN_H - 1
        oh_r = ((di[..., None] == jnp.arange(2 * WIN_H - 1)) & row_ok[..., None]).astype(F32)
        b = jnp.einsum('jka,hab,qcb->hjqkc', oh_r, rpb.astype(F32), oh_c, precision=lax.Precision.HIGHEST)
        valid = row_ok[:, None, :, None] & col_ok[None, :, None, :]
        b = jnp.where(valid[None], b, NEG)
        return b.reshape(rpb.shape[0], NBR_ROWS * GRID_W, NBR_KROWS * GRID_W)

    return jnp.stack([variant(0, 0), variant(NBR_ROWS, NBR_ROWS - kh // 2),
                      variant(rows - NBR_ROWS, rows - NBR_KROWS)])


def _nbr(q, k, v, kc, vc, bias):
    B, H, S, d = q.shape
    L = kc.shape[2]
    tq = NBR_ROWS * GRID_W
    kb = tq // 2
    nb = S // tq
    nh = NBR_HEADS
    assert S % tq == 0 and nb >= 3 and nh * d == LANES

    def kmap(j):
        return lambda h, i, b: (b, h, jnp.clip(2 * i - 1, 0, 2 * nb - 4) + j, 0)

    def bmap(h, i, b):
        return (jnp.where(i == 0, 0, jnp.where(i == nb - 1, 2, 1)), h, 0, 0)

    kv_specs = [pl.BlockSpec((1, nh, kb, d), kmap(j)) for j in range(4)]
    return pl.pallas_call(
        _nbr_kernel,
        grid=(H // nh, nb, B),
        in_specs=[pl.BlockSpec((1, nh, tq, d), lambda h, i, b: (b, h, i, 0))] + kv_specs + kv_specs + [
            pl.BlockSpec((1, nh, L, d), lambda h, i, b: (b, h, 0, 0)),
            pl.BlockSpec((1, nh, L, d), lambda h, i, b: (b, h, 0, 0)),
            pl.BlockSpec((1, nh, tq, NBR_KROWS * GRID_W), bmap),
        ],
        out_specs=pl.BlockSpec((1, tq, LANES), lambda h, i, b: (b, i, h)),
        out_shape=jax.ShapeDtypeStruct((B, S, H * d), BF16),
        compiler_params=_cparams(("parallel", "parallel", "parallel")),
        name="nbr_attn",
    )(q, k, k, k, k, v, v, v, v, kc, vc, bias)


def _gelu(x):
    return 0.5 * x * (1.0 + jnp.tanh(math.sqrt(2.0 / math.pi) * (x + 0.044715 * (x * x * x))))


def _sgu_kernel(u_ref, v_ref, g_ref, w_ref, b_ref, o_ref, *, nchunk):
    grp = lax.broadcasted_iota(I32, (SGU_CHUNK, BRANCH_W), 1) // (BRANCH_W // SGU_GROUPS)
    for c in range(nchunk):
        rows = slice(c * SGU_CHUNK, (c + 1) * SGU_CHUNK)
        u = u_ref[0, rows, :].astype(F32)
        v = _gelu(v_ref[0, rows, :].astype(F32))
        mu = jnp.mean(v, axis=-1, keepdims=True)
        var = jnp.mean(jnp.square(v - mu), axis=-1, keepdims=True)
        vn = ((v - mu) * lax.rsqrt(var + EPS) * g_ref[...]).astype(BF16)
        mixed = b_ref[...]
        for g in range(SGU_GROUPS):
            mg = jnp.dot(w_ref[g], vn, preferred_element_type=F32)
            mixed = mixed + jnp.where(grp == g, mg, 0.0)
        o_ref[0, rows, :] = (_gelu(u) * mixed).astype(o_ref.dtype)


def _sgu(p, norm_g, w_s, b_full, tt):
    B, S, _ = p.shape
    tt = min(tt, S)
    return pl.pallas_call(
        functools.partial(_sgu_kernel, nchunk=tt // SGU_CHUNK),
        grid=(B, S // tt),
        in_specs=[
            pl.BlockSpec((1, tt, BRANCH_W), lambda b, i: (b, i, SEG_U)),
            pl.BlockSpec((1, tt, BRANCH_W), lambda b, i: (b, i, SEG_V)),
            pl.BlockSpec((1, BRANCH_W), lambda b, i: (0, 0)),
            pl.BlockSpec((SGU_GROUPS, SGU_CHUNK, SGU_CHUNK), lambda b, i: (0, 0, 0)),
            pl.BlockSpec((SGU_CHUNK, BRANCH_W), lambda b, i: (0, 0)),
        ],
        out_specs=pl.BlockSpec((1, tt, BRANCH_W), lambda b, i: (b, i, 0)),
        out_shape=jax.ShapeDtypeStruct((B, S, BRANCH_W), BF16),
        compiler_params=_cparams(("parallel", "parallel")),
        name="sgu",
    )(p, p, norm_g, w_s, b_full)


def _merge_kernel(ya, yb, yc, yd, g0, g1, g2, g3, wb_ref, wo_ref, x_ref, gate_ref, mult_ref, shift_ref,
                  wr_ref, xo_ref, h2_ref, lg_ref):
    mix = None
    for n, (y, g) in enumerate(((ya, g0), (yb, g1), (yc, g2), (yd, g3))):
        pr = jnp.dot(y[0], wb_ref[n], preferred_element_type=F32)
        t = g[0].astype(F32) * pr
        mix = t if mix is None else mix + t
    out = jnp.dot(mix.astype(BF16), wo_ref[...], preferred_element_type=F32)
    xn = x_ref[0] + gate_ref[0] * out
    xo_ref[0] = xn
    ms = jnp.mean(xn * xn, axis=-1, keepdims=True)
    h2 = xn * lax.rsqrt(ms + EPS) * mult_ref[0] + shift_ref[0]
    h2_ref[0] = h2.astype(BF16)
    lg_ref[0] = lax.dot_general(wr_ref[...], h2, (((1,), (1,)), ((), ())),
                                precision=lax.Precision.HIGHEST, preferred_element_type=F32)


def _merge(ys, p, w_branch, w_out, x, gate, mult2, shift2, w_router_t, tm):
    B, S, D = x.shape
    E = w_router_t.shape[0]
    tm = min(tm, S)
    y_spec = pl.BlockSpec((1, tm, BRANCH_W), lambda b, i: (b, i, 0))
    g_specs = [pl.BlockSpec((1, tm, D), functools.partial(lambda b, i, n: (b, i, n), n=n)) for n in range(N_BRANCH)]
    vec = pl.BlockSpec((1, 1, D), lambda b, i: (b, 0, 0))
    return pl.pallas_call(
        _merge_kernel,
        grid=(B, S // tm),
        in_specs=[y_spec] * 4 + g_specs + [
            pl.BlockSpec((N_BRANCH, BRANCH_W, D), lambda b, i: (0, 0, 0)),
            pl.BlockSpec((D, D), lambda b, i: (0, 0)),
            pl.BlockSpec((1, tm, D), lambda b, i: (b, i, 0)),
            vec, vec, vec,
            pl.BlockSpec((E, D), lambda b, i: (0, 0)),
        ],
        out_specs=[
            pl.BlockSpec((1, tm, D), lambda b, i: (b, i, 0)),
            pl.BlockSpec((1, tm, D), lambda b, i: (b, i, 0)),
            pl.BlockSpec((1, E, tm), lambda b, i: (b, 0, i)),
        ],
        out_shape=[
            jax.ShapeDtypeStruct((B, S, D), F32),
            jax.ShapeDtypeStruct((B, S, D), BF16),
            jax.ShapeDtypeStruct((B, E, S), F32),
        ],
        compiler_params=_cparams(("parallel", "parallel")),
        name="merge",
    )(*ys, p, p, p, p, w_branch, w_out, x, gate, mult2, shift2, w_router_t)


def _cumsum_excl(x, tri):
    n = x.shape[1]
    outs = []
    carry = jnp.zeros((x.shape[0], 1), F32)
    for c in range(n // 128):
        xc = x[:, c * 128:(c + 1) * 128]
        outs.append(jnp.dot(xc.astype(BF16), tri, preferred_element_type=F32) + carry)
        carry = carry + jnp.sum(xc, axis=1, keepdims=True)
    return jnp.concatenate(outs, axis=1)


def _route_kernel(lg_ref, rank_ref, score_ref, count_ref, *, cap):
    lg = lg_ref[0]
    mx = jnp.max(lg, axis=0, keepdims=True)
    ex = jnp.exp(lg - mx)
    aff = ex / jnp.sum(ex, axis=0, keepdims=True)
    E = lg.shape[0]

    def bisect(i, thr_bits):
        cand = thr_bits | jnp.left_shift(jnp.int32(1), 30 - i)
        cnt = jnp.sum(jnp.where(aff >= pltpu.bitcast(cand, F32), 1, 0), axis=1, keepdims=True)
        return jnp.where(cnt >= cap, cand, thr_bits)

    thr = pltpu.bitcast(lax.fori_loop(0, 31, bisect, jnp.zeros((E, 1), I32)), F32)
    gt = aff > thr
    eq = aff == thr
    need = (cap - jnp.sum(jnp.where(gt, 1, 0), axis=1, keepdims=True)).astype(F32)
    ri = lax.broadcasted_iota(I32, (128, 128), 0)
    ci = lax.broadcasted_iota(I32, (128, 128), 1)
    tri = jnp.where(ri < ci, 1.0, 0.0).astype(BF16)
    eq_before = _cumsum_excl(jnp.where(eq, 1.0, 0.0), tri)
    sel = gt | (eq & (eq_before < need))
    rank = _cumsum_excl(jnp.where(sel, 1.0, 0.0), tri)
    sel = sel & (rank < cap)
    rank_ref[0] = jnp.where(sel, rank.astype(I32), -1)
    score_ref[0] = jnp.where(sel, aff, 0.0)
    n = lg.shape[1]
    chunk_of_token = lax.broadcasted_iota(I32, (n, LANES), 0) // MOE_CHUNK
    member = jnp.where(chunk_of_token == lax.broadcasted_iota(I32, (n, LANES), 1), 1.0, 0.0).astype(BF16)
    count_ref[0] = jnp.dot(jnp.where(sel, 1.0, 0.0).astype(BF16), member, preferred_element_type=F32).astype(I32)


def _route(logits, cap):
    B, E, n = logits.shape
    assert n // MOE_CHUNK <= LANES
    spec = pl.BlockSpec((1, E, n), lambda b: (b, 0, 0))
    rank, score, count = pl.pallas_call(
        functools.partial(_route_kernel, cap=cap),
        grid=(B,),
        in_specs=[spec],
        out_specs=[spec, spec, pl.BlockSpec((1, E, LANES), lambda b: (b, 0, 0))],
        out_shape=[jax.ShapeDtypeStruct((B, E, n), I32), jax.ShapeDtypeStruct((B, E, n), F32),
                   jax.ShapeDtypeStruct((B, E, LANES), I32)],
        compiler_params=_cparams(("parallel",)),
        name="route",
    )(logits)
    return rank, score, count[..., :n // MOE_CHUNK]


SLOT_ALIGN = 16


def _window_start(first, j, W, cap):
    return pl.multiple_of(jnp.minimum((first // SLOT_ALIGN) * SLOT_ALIGN + j * W, cap - W), SLOT_ALIGN)


def _windows_needed(lo, hi, W):
    return (hi - (lo // SLOT_ALIGN) * SLOT_ALIGN + W - 1) // W


def _gather_kernel(cnt_ref, rank_ref, h_ref, o_ref, *, nc, W, unroll, flag_off):
    b = pl.program_id(0)
    e = pl.program_id(1)
    be = b * pl.num_programs(1) + e
    base = be * (nc + 1)
    cap = o_ref.shape[2]
    o_ref[...] = jnp.zeros(o_ref.shape, o_ref.dtype)
    T = MOE_CHUNK

    def window(c, j):
        lo = cnt_ref[base + c]
        r = rank_ref[0, 0, pl.ds(c, 1), :]
        hc = h_ref[0, pl.ds(pl.multiple_of(c * T, T), T), :]
        st = _window_start(lo, j, W, cap)
        slot = lax.broadcasted_iota(I32, (W, T), 0) + st
        hit = (slot == r) & (slot >= (lo // SLOT_ALIGN) * SLOT_ALIGN + j * W)
        got = jnp.dot(jnp.where(hit, 1.0, 0.0).astype(BF16), hc, preferred_element_type=F32)
        o_ref[0, 0, pl.ds(st, W), :] = o_ref[0, 0, pl.ds(st, W), :] + got.astype(o_ref.dtype)

    def group(g, carry):
        for u in range(unroll):
            window(g * unroll + u, 0)
        return carry

    lax.fori_loop(0, nc // unroll, group, 0)

    def tail(c, carry):
        def more(j, carry2):
            window(c, j)
            return carry2

        lax.fori_loop(1, _windows_needed(cnt_ref[base + c], cnt_ref[base + c + 1], W), more, 0)
        return carry

    @pl.when(cnt_ref[flag_off + be] > 0)
    def _():
        lax.fori_loop(0, nc, tail, 0)


def _moe_window(cap):
    return min(128, cap)


def _moe_tables(per_chunk, cap, tt, eg):
    B, E, nc = per_chunk.shape
    n = nc * MOE_CHUNK
    W = _moe_window(cap)
    cnt = jnp.concatenate([jnp.zeros((B, E, 1), I32), jnp.cumsum(per_chunk, axis=-1, dtype=I32)], axis=-1)
    extra = _windows_needed(cnt[..., :-1], cnt[..., 1:], W) > 1
    g_flag = jnp.any(extra, axis=-1)
    nsub = min(tt, n) // MOE_CHUNK
    s_flag = jnp.any(extra.reshape(B, E // eg, eg, nc // nsub, nsub), axis=(2, 4))
    tbl = jnp.concatenate([cnt.reshape(-1), g_flag.reshape(-1).astype(I32), s_flag.reshape(-1).astype(I32)])
    return tbl, cnt.size, cnt.size + g_flag.size


def _gather(tbl, flag_off, rank, h, cap):
    B, E, n = rank.shape
    D = h.shape[-1]
    nc = n // MOE_CHUNK
    return pl.pallas_call(
        functools.partial(_gather_kernel, nc=nc, W=_moe_window(cap), unroll=min(4, nc), flag_off=flag_off),
        grid_spec=pltpu.PrefetchScalarGridSpec(
            num_scalar_prefetch=1,
            grid=(B, E),
            in_specs=[
                pl.BlockSpec((1, 1, nc, MOE_CHUNK), lambda b, e, tbl: (b, e, 0, 0)),
                pl.BlockSpec((1, n, D), lambda b, e, tbl: (b, 0, 0)),
            ],
            out_specs=pl.BlockSpec((1, 1, cap, D), lambda b, e, tbl: (b, e, 0, 0)),
        ),
        out_shape=jax.ShapeDtypeStruct((B, E, cap, D), BF16),
        compiler_params=_cparams(("parallel", "arbitrary")),
        name="moe_gather",
    )(tbl, rank.reshape(B, E, nc, MOE_CHUNK), h)


def _ffn_kernel(*refs, nsets):
    x_refs = refs[:nsets]
    wg_ref, wu_ref, wd_ref = refs[nsets:nsets + 3]
    o_refs = refs[nsets + 3:2 * nsets + 3]
    acc_scrs = refs[2 * nsets + 3:]
    f = pl.program_id(1)

    @pl.when(f == 0)
    def _():
        for acc_scr in acc_scrs:
            acc_scr[...] = jnp.zeros(acc_scr.shape, F32)

    wg = wg_ref[0, 0].astype(BF16)
    wu = wu_ref[0, 0].astype(BF16)
    wd = wd_ref[0, 0].astype(BF16)
    for x_ref, acc_scr in zip(x_refs, acc_scrs):
        for b in range(x_ref.shape[0]):
            x = x_ref[b, 0]
            g = jnp.dot(x, wg, preferred_element_type=F32)
            u = jnp.dot(x, wu, preferred_element_type=F32)
            hid = (g * (0.5 + 0.5 * jnp.tanh(0.5 * g)) * u).astype(BF16)
            acc_scr[b] = acc_scr[b] + jnp.dot(hid, wd, preferred_element_type=F32)

    @pl.when(f == pl.num_programs(1) - 1)
    def _():
        for o_ref, acc_scr in zip(o_refs, acc_scrs):
            o_ref[:, 0] = acc_scr[...].astype(o_ref.dtype)


def _ffn(xins, w_gate, w_up, w_down, layer, tf):
    E, D = xins[0].shape[1], xins[0].shape[3]
    Fh = w_gate.shape[-1]
    x_specs = [pl.BlockSpec((x.shape[0], 1, x.shape[2], D), lambda e, f: (0, e, 0, 0)) for x in xins]
    return pl.pallas_call(
        functools.partial(_ffn_kernel, nsets=len(xins)),
        grid=(E, Fh // tf),
        in_specs=x_specs + [
            pl.BlockSpec((1, 1, D, tf), lambda e, f: (layer, e, 0, f)),
            pl.BlockSpec((1, 1, D, tf), lambda e, f: (layer, e, 0, f)),
            pl.BlockSpec((1, 1, tf, D), lambda e, f: (layer, e, f, 0)),
        ],
        out_specs=x_specs,
        out_shape=[jax.ShapeDtypeStruct(x.shape, BF16) for x in xins],
        scratch_shapes=[pltpu.VMEM((x.shape[0], x.shape[2], D), F32) for x in xins],
        compiler_params=_cparams(("parallel", "arbitrary")),
        name="moe_ffn",
    )(*xins, w_gate, w_up, w_down)


SCATTER_EG = 4
SCATTER_TT = 1024


def _scatter_kernel(cnt_ref, rank_ref, score_ref, y_ref, x_ref, g2_ref, fg_ref, o_ref, acc_scr, *,
                    nc, W, final, flag_off):
    b = pl.program_id(0)
    i = pl.program_id(1)
    g = pl.program_id(2)
    G = pl.num_programs(2)
    eg = y_ref.shape[1]
    T = MOE_CHUNK
    nsub = acc_scr.shape[0] // T
    cap = y_ref.shape[2]

    @pl.when(g == 0)
    def _():
        acc_scr[...] = jnp.zeros(acc_scr.shape, F32)

    def window(c, ee, j):
        rows = slice(c * T, (c + 1) * T)
        lo = cnt_ref[((b * G + g) * eg + ee) * (nc + 1) + i * nsub + c]
        rc = rank_ref[0, 0, rows, ee:ee + 1]
        sc = score_ref[0, 0, rows, ee:ee + 1]
        st = _window_start(lo, j, W, cap)
        slot = lax.broadcasted_iota(I32, (T, W), 1) + st
        hit = (slot == rc) & (slot >= (lo // SLOT_ALIGN) * SLOT_ALIGN + j * W)
        got = jnp.dot(jnp.where(hit, 1.0, 0.0).astype(BF16), y_ref[0, ee, pl.ds(st, W), :],
                      preferred_element_type=F32)
        return sc * got

    for c in range(nsub):
        rows = slice(c * T, (c + 1) * T)
        acc_scr[rows, :] = acc_scr[rows, :] + functools.reduce(jnp.add, [window(c, ee, 0) for ee in range(eg)])

    @pl.when(cnt_ref[flag_off + (b * G + g) * pl.num_programs(1) + i] > 0)
    def _():
        for c in range(nsub):
            for ee in range(eg):
                def more(j, carry, c=c, ee=ee):
                    rows = slice(c * T, (c + 1) * T)
                    acc_scr[rows, :] = acc_scr[rows, :] + window(c, ee, j)
                    return carry

                base = ((b * G + g) * eg + ee) * (nc + 1) + i * nsub + c
                lax.fori_loop(1, _windows_needed(cnt_ref[base], cnt_ref[base + 1], W), more, 0)

    @pl.when(g == G - 1)
    def _():
        xn = x_ref[0] + g2_ref[0] * acc_scr[...]
        if final:
            ms = jnp.mean(xn * xn, axis=-1, keepdims=True)
            xn = xn * lax.rsqrt(ms + EPS) * fg_ref[...]
        o_ref[0] = xn


def _scatter(tbl, flag_off, rank, score, y, x, g2, final_g, final):
    B, E, n = rank.shape
    cap, D = y.shape[2], y.shape[3]
    tt = min(SCATTER_TT, n)
    eg = SCATTER_EG
    nc = n // MOE_CHUNK

    def token_major(t):
        return t.reshape(B, E // eg, eg, n).transpose(0, 1, 3, 2)

    return pl.pallas_call(
        functools.partial(_scatter_kernel, nc=nc, W=_moe_window(cap), final=final, flag_off=flag_off),
        grid_spec=pltpu.PrefetchScalarGridSpec(
            num_scalar_prefetch=1,
            grid=(B, n // tt, E // eg),
            in_specs=[
                pl.BlockSpec((1, 1, tt, eg), lambda b, i, g, tbl: (b, g, i, 0)),
                pl.BlockSpec((1, 1, tt, eg), lambda b, i, g, tbl: (b, g, i, 0)),
                pl.BlockSpec((1, eg, cap, D), lambda b, i, g, tbl: (b, g, 0, 0)),
                pl.BlockSpec((1, tt, D), lambda b, i, g, tbl: (b, i, 0)),
                pl.BlockSpec((1, 1, D), lambda b, i, g, tbl: (b, 0, 0)),
                pl.BlockSpec((1, D), lambda b, i, g, tbl: (0, 0)),
            ],
            out_specs=pl.BlockSpec((1, tt, D), lambda b, i, g, tbl: (b, i, 0)),
            scratch_shapes=[pltpu.VMEM((tt, D), F32)],
        ),
        out_shape=jax.ShapeDtypeStruct((B, n, D), F32),
        compiler_params=_cparams(("parallel", "parallel", "arbitrary")),
        name="moe_scatter",
    )(tbl, token_major(rank), token_major(score), y, x, g2, final_g)


def _moe_dispatch(h2, logits):
    n = h2.shape[1]
    cap = CAPACITY_FACTOR * n // N_EXPERTS
    rank, score, per_chunk = _route(logits, cap)
    tbl, g_off, s_off = _moe_tables(per_chunk, cap, SCATTER_TT, SCATTER_EG)
    return (tbl, s_off, rank, score), _gather(tbl, g_off, rank, h2, cap)


def _moe_combine(routing, y, x, g2, final_g, final):
    tbl, s_off, rank, score = routing
    return _scatter(tbl, s_off, rank, score, y, x, g2, final_g, final)


LOG2E = math.log2(math.e)
HEADS_PER_BLK = COL_BLK // HEAD_DIM


def _prep_kernel(*refs, rope, qa_scale):
    qa_r, qb_r, qd_r, ka_r, va_r, kb_r, vb_r, kvd_r = refs[:8]
    refs = refs[8:]
    if rope:
        cb_r, sb_r, cd_r, sd_r = refs[:4]
        refs = refs[4:]
    qg_r, kg_r, gm_r = refs[:3]
    qa_o, qb_o, qd_o, ka_o, va_o, kb_o, vb_o, kd_o, vd_o = refs[3:]
    ts = qa_r.shape[1]
    kvw = GQA_KV_HEADS * HEAD_DIM

    def partner(x, half):
        n = x.shape[1]
        lane = lax.broadcasted_iota(I32, x.shape, 1)
        return jnp.where((lane & half) == 0, pltpu.roll(x, n - half, 1), pltpu.roll(x, half, 1))

    def rot(x, cos, sin, half):
        return x * cos + partner(x, half) * sin

    def group_norm(x, gm, gain):
        ms = jnp.dot(x * x, gm, precision=lax.Precision.HIGHEST, preferred_element_type=F32)
        return x * lax.rsqrt(ms + EPS) * gain

    def head(x, h):
        return x[:, h * HEAD_DIM:(h + 1) * HEAD_DIM]

    lane64 = lax.broadcasted_iota(I32, (ts, HEAD_DIM), 1)
    ones_tail = jnp.where(lane64 == 0, 1.0, 0.0).astype(BF16)

    qa = qa_r[0].astype(F32) * qa_scale
    for h in range(HEADS_PER_BLK):
        qa_o[0, h] = head(qa, h).astype(BF16)
        ka_o[0, h] = head(ka_r[0], h)
        va_o[0, h] = head(va_r[0], h)

    qb = qb_r[0].astype(F32)
    kb = kb_r[0].astype(F32)
    if rope:
        qb = rot(qb, cb_r[...], sb_r[...], DIFF_QK_DIM // 2)
        kb = rot(kb, cb_r[...], sb_r[...], DIFF_QK_DIM // 2)
    qb = qb * (DIFF_QK_DIM ** -0.5 * LOG2E)
    for h in range(HEADS_PER_BLK):
        xh = head(qb, h)
        qb_o[0, h // 2, h % 2, 0] = jnp.where(lane64 < DIFF_QK_DIM, xh, 0.0).astype(BF16)
        qb_o[0, h // 2, h % 2, 1] = jnp.where(lane64 >= DIFF_QK_DIM, xh, 0.0).astype(BF16)
        kb_o[0, h // 2, h % 2] = head(kb, h).astype(BF16)
        vb_o[0, h // 2, h % 2] = jnp.concatenate([head(vb_r[0], h), ones_tail], axis=1)

    qd = group_norm(qd_r[0].astype(F32), gm_r[...], qg_r[...])
    kd = group_norm(kvd_r[0, :, :kvw].astype(F32), gm_r[:kvw, :kvw], kg_r[...])
    if rope:
        qd = rot(qd, cd_r[...], sd_r[...], HEAD_DIM // 2)
        kd = rot(kd, cd_r[:, :kvw], sd_r[:, :kvw], HEAD_DIM // 2)
    qd = qd * (HEAD_DIM ** -0.5 * LOG2E)
    grp = GQA_Q_HEADS // GQA_KV_HEADS
    for h in range(GQA_Q_HEADS):
        qd_o[0, h // grp, 0, h % grp] = head(qd, h).astype(BF16)
    vd = kvd_r[0, :, kvw:]
    for h in range(GQA_KV_HEADS):
        kd_o[0, h, 0] = head(kd, h).astype(BF16)
        vd_o[0, h, 0] = jnp.concatenate([head(vd, h), ones_tail], axis=1)


def _prep(p, tables, qg, kg, gm, qa_scale, ts):
    B, S, _ = p.shape
    ts = min(ts, S)

    def seg(blk):
        return pl.BlockSpec((1, ts, COL_BLK), lambda b, i: (b, i, blk))

    in_specs = [seg(s) for s in (SEG_QA, SEG_QB, SEG_QD, SEG_KA, SEG_VA, SEG_KB, SEG_VB, SEG_VB + 1)]
    args = [p] * 8
    if tables is not None:
        in_specs += [pl.BlockSpec((ts, COL_BLK), lambda b, i: (i, 0))] * 4
        args += list(tables)
    in_specs += [pl.BlockSpec(a.shape, lambda b, i: (0, 0)) for a in (qg, kg, gm)]
    args += [qg, kg, gm]

    def out(lead, width):
        shape = (B,) + lead + (S, width)
        nl = len(lead)
        spec = pl.BlockSpec((1,) + lead + (ts, width), lambda b, i: (b,) + (0,) * nl + (i, 0))
        return jax.ShapeDtypeStruct(shape, BF16), spec

    grp = GQA_Q_HEADS // GQA_KV_HEADS
    outs = [out((NA_HEADS,), HEAD_DIM), out((DIFF_HEADS // 2, 2, 2), HEAD_DIM),
            out((GQA_KV_HEADS, 1, grp), HEAD_DIM), out((NA_HEADS,), HEAD_DIM), out((NA_HEADS,), HEAD_DIM),
            out((DIFF_HEADS // 2, 2), HEAD_DIM), out((DIFF_HEADS // 2, 2), LANES),
            out((GQA_KV_HEADS, 1), HEAD_DIM), out((GQA_KV_HEADS, 1), LANES)]
    return pl.pallas_call(
        functools.partial(_prep_kernel, rope=tables is not None, qa_scale=qa_scale),
        grid=(B, S // ts),
        in_specs=in_specs,
        out_specs=[o[1] for o in outs],
        out_shape=[o[0] for o in outs],
        compiler_params=_cparams(("parallel", "parallel")),
        name="attn_prep",
    )(*args)


def _rope_lane_tables(n, dim):
    t = jnp.arange(n)
    row = (t // GRID_W).astype(F32)
    col = (t % GRID_W).astype(F32)
    n_pairs = dim // 4
    inv = ROPE_THETA ** (-jnp.arange(n_pairs, dtype=F32) / n_pairs)
    ang = jnp.concatenate([row[:, None] * inv, col[:, None] * inv], axis=-1)
    cos, sin = jnp.cos(ang), jnp.sin(ang)
    reps = COL_BLK // dim
    return jnp.tile(jnp.concatenate([cos, cos], -1), (1, reps)), jnp.tile(jnp.concatenate([-sin, sin], -1), (1, reps))


def _deinterleave(w, width):
    lead = w.shape[:-1]
    n = w.shape[-1]
    return jnp.swapaxes(w.reshape(lead + (n // width, width // 2, 2)), -1, -2).reshape(lead + (n,))


def _with_ones(v):
    pad = [(0, 0)] * (v.ndim - 1) + [(0, LANES - v.shape[-1] - 1)]
    return jnp.pad(jnp.concatenate([v, jnp.ones(v.shape[:-1] + (1,), v.dtype)], axis=-1), pad)


def _mixer_branches(p, ops, ops_ctx, bias_a, lam, lam_init, sub_g, sgu_g, sgu_w, sgu_bf):
    qa, qb, qd, ka, va, kb, vb, kd, vd = ops
    b, _, s, _ = qa.shape
    if ops_ctx is None:
        y_a = _flash(qa.reshape(b, NA_HEADS // 2, 2, 1, s, HEAD_DIM), [ka.reshape(b, NA_HEADS // 2, 2, s, HEAD_DIM)],
                     [_with_ones(va).reshape(b, NA_HEADS // 2, 2, s, LANES)], tq=256, tk=256)
        kbs, vbs, kds, vds = [kb], [vb], [kd], [vd]
    else:
        _, _, _, kac, vac, kbc, vbc, kdc, vdc = ops_ctx
        y_a = _nbr(qa, ka, va, kac, vac, bias_a)
        kbs, vbs, kds, vds = [kbc, kb], [vbc, vb], [kdc, kd], [vdc, vd]
    lam_v = jnp.full((1, DIFF_V_DIM), lam, F32)
    gain_v = (sub_g.astype(F32) * (1.0 - lam_init))[None, :]
    y_b = _flash(qb, kbs, vbs, tq=256, tk=2048, diff=(lam_v, gain_v))
    y_d = _flash(qd, kds, vds, tq=512, tk=2048)
    y_c = _sgu(p, sgu_g, sgu_w, sgu_bf, tt=1024)
    return [y_a, y_b, y_c, y_d]


def kernel(x, c, ctx, c_ctx, w_mod, b_mod, norm1_g, norm2_g, w_in, q_gain, k_gain, na_rpb, lambda_q1, lambda_k1, lambda_q2, lambda_k2, diff_sub_g, sgu_norm_g, sgu_w, sgu_b, gate_b, w_branch, w_out, w_router, w_e_gate, w_e_up, w_e_down, final_g):
    B, S, D = x.shape
    depth = w_mod.shape[0]
    rows = S // GRID_W
    tables = _rope_lane_tables(S, DIFF_QK_DIM) + _rope_lane_tables(S, HEAD_DIM)
    hp = lax.Precision.HIGHEST
    fg = final_g.astype(F32)[None, :]
    grp_id = jnp.arange(COL_BLK) // HEAD_DIM
    gm = (grp_id[:, None] == grp_id[None, :]).astype(F32) / HEAD_DIM
    for l in range(depth):
        last = l == depth - 1
        mod = jnp.dot(jax.nn.silu(c), w_mod[l], precision=hp) + b_mod[l]
        sh1, sc1, g1, sh2, sc2, g2 = (t[:, None, :] for t in jnp.split(mod, 6, axis=-1))
        mod_c = jnp.dot(jax.nn.silu(c_ctx), w_mod[l], precision=hp) + b_mod[l]
        csh1, csc1, cg1, csh2, csc2, cg2 = (jnp.broadcast_to(t[None, None, :], (B, 1, D))
                                            for t in jnp.split(mod_c, 6, axis=-1))
        lam_init = 0.8 - 0.6 * math.exp(-0.3 * l)
        lam = (jnp.exp(jnp.sum(lambda_q1[l].astype(F32) * lambda_k1[l].astype(F32)))
               - jnp.exp(jnp.sum(lambda_q2[l].astype(F32) * lambda_k2[l].astype(F32))) + lam_init)

        wl = w_in[l]
        q0, kv0 = 0, wl.shape[-1] - 5 * COL_BLK

        def cols(start, blk, n=1):
            return wl[:, start + blk * COL_BLK:start + (blk + n) * COL_BLK]

        kdw = GQA_KV_HEADS * HEAD_DIM
        w_l = jnp.concatenate([
            wl[:, 5 * COL_BLK:kv0],
            cols(q0, 0), _deinterleave(cols(q0, 1), DIFF_QK_DIM), _deinterleave(cols(q0, 2), HEAD_DIM),
            cols(q0, 3, 2),
            cols(kv0, 0, 2), _deinterleave(cols(kv0, 2), DIFF_QK_DIM), cols(kv0, 3),
            _deinterleave(wl[:, kv0 + 4 * COL_BLK:kv0 + 4 * COL_BLK + kdw], HEAD_DIM),
            wl[:, kv0 + 4 * COL_BLK + kdw:],
        ], axis=1).astype(BF16)
        qg = jnp.tile(_deinterleave(q_gain[l].astype(F32), HEAD_DIM), GQA_Q_HEADS)[None, :]
        kg = jnp.tile(_deinterleave(k_gain[l].astype(F32), HEAD_DIM), GQA_KV_HEADS)[None, :]
        bias_in = jnp.concatenate([gate_b[l].astype(F32), jnp.zeros((w_l.shape[1] - GATE_W,), F32)])[None, :]
        n1 = norm1_g[l].astype(F32)[None, None, :]
        n2 = norm2_g[l].astype(F32)[None, None, :]
        wb = w_branch[l].astype(BF16)
        wo = w_out[l].astype(BF16)
        wr_t = w_router[l].astype(F32).T
        sgu_g = sgu_norm_g[l].astype(F32)[None, :]
        sgu_wb = sgu_w[l].astype(BF16)
        sgu_bf = jnp.repeat(sgu_b[l].astype(F32).T, BRANCH_W // SGU_GROUPS, axis=1)
        bias_a = _nbr_bias(na_rpb[l], rows)

        tn = w_l.shape[1] // 4
        p = _inproj(x, n1 * (1.0 + sc1), sh1, w_l, bias_in, GATE_W, tm=2048, tn=tn)
        ops = _prep(p, tables, qg, kg, gm, HEAD_DIM ** -0.5, ts=512)
        pc = _inproj(ctx, n1 * (1.0 + csc1), csh1, w_l, bias_in, GATE_W, tm=256, tn=tn)
        ops_c = _prep(pc, None, qg, kg, gm, HEAD_DIM ** -0.5 * LOG2E, ts=256)
        if not last:
            ys_c = _mixer_branches(pc, ops_c, None, None, lam, lam_init, diff_sub_g[l], sgu_g, sgu_wb, sgu_bf)
            ctx, hc2, lg_c = _merge(ys_c, pc, wb, wo, ctx, cg1, n2 * (1.0 + csc2), csh2, wr_t, tm=256)
        ys = _mixer_branches(p, ops, ops_c, bias_a, lam, lam_init, diff_sub_g[l], sgu_g, sgu_wb, sgu_bf)
        x, h2, lg = _merge(ys, p, wb, wo, x, g1, n2 * (1.0 + sc2), sh2, wr_t, tm=512)

        routings, xins = zip(*([_moe_dispatch(h2, lg)] + ([] if last else [_moe_dispatch(hc2, lg_c)])))
        ys_e = _ffn(list(xins), w_e_gate, w_e_up, w_e_down, l, tf=256)
        x = _moe_combine(routings[0], ys_e[0], x, g2, fg, final=last)
        if not last:
            ctx = _moe_combine(routings[1], ys_e[1], ctx, cg2, fg, final=False)
    return x
```

```python
import functools
import math

import jax
import jax.numpy as jnp
from jax import lax
from jax.experimental import pallas as pl
from jax.experimental.pallas import tpu as pltpu

F32 = jnp.float32
BF16 = jnp.bfloat16
I32 = jnp.int32

GRID_W = 64
HEAD_DIM = 64
N_BRANCH = 4
BRANCH_W = 256
NA_HEADS = 4
WIN_H = 8
WIN_W = 16
DIFF_HEADS = 4
DIFF_QK_DIM = 32
DIFF_V_DIM = 64
SGU_GROUPS = 4
SGU_CHUNK = 128
GQA_Q_HEADS = 4
GQA_KV_HEADS = 2
ROPE_THETA = 10000.0
N_EXPERTS = 16
CAPACITY_FACTOR = 2
EPS = 1e-6
NEG = -1e30

GATE_W = N_BRANCH * 1024
COL_BLK = 256
SEG_QA, SEG_QB, SEG_QD, SEG_U, SEG_V, SEG_KA, SEG_VA, SEG_KB, SEG_VB = range(16, 25)
COL_KD = GATE_W + 9 * COL_BLK
KV_COL0 = GATE_W + 5 * COL_BLK

VMEM_LIMIT = 56 * 1024 * 1024

NBR_ROWS = 8
NBR_KROWS = 16
MOE_CHUNK = 256


def _cparams(sem):
    return pltpu.CompilerParams(dimension_semantics=sem, vmem_limit_bytes=VMEM_LIMIT)


def _dot_nt(a, b):
    return lax.dot_general(a, b, (((1,), (1,)), ((), ())), preferred_element_type=F32)


def _inproj_kernel(x_ref, mult_ref, shift_ref, w_ref, bias_ref, o_ref, h_scr, *, n_gate_cols):
    j = pl.program_id(2)
    tn = o_ref.shape[-1]

    @pl.when(j == 0)
    def _():
        x = x_ref[0]
        ms = jnp.mean(x * x, axis=-1, keepdims=True)
        h = x * lax.rsqrt(ms + EPS) * mult_ref[0] + shift_ref[0]
        h_scr[...] = h.astype(BF16)

    acc = jnp.dot(h_scr[...], w_ref[...], preferred_element_type=F32) + bias_ref[...]

    @pl.when(j * tn < n_gate_cols)
    def _():
        col = j * tn + lax.broadcasted_iota(I32, acc.shape, 1)
        o_ref[0] = jnp.where(col < n_gate_cols, 0.5 + 0.5 * jnp.tanh(0.5 * acc), acc).astype(o_ref.dtype)

    @pl.when(j * tn >= n_gate_cols)
    def _():
        o_ref[0] = acc.astype(o_ref.dtype)


def _inproj(x, mult, shift, w, bias, n_gate_cols, tm, tn):
    B, S, D = x.shape
    N = w.shape[1]
    tm = min(tm, S)
    return pl.pallas_call(
        functools.partial(_inproj_kernel, n_gate_cols=n_gate_cols),
        grid=(B, S // tm, N // tn),
        in_specs=[
            pl.BlockSpec((1, tm, D), lambda b, i, j: (b, i, 0)),
            pl.BlockSpec((1, 1, D), lambda b, i, j: (b, 0, 0)),
            pl.BlockSpec((1, 1, D), lambda b, i, j: (b, 0, 0)),
            pl.BlockSpec((D, tn), lambda b, i, j: (0, j)),
            pl.BlockSpec((1, tn), lambda b, i, j: (0, j)),
        ],
        out_specs=pl.BlockSpec((1, tm, tn), lambda b, i, j: (b, i, j)),
        out_shape=jax.ShapeDtypeStruct((B, S, N), BF16),
        scratch_shapes=[pltpu.VMEM((tm, D), BF16)],
        compiler_params=_cparams(("parallel", "parallel", "arbitrary")),
        name="inproj",
    )(x, mult, shift, w, bias)


LANES = 128


def _flash_kernel(*refs, nseg, tks, J, R, dv, diff):
    if diff:
        lam_ref, subg_ref = refs[:2]
        refs = refs[2:]
    q_ref = refs[0]
    k_refs = refs[1:1 + nseg]
    v_refs = refs[1 + nseg:1 + 2 * nseg]
    o_ref = refs[1 + 2 * nseg]
    m_scr, acc_scr = refs[2 + 2 * nseg:4 + 2 * nseg]
    s_scrs = refs[4 + 2 * nseg:]
    tq, d = q_ref.shape[-2:]
    rows = R * tq
    qs = [q_ref[0, 0, j].reshape(rows, d) for j in range(J)]
    m_scr[...] = jnp.full(m_scr.shape, NEG, F32)
    acc_scr[...] = jnp.zeros(acc_scr.shape, F32)

    def scores(j, k_ref, c, tk):
        return _dot_nt(qs[j], k_ref[0, 0, j, pl.ds(pl.multiple_of(c * tk, tk), tk), :])

    def accumulate(j, s, v_ref, c, tk):
        vc = v_ref[0, 0, j, pl.ds(pl.multiple_of(c * tk, tk), tk), :]
        slabs = [s[:, t * LANES:(t + 1) * LANES] for t in range(tk // LANES)]
        m_cur = functools.reduce(jnp.maximum, slabs)
        m_prev = m_scr[j]
        m_new = jnp.maximum(m_prev, jnp.max(m_cur, axis=1, keepdims=True))
        alpha = jnp.exp2(m_prev - m_new)
        p = jnp.concatenate([jnp.exp2(sl - m_new) for sl in slabs], axis=1).astype(BF16)
        acc_scr[j] = alpha * acc_scr[j] + jnp.dot(p, vc, preferred_element_type=F32)
        m_scr[j] = m_new

    if nseg == 1:
        k_ref, v_ref, tk = k_refs[0], v_refs[0], tks[0]

        def body(c, carry):
            for j in range(J):
                accumulate(j, scores(j, k_ref, c, tk), v_ref, c, tk)
            return carry

        lax.fori_loop(0, k_ref.shape[3] // tk, body, 0)
    else:
        (kc_ref, k_ref), (vc_ref, v_ref), (tkc, tk) = k_refs, v_refs, tks
        assert kc_ref.shape[3] == tkc
        stages = [(kc_ref, vc_ref, 0, tkc)] + [(k_ref, v_ref, c, tk) for c in range(k_ref.shape[3] // tk)]
        bufs = (s_scrs[:J], s_scrs[J:])
        cur = [scores(j, stages[0][0], 0, tkc) for j in range(J)]
        for idx, (_, vr, c, t) in enumerate(stages):
            nxt = stages[idx + 1] if idx + 1 < len(stages) else None
            for j in range(J):
                if nxt is not None:
                    bufs[idx % 2][j][...] = scores(j, nxt[0], nxt[2], nxt[3])
                accumulate(j, cur[j], vr, c, t)
            if nxt is not None:
                cur = [bufs[idx % 2][j][...] for j in range(J)]
    pieces = []
    for j in range(J):
        acc = acc_scr[j]
        o = acc[:, :dv] / acc[:, dv:dv + 1]
        parts = [o[r * tq:(r + 1) * tq] for r in range(R)]
        if diff:
            y = parts[0] - lam_ref[...] * parts[1]
            y = y * lax.rsqrt(jnp.mean(y * y, axis=-1, keepdims=True) + EPS) * subg_ref[...]
            pieces.append(y)
        else:
            pieces.extend(parts)
    o_ref[0] = jnp.concatenate(pieces, axis=1).astype(o_ref.dtype)


def _flash(q, ks, vs, tq, tk, diff=None):
    B, Hs, J, R, Sq, d = q.shape
    dv = LANES // 2
    assert (J if diff else J * R) * dv == LANES
    tq = min(tq, Sq)
    tks = tuple(min(tk, k.shape[3]) for k in ks)
    in_specs = []
    if diff:
        in_specs += [pl.BlockSpec((1, dv), lambda b, h, i: (0, 0))] * 2
    in_specs.append(pl.BlockSpec((1, 1, J, R, tq, d), lambda b, h, i: (b, h, 0, 0, i, 0)))
    for k in ks:
        in_specs.append(pl.BlockSpec((1, 1, J, k.shape[3], d), lambda b, h, i: (b, h, 0, 0, 0)))
    for v in vs:
        in_specs.append(pl.BlockSpec((1, 1, J, v.shape[3], LANES), lambda b, h, i: (b, h, 0, 0, 0)))
    return pl.pallas_call(
        functools.partial(_flash_kernel, nseg=len(ks), tks=tks, J=J, R=R, dv=dv, diff=bool(diff)),
        grid=(B, Hs, Sq // tq),
        in_specs=in_specs,
        out_specs=pl.BlockSpec((1, tq, LANES), lambda b, h, i: (b, i, h)),
        out_shape=jax.ShapeDtypeStruct((B, Sq, Hs * LANES), BF16),
        scratch_shapes=[pltpu.VMEM((J, R * tq, LANES), F32), pltpu.VMEM((J, R * tq, LANES), F32)]
        + ([pltpu.VMEM((R * tq, tks[1]), F32)] * (2 * J) if len(ks) == 2 else []),
        compiler_params=_cparams(("parallel", "parallel", "arbitrary")),
        name="flash_diff" if diff else "flash",
    )(*(diff or ()), q, *ks, *vs)


NBR_HEADS = 2


def _nbr_kernel(q_ref, k0, k1, k2, k3, v0, v1, v2, v3, kc_ref, vc_ref, bias_ref, o_ref):
    outs = []
    for j in range(NBR_HEADS):
        q = q_ref[0, j]
        k = jnp.concatenate([k0[0, j], k1[0, j], k2[0, j], k3[0, j]], axis=0)
        v = jnp.concatenate([v0[0, j], v1[0, j], v2[0, j], v3[0, j]], axis=0)
        s_loc = _dot_nt(q, k) + bias_ref[0, j]
        s_ctx = _dot_nt(q, kc_ref[0, j])
        m = jnp.maximum(jnp.max(s_loc, axis=1, keepdims=True), jnp.max(s_ctx, axis=1, keepdims=True))
        p_loc = jnp.exp(s_loc - m)
        p_ctx = jnp.exp(s_ctx - m)
        l = jnp.sum(p_loc, axis=1, keepdims=True) + jnp.sum(p_ctx, axis=1, keepdims=True)
        o = (jnp.dot(p_ctx.astype(BF16), vc_ref[0, j], preferred_element_type=F32)
             + jnp.dot(p_loc.astype(BF16), v, preferred_element_type=F32))
        outs.append(o / l)
    o_ref[0] = jnp.concatenate(outs, axis=1).astype(o_ref.dtype)


def _nbr_bias(rpb, rows):
    kh, kw = WIN_H, WIN_W
    qc = jnp.arange(GRID_W)
    kc = jnp.arange(GRID_W)
    c0 = jnp.clip(qc - kw // 2, 0, GRID_W - kw)
    col_ok = (kc[None, :] >= c0[:, None]) & (kc[None, :] < c0[:, None] + kw)
    dj = kc[None, :] - qc[:, None] + WIN_W - 1
    oh_c = ((dj[..., None] == jnp.arange(2 * WIN_W - 1)) & col_ok[..., None]).astype(F32)

    def variant(r_start, k_start):
        r = r_start + jnp.arange(NBR_ROWS)
        kr = k_start + jnp.arange(NBR_KROWS)
        r0 = jnp.clip(r - kh // 2, 0, rows - kh)
        row_ok = (kr[None, :] >= r0[:, None]) & (kr[None, :] < r0[:, None] + kh)
        di = kr[None, :] - r[:, None] + WIN_H - 1
        oh_r = ((di[..., None] == jnp.arange(2 * WIN_H - 1)) & row_ok[..., None]).astype(F32)
        b = jnp.einsum('jka,hab,qcb->hjqkc', oh_r, rpb.astype(F32), oh_c, precision=lax.Precision.HIGHEST)
        valid = row_ok[:, None, :, None] & col_ok[None, :, None, :]
        b = jnp.where(valid[None], b, NEG)
        return b.reshape(rpb.shape[0], NBR_ROWS * GRID_W, NBR_KROWS * GRID_W)

    return jnp.stack([variant(0, 0), variant(NBR_ROWS, NBR_ROWS - kh // 2),
                      variant(rows - NBR_ROWS, rows - NBR_KROWS)])


def _nbr(q, k, v, kc, vc, bias):
    B, H, S, d = q.shape
    L = kc.shape[2]
    tq = NBR_ROWS * GRID_W
    kb = tq // 2
    nb = S // tq
    nh = NBR_HEADS
    assert S % tq == 0 and nb >= 3 and nh * d == LANES

    def kmap(j):
        return lambda h, i, b: (b, h, jnp.clip(2 * i - 1, 0, 2 * nb - 4) + j, 0)

    def bmap(h, i, b):
        return (jnp.where(i == 0, 0, jnp.where(i == nb - 1, 2, 1)), h, 0, 0)

    kv_specs = [pl.BlockSpec((1, nh, kb, d), kmap(j)) for j in range(4)]
    return pl.pallas_call(
        _nbr_kernel,
        grid=(H // nh, nb, B),
        in_specs=[pl.BlockSpec((1, nh, tq, d), lambda h, i, b: (b, h, i, 0))] + kv_specs + kv_specs + [
            pl.BlockSpec((1, nh, L, d), lambda h, i, b: (b, h, 0, 0)),
            pl.BlockSpec((1, nh, L, d), lambda h, i, b: (b, h, 0, 0)),
            pl.BlockSpec((1, nh, tq, NBR_KROWS * GRID_W), bmap),
        ],
        out_specs=pl.BlockSpec((1, tq, LANES), lambda h, i, b: (b, i, h)),
        out_shape=jax.ShapeDtypeStruct((B, S, H * d), BF16),
        compiler_params=_cparams(("parallel", "parallel", "parallel")),
        name="nbr_attn",
    )(q, k, k, k, k, v, v, v, v, kc, vc, bias)


def _gelu(x):
    return 0.5 * x * (1.0 + jnp.tanh(math.sqrt(2.0 / math.pi) * (x + 0.044715 * (x * x * x))))


def _sgu_kernel(u_ref, v_ref, g_ref, w_ref, b_ref, o_ref, *, nchunk):
    grp = lax.broadcasted_iota(I32, (SGU_CHUNK, BRANCH_W), 1) // (BRANCH_W // SGU_GROUPS)
    for c in range(nchunk):
        rows = slice(c * SGU_CHUNK, (c + 1) * SGU_CHUNK)
        u = u_ref[0, rows, :].astype(F32)
        v = _gelu(v_ref[0, rows, :].astype(F32))
        mu = jnp.mean(v, axis=-1, keepdims=True)
        var = jnp.mean(jnp.square(v - mu), axis=-1, keepdims=True)
        vn = ((v - mu) * lax.rsqrt(var + EPS) * g_ref[...]).astype(BF16)
        mixed = b_ref[...]
        for g in range(SGU_GROUPS):
            mg = jnp.dot(w_ref[g], vn, preferred_element_type=F32)
            mixed = mixed + jnp.where(grp == g, mg, 0.0)
        o_ref[0, rows, :] = (_gelu(u) * mixed).astype(o_ref.dtype)


def _sgu(p, norm_g, w_s, b_full, tt):
    B, S, _ = p.shape
    tt = min(tt, S)
    return pl.pallas_call(
        functools.partial(_sgu_kernel, nchunk=tt // SGU_CHUNK),
        grid=(B, S // tt),
        in_specs=[
            pl.BlockSpec((1, tt, BRANCH_W), lambda b, i: (b, i, SEG_U)),
            pl.BlockSpec((1, tt, BRANCH_W), lambda b, i: (b, i, SEG_V)),
            pl.BlockSpec((1, BRANCH_W), lambda b, i: (0, 0)),
            pl.BlockSpec((SGU_GROUPS, SGU_CHUNK, SGU_CHUNK), lambda b, i: (0, 0, 0)),
            pl.BlockSpec((SGU_CHUNK, BRANCH_W), lambda b, i: (0, 0)),
        ],
        out_specs=pl.BlockSpec((1, tt, BRANCH_W), lambda b, i: (b, i, 0)),
        out_shape=jax.ShapeDtypeStruct((B, S, BRANCH_W), BF16),
        compiler_params=_cparams(("parallel", "parallel")),
        name="sgu",
    )(p, p, norm_g, w_s, b_full)


def _merge_kernel(ya, yb, yc, yd, g0, g1, g2, g3, wb_ref, wo_ref, x_ref, gate_ref, mult_ref, shift_ref,
                  wr_ref, xo_ref, h2_ref, lg_ref):
    mix = None
    for n, (y, g) in enumerate(((ya, g0), (yb, g1), (yc, g2), (yd, g3))):
        pr = jnp.dot(y[0], wb_ref[n], preferred_element_type=F32)
        t = g[0].astype(F32) * pr
        mix = t if mix is None else mix + t
    out = jnp.dot(mix.astype(BF16), wo_ref[...], preferred_element_type=F32)
    xn = x_ref[0] + gate_ref[0] * out
    xo_ref[0] = xn
    ms = jnp.mean(xn * xn, axis=-1, keepdims=True)
    h2 = xn * lax.rsqrt(ms + EPS) * mult_ref[0] + shift_ref[0]
    h2_ref[0] = h2.astype(BF16)
    lg_ref[0] = lax.dot_general(wr_ref[...], h2, (((1,), (1,)), ((), ())),
                                precision=lax.Precision.HIGHEST, preferred_element_type=F32)


def _merge(ys, p, w_branch, w_out, x, gate, mult2, shift2, w_router_t, tm):
    B, S, D = x.shape
    E = w_router_t.shape[0]
    tm = min(tm, S)
    y_spec = pl.BlockSpec((1, tm, BRANCH_W), lambda b, i: (b, i, 0))
    g_specs = [pl.BlockSpec((1, tm, D), functools.partial(lambda b, i, n: (b, i, n), n=n)) for n in range(N_BRANCH)]
    vec = pl.BlockSpec((1, 1, D), lambda b, i: (b, 0, 0))
    return pl.pallas_call(
        _merge_kernel,
        grid=(B, S // tm),
        in_specs=[y_spec] * 4 + g_specs + [
            pl.BlockSpec((N_BRANCH, BRANCH_W, D), lambda b, i: (0, 0, 0)),
            pl.BlockSpec((D, D), lambda b, i: (0, 0)),
            pl.BlockSpec((1, tm, D), lambda b, i: (b, i, 0)),
            vec, vec, vec,
            pl.BlockSpec((E, D), lambda b, i: (0, 0)),
        ],
        out_specs=[
            pl.BlockSpec((1, tm, D), lambda b, i: (b, i, 0)),
            pl.BlockSpec((1, tm, D), lambda b, i: (b, i, 0)),
            pl.BlockSpec((1, E, tm), lambda b, i: (b, 0, i)),
        ],
        out_shape=[
            jax.ShapeDtypeStruct((B, S, D), F32),
            jax.ShapeDtypeStruct((B, S, D), BF16),
            jax.ShapeDtypeStruct((B, E, S), F32),
        ],
        compiler_params=_cparams(("parallel", "parallel")),
        name="merge",
    )(*ys, p, p, p, p, w_branch, w_out, x, gate, mult2, shift2, w_router_t)


def _cumsum_excl(x, tri):
    n = x.shape[1]
    outs = []
    carry = jnp.zeros((x.shape[0], 1), F32)
    for c in range(n // 128):
        xc = x[:, c * 128:(c + 1) * 128]
        outs.append(jnp.dot(xc.astype(BF16), tri, preferred_element_type=F32) + carry)
        carry = carry + jnp.sum(xc, axis=1, keepdims=True)
    return jnp.concatenate(outs, axis=1)


def _route_kernel(lg_ref, rank_ref, score_ref, count_ref, *, cap):
    lg = lg_ref[0]
    mx = jnp.max(lg, axis=0, keepdims=True)
    ex = jnp.exp(lg - mx)
    aff = ex / jnp.sum(ex, axis=0, keepdims=True)
    E = lg.shape[0]

    def bisect(i, thr_bits):
        cand = thr_bits | jnp.left_shift(jnp.int32(1), 30 - i)
        cnt = jnp.sum(jnp.where(aff >= pltpu.bitcast(cand, F32), 1, 0), axis=1, keepdims=True)
        return jnp.where(cnt >= cap, cand, thr_bits)

    thr = pltpu.bitcast(lax.fori_loop(0, 31, bisect, jnp.zeros((E, 1), I32)), F32)
    gt = aff > thr
    eq = aff == thr
    need = (cap - jnp.sum(jnp.where(gt, 1, 0), axis=1, keepdims=True)).astype(F32)
    ri = lax.broadcasted_iota(I32, (128, 128), 0)
    ci = lax.broadcasted_iota(I32, (128, 128), 1)
    tri = jnp.where(ri < ci, 1.0, 0.0).astype(BF16)
    eq_before = _cumsum_excl(jnp.where(eq, 1.0, 0.0), tri)
    sel = gt | (eq & (eq_before < need))
    rank = _cumsum_excl(jnp.where(sel, 1.0, 0.0), tri)
    sel = sel & (rank < cap)
    rank_ref[0] = jnp.where(sel, rank.astype(I32), -1)
    score_ref[0] = jnp.where(sel, aff, 0.0)
    n = lg.shape[1]
    chunk_of_token = lax.broadcasted_iota(I32, (n, LANES), 0) // MOE_CHUNK
    member = jnp.where(chunk_of_token == lax.broadcasted_iota(I32, (n, LANES), 1), 1.0, 0.0).astype(BF16)
    count_ref[0] = jnp.dot(jnp.where(sel, 1.0, 0.0).astype(BF16), member, preferred_element_type=F32).astype(I32)


def _route(logits, cap):
    B, E, n = logits.shape
    assert n // MOE_CHUNK <= LANES
    spec = pl.BlockSpec((1, E, n), lambda b: (b, 0, 0))
    rank, score, count = pl.pallas_call(
        functools.partial(_route_kernel, cap=cap),
        grid=(B,),
        in_specs=[spec],
        out_specs=[spec, spec, pl.BlockSpec((1, E, LANES), lambda b: (b, 0, 0))],
        out_shape=[jax.ShapeDtypeStruct((B, E, n), I32), jax.ShapeDtypeStruct((B, E, n), F32),
                   jax.ShapeDtypeStruct((B, E, LANES), I32)],
        compiler_params=_cparams(("parallel",)),
        name="route",
    )(logits)
    return rank, score, count[..., :n // MOE_CHUNK]


SLOT_ALIGN = 16


def _window_start(first, j, W, cap):
    return pl.multiple_of(jnp.minimum((first // SLOT_ALIGN) * SLOT_ALIGN + j * W, cap - W), SLOT_ALIGN)


def _windows_needed(lo, hi, W):
    return (hi - (lo // SLOT_ALIGN) * SLOT_ALIGN + W - 1) // W


def _gather_kernel(cnt_ref, rank_ref, h_ref, o_ref, *, nc, W, unroll, flag_off):
    b = pl.program_id(0)
    e = pl.program_id(1)
    be = b * pl.num_programs(1) + e
    base = be * (nc + 1)
    cap = o_ref.shape[2]
    o_ref[...] = jnp.zeros(o_ref.shape, o_ref.dtype)
    T = MOE_CHUNK

    def window(c, j):
        lo = cnt_ref[base + c]
        r = rank_ref[0, 0, pl.ds(c, 1), :]
        hc = h_ref[0, pl.ds(pl.multiple_of(c * T, T), T), :]
        st = _window_start(lo, j, W, cap)
        slot = lax.broadcasted_iota(I32, (W, T), 0) + st
        hit = (slot == r) & (slot >= (lo // SLOT_ALIGN) * SLOT_ALIGN + j * W)
        got = jnp.dot(jnp.where(hit, 1.0, 0.0).astype(BF16), hc, preferred_element_type=F32)
        o_ref[0, 0, pl.ds(st, W), :] = o_ref[0, 0, pl.ds(st, W), :] + got.astype(o_ref.dtype)

    def group(g, carry):
        for u in range(unroll):
            window(g * unroll + u, 0)
        return carry

    lax.fori_loop(0, nc // unroll, group, 0)

    def tail(c, carry):
        def more(j, carry2):
            window(c, j)
            return carry2

        lax.fori_loop(1, _windows_needed(cnt_ref[base + c], cnt_ref[base + c + 1], W), more, 0)
        return carry

    @pl.when(cnt_ref[flag_off + be] > 0)
    def _():
        lax.fori_loop(0, nc, tail, 0)


def _moe_window(cap):
    return min(128, cap)


def _moe_tables(per_chunk, cap, tt, eg):
    B, E, nc = per_chunk.shape
    n = nc * MOE_CHUNK
    W = _moe_window(cap)
    cnt = jnp.concatenate([jnp.zeros((B, E, 1), I32), jnp.cumsum(per_chunk, axis=-1, dtype=I32)], axis=-1)
    extra = _windows_needed(cnt[..., :-1], cnt[..., 1:], W) > 1
    g_flag = jnp.any(extra, axis=-1)
    nsub = min(tt, n) // MOE_CHUNK
    s_flag = jnp.any(extra.reshape(B, E // eg, eg, nc // nsub, nsub), axis=(2, 4))
    tbl = jnp.concatenate([cnt.reshape(-1), g_flag.reshape(-1).astype(I32), s_flag.reshape(-1).astype(I32)])
    return tbl, cnt.size, cnt.size + g_flag.size


def _gather(tbl, flag_off, rank, h, cap):
    B, E, n = rank.shape
    D = h.shape[-1]
    nc = n // MOE_CHUNK
    return pl.pallas_call(
        functools.partial(_gather_kernel, nc=nc, W=_moe_window(cap), unroll=min(4, nc), flag_off=flag_off),
        grid_spec=pltpu.PrefetchScalarGridSpec(
            num_scalar_prefetch=1,
            grid=(B, E),
            in_specs=[
                pl.BlockSpec((1, 1, nc, MOE_CHUNK), lambda b, e, tbl: (b, e, 0, 0)),
                pl.BlockSpec((1, n, D), lambda b, e, tbl: (b, 0, 0)),
            ],
            out_specs=pl.BlockSpec((1, 1, cap, D), lambda b, e, tbl: (b, e, 0, 0)),
        ),
        out_shape=jax.ShapeDtypeStruct((B, E, cap, D), BF16),
        compiler_params=_cparams(("parallel", "arbitrary")),
        name="moe_gather",
    )(tbl, rank.reshape(B, E, nc, MOE_CHUNK), h)


def _ffn_kernel(*refs, nsets):
    x_refs = refs[:nsets]
    wg_ref, wu_ref, wd_ref = refs[nsets:nsets + 3]
    o_refs = refs[nsets + 3:2 * nsets + 3]
    acc_scrs = refs[2 * nsets + 3:]
    f = pl.program_id(1)

    @pl.when(f == 0)
    def _():
        for acc_scr in acc_scrs:
            acc_scr[...] = jnp.zeros(acc_scr.shape, F32)

    wg = wg_ref[0, 0].astype(BF16)
    wu = wu_ref[0, 0].astype(BF16)
    wd = wd_ref[0, 0].astype(BF16)
    for x_ref, acc_scr in zip(x_refs, acc_scrs):
        for b in range(x_ref.shape[0]):
            x = x_ref[b, 0]
            g = jnp.dot(x, wg, preferred_element_type=F32)
            u = jnp.dot(x, wu, preferred_element_type=F32)
            hid = (g * (0.5 + 0.5 * jnp.tanh(0.5 * g)) * u).astype(BF16)
            acc_scr[b] = acc_scr[b] + jnp.dot(hid, wd, preferred_element_type=F32)

    @pl.when(f == pl.num_programs(1) - 1)
    def _():
        for o_ref, acc_scr in zip(o_refs, acc_scrs):
            o_ref[:, 0] = acc_scr[...].astype(o_ref.dtype)


def _ffn(xins, w_gate, w_up, w_down, layer, tf):
    E, D = xins[0].shape[1], xins[0].shape[3]
    Fh = w_gate.shape[-1]
    x_specs = [pl.BlockSpec((x.shape[0], 1, x.shape[2], D), lambda e, f: (0, e, 0, 0)) for x in xins]
    return pl.pallas_call(
        functools.partial(_ffn_kernel, nsets=len(xins)),
        grid=(E, Fh // tf),
        in_specs=x_specs + [
            pl.BlockSpec((1, 1, D, tf), lambda e, f: (layer, e, 0, f)),
            pl.BlockSpec((1, 1, D, tf), lambda e, f: (layer, e, 0, f)),
            pl.BlockSpec((1, 1, tf, D), lambda e, f: (layer, e, f, 0)),
        ],
        out_specs=x_specs,
        out_shape=[jax.ShapeDtypeStruct(x.shape, BF16) for x in xins],
        scratch_shapes=[pltpu.VMEM((x.shape[0], x.shape[2], D), F32) for x in xins],
        compiler_params=_cparams(("parallel", "arbitrary")),
        name="moe_ffn",
    )(*xins, w_gate, w_up, w_down)


SCATTER_EG = 4
SCATTER_TT = 1024


def _scatter_kernel(cnt_ref, rank_ref, score_ref, y_ref, x_ref, g2_ref, fg_ref, o_ref, acc_scr, *,
                    nc, W, final, flag_off):
    b = pl.program_id(0)
    i = pl.program_id(1)
    g = pl.program_id(2)
    G = pl.num_programs(2)
    eg = y_ref.shape[1]
    T = MOE_CHUNK
    nsub = acc_scr.shape[0] // T
    cap = y_ref.shape[2]

    @pl.when(g == 0)
    def _():
        acc_scr[...] = jnp.zeros(acc_scr.shape, F32)

    def window(c, ee, j):
        rows = slice(c * T, (c + 1) * T)
        lo = cnt_ref[((b * G + g) * eg + ee) * (nc + 1) + i * nsub + c]
        rc = rank_ref[0, 0, rows, ee:ee + 1]
        sc = score_ref[0, 0, rows, ee:ee + 1]
        st = _window_start(lo, j, W, cap)
        slot = lax.broadcasted_iota(I32, (T, W), 1) + st
        hit = (slot == rc) & (slot >= (lo // SLOT_ALIGN) * SLOT_ALIGN + j * W)
        got = jnp.dot(jnp.where(hit, 1.0, 0.0).astype(BF16), y_ref[0, ee, pl.ds(st, W), :],
                      preferred_element_type=F32)
        return sc * got

    for c in range(nsub):
        rows = slice(c * T, (c + 1) * T)
        acc_scr[rows, :] = acc_scr[rows, :] + functools.reduce(jnp.add, [window(c, ee, 0) for ee in range(eg)])

    @pl.when(cnt_ref[flag_off + (b * G + g) * pl.num_programs(1) + i] > 0)
    def _():
        for c in range(nsub):
            for ee in range(eg):
                def more(j, carry, c=c, ee=ee):
                    rows = slice(c * T, (c + 1) * T)
                    acc_scr[rows, :] = acc_scr[rows, :] + window(c, ee, j)
                    return carry

                base = ((b * G + g) * eg + ee) * (nc + 1) + i * nsub + c
                lax.fori_loop(1, _windows_needed(cnt_ref[base], cnt_ref[base + 1], W), more, 0)

    @pl.when(g == G - 1)
    def _():
        xn = x_ref[0] + g2_ref[0] * acc_scr[...]
        if final:
            ms = jnp.mean(xn * xn, axis=-1, keepdims=True)
            xn = xn * lax.rsqrt(ms + EPS) * fg_ref[...]
        o_ref[0] = xn


def _scatter(tbl, flag_off, rank, score, y, x, g2, final_g, final):
    B, E, n = rank.shape
    cap, D = y.shape[2], y.shape[3]
    tt = min(SCATTER_TT, n)
    eg = SCATTER_EG
    nc = n // MOE_CHUNK

    def token_major(t):
        return t.reshape(B, E // eg, eg, n).transpose(0, 1, 3, 2)

    return pl.pallas_call(
        functools.partial(_scatter_kernel, nc=nc, W=_moe_window(cap), final=final, flag_off=flag_off),
        grid_spec=pltpu.PrefetchScalarGridSpec(
            num_scalar_prefetch=1,
            grid=(B, n // tt, E // eg),
            in_specs=[
                pl.BlockSpec((1, 1, tt, eg), lambda b, i, g, tbl: (b, g, i, 0)),
                pl.BlockSpec((1, 1, tt, eg), lambda b, i, g, tbl: (b, g, i, 0)),
                pl.BlockSpec((1, eg, cap, D), lambda b, i, g, tbl: (b, g, 0, 0)),
                pl.BlockSpec((1, tt, D), lambda b, i, g, tbl: (b, i, 0)),
                pl.BlockSpec((1, 1, D), lambda b, i, g, tbl: (b, 0, 0)),
                pl.BlockSpec((1, D), lambda b, i, g, tbl: (0, 0)),
            ],
            out_specs=pl.BlockSpec((1, tt, D), lambda b, i, g, tbl: (b, i, 0)),
            scratch_shapes=[pltpu.VMEM((tt, D), F32)],
        ),
        out_shape=jax.ShapeDtypeStruct((B, n, D), F32),
        compiler_params=_cparams(("parallel", "parallel", "arbitrary")),
        name="moe_scatter",
    )(tbl, token_major(rank), token_major(score), y, x, g2, final_g)


def _moe_dispatch(h2, logits):
    n = h2.shape[1]
    cap = CAPACITY_FACTOR * n // N_EXPERTS
    rank, score, per_chunk = _route(logits, cap)
    tbl, g_off, s_off = _moe_tables(per_chunk, cap, SCATTER_TT, SCATTER_EG)
    return (tbl, s_off, rank, score), _gather(tbl, g_off, rank, h2, cap)


def _moe_combine(routing, y, x, g2, final_g, final):
    tbl, s_off, rank, score = routing
    return _scatter(tbl, s_off, rank, score, y, x, g2, final_g, final)


LOG2E = math.log2(math.e)
HEADS_PER_BLK = COL_BLK // HEAD_DIM


def _prep_kernel(*refs, rope, qa_scale):
    qa_r, qb_r, qd_r, ka_r, va_r, kb_r, vb_r, kvd_r = refs[:8]
    refs = refs[8:]
    if rope:
        cb_r, sb_r, cd_r, sd_r = refs[:4]
        refs = refs[4:]
    qg_r, kg_r, gm_r = refs[:3]
    qa_o, qb_o, qd_o, ka_o, va_o, kb_o, vb_o, kd_o, vd_o = refs[3:]
    ts = qa_r.shape[1]
    kvw = GQA_KV_HEADS * HEAD_DIM

    def partner(x, half):
        n = x.shape[1]
        lane = lax.broadcasted_iota(I32, x.shape, 1)
        return jnp.where((lane & half) == 0, pltpu.roll(x, n - half, 1), pltpu.roll(x, half, 1))

    def rot(x, cos, sin, half):
        return x * cos + partner(x, half) * sin

    def group_norm(x, gm, gain):
        ms = jnp.dot(x * x, gm, precision=lax.Precision.HIGHEST, preferred_element_type=F32)
        return x * lax.rsqrt(ms + EPS) * gain

    def head(x, h):
        return x[:, h * HEAD_DIM:(h + 1) * HEAD_DIM]

    lane64 = lax.broadcasted_iota(I32, (ts, HEAD_DIM), 1)
    ones_tail = jnp.where(lane64 == 0, 1.0, 0.0).astype(BF16)

    qa = qa_r[0].astype(F32) * qa_scale
    for h in range(HEADS_PER_BLK):
        qa_o[0, h] = head(qa, h).astype(BF16)
        ka_o[0, h] = head(ka_r[0], h)
        va_o[0, h] = head(va_r[0], h)

    qb = qb_r[0].astype(F32)
    kb = kb_r[0].astype(F32)
    if rope:
        qb = rot(qb, cb_r[...], sb_r[...], DIFF_QK_DIM // 2)
        kb = rot(kb, cb_r[...], sb_r[...], DIFF_QK_DIM // 2)
    qb = qb * (DIFF_QK_DIM ** -0.5 * LOG2E)
    for h in range(HEADS_PER_BLK):
        xh = head(qb, h)
        qb_o[0, h // 2, h % 2, 0] = jnp.where(lane64 < DIFF_QK_DIM, xh, 0.0).astype(BF16)
        qb_o[0, h // 2, h % 2, 1] = jnp.where(lane64 >= DIFF_QK_DIM, xh, 0.0).astype(BF16)
        kb_o[0, h // 2, h % 2] = head(kb, h).astype(BF16)
        vb_o[0, h // 2, h % 2] = jnp.concatenate([head(vb_r[0], h), ones_tail], axis=1)

    qd = group_norm(qd_r[0].astype(F32), gm_r[...], qg_r[...])
    kd = group_norm(kvd_r[0, :, :kvw].astype(F32), gm_r[:kvw, :kvw], kg_r[...])
    if rope:
        qd = rot(qd, cd_r[...], sd_r[...], HEAD_DIM // 2)
        kd = rot(kd, cd_r[:, :kvw], sd_r[:, :kvw], HEAD_DIM // 2)
    qd = qd * (HEAD_DIM ** -0.5 * LOG2E)
    grp = GQA_Q_HEADS // GQA_KV_HEADS
    for h in range(GQA_Q_HEADS):
        qd_o[0, h // grp, 0, h % grp] = head(qd, h).astype(BF16)
    vd = kvd_r[0, :, kvw:]
    for h in range(GQA_KV_HEADS):
        kd_o[0, h, 0] = head(kd, h).astype(BF16)
        vd_o[0, h, 0] = jnp.concatenate([head(vd, h), ones_tail], axis=1)


def _prep(p, tables, qg, kg, gm, qa_scale, ts):
    B, S, _ = p.shape
    ts = min(ts, S)

    def seg(blk):
        return pl.BlockSpec((1, ts, COL_BLK), lambda b, i: (b, i, blk))

    in_specs = [seg(s) for s in (SEG_QA, SEG_QB, SEG_QD, SEG_KA, SEG_VA, SEG_KB, SEG_VB, SEG_VB + 1)]
    args = [p] * 8
    if tables is not None:
        in_specs += [pl.BlockSpec((ts, COL_BLK), lambda b, i: (i, 0))] * 4
        args += list(tables)
    in_specs += [pl.BlockSpec(a.shape, lambda b, i: (0, 0)) for a in (qg, kg, gm)]
    args += [qg, kg, gm]

    def out(lead, width):
        shape = (B,) + lead + (S, width)
        nl = len(lead)
        spec = pl.BlockSpec((1,) + lead + (ts, width), lambda b, i: (b,) + (0,) * nl + (i, 0))
        return jax.ShapeDtypeStruct(shape, BF16), spec

    grp = GQA_Q_HEADS // GQA_KV_HEADS
    outs = [out((NA_HEADS,), HEAD_DIM), out((DIFF_HEADS // 2, 2, 2), HEAD_DIM),
            out((GQA_KV_HEADS, 1, grp), HEAD_DIM), out((NA_HEADS,), HEAD_DIM), out((NA_HEADS,), HEAD_DIM),
            out((DIFF_HEADS // 2, 2), HEAD_DIM), out((DIFF_HEADS // 2, 2), LANES),
            out((GQA_KV_HEADS, 1), HEAD_DIM), out((GQA_KV_HEADS, 1), LANES)]
    return pl.pallas_call(
        functools.partial(_prep_kernel, rope=tables is not None, qa_scale=qa_scale),
        grid=(B, S // ts),
        in_specs=in_specs,
        out_specs=[o[1] for o in outs],
        out_shape=[o[0] for o in outs],
        compiler_params=_cparams(("parallel", "parallel")),
        name="attn_prep",
    )(*args)


def _rope_lane_tables(n, dim):
    t = jnp.arange(n)
    row = (t // GRID_W).astype(F32)
    col = (t % GRID_W).astype(F32)
    n_pairs = dim // 4
    inv = ROPE_THETA ** (-jnp.arange(n_pairs, dtype=F32) / n_pairs)
    ang = jnp.concatenate([row[:, None] * inv, col[:, None] * inv], axis=-1)
    cos, sin = jnp.cos(ang), jnp.sin(ang)
    reps = COL_BLK // dim
    return jnp.tile(jnp.concatenate([cos, cos], -1), (1, reps)), jnp.tile(jnp.concatenate([-sin, sin], -1), (1, reps))


def _deinterleave(w, width):
    lead = w.shape[:-1]
    n = w.shape[-1]
    return jnp.swapaxes(w.reshape(lead + (n // width, width // 2, 2)), -1, -2).reshape(lead + (n,))


def _with_ones(v):
    pad = [(0, 0)] * (v.ndim - 1) + [(0, LANES - v.shape[-1] - 1)]
    return jnp.pad(jnp.concatenate([v, jnp.ones(v.shape[:-1] + (1,), v.dtype)], axis=-1), pad)


def _mixer_branches(p, ops, ops_ctx, bias_a, lam, lam_init, sub_g, sgu_g, sgu_w, sgu_bf):
    qa, qb, qd, ka, va, kb, vb, kd, vd = ops
    b, _, s, _ = qa.shape
    if ops_ctx is None:
        y_a = _flash(qa.reshape(b, NA_HEADS // 2, 2, 1, s, HEAD_DIM), [ka.reshape(b, NA_HEADS // 2, 2, s, HEAD_DIM)],
                     [_with_ones(va).reshape(b, NA_HEADS // 2, 2, s, LANES)], tq=256, tk=256)
        kbs, vbs, kds, vds = [kb], [vb], [kd], [vd]
    else:
        _, _, _, kac, vac, kbc, vbc, kdc, vdc = ops_ctx
        y_a = _nbr(qa, ka, va, kac, vac, bias_a)
        kbs, vbs, kds, vds = [kbc, kb], [vbc, vb], [kdc, kd], [vdc, vd]
    lam_v = jnp.full((1, DIFF_V_DIM), lam, F32)
    gain_v = (sub_g.astype(F32) * (1.0 - lam_init))[None, :]
    y_b = _flash(qb, kbs, vbs, tq=256, tk=2048, diff=(lam_v, gain_v))
    y_d = _flash(qd, kds, vds, tq=512, tk=2048)
    y_c = _sgu(p, sgu_g, sgu_w, sgu_bf, tt=1024)
    return [y_a, y_b, y_c, y_d]


def kernel(x, c, ctx, c_ctx, w_mod, b_mod, norm1_g, norm2_g, w_in, q_gain, k_gain, na_rpb, lambda_q1, lambda_k1, lambda_q2, lambda_k2, diff_sub_g, sgu_norm_g, sgu_w, sgu_b, gate_b, w_branch, w_out, w_router, w_e_gate, w_e_up, w_e_down, final_g):
    B, S, D = x.shape
    depth = w_mod.shape[0]
    rows = S // GRID_W
    tables = _rope_lane_tables(S, DIFF_QK_DIM) + _rope_lane_tables(S, HEAD_DIM)
    hp = lax.Precision.HIGHEST
    fg = final_g.astype(F32)[None, :]
    grp_id = jnp.arange(COL_BLK) // HEAD_DIM
    gm = (grp_id[:, None] == grp_id[None, :]).astype(F32) / HEAD_DIM
    for l in range(depth):
        last = l == depth - 1
        mod = jnp.dot(jax.nn.silu(c), w_mod[l], precision=hp) + b_mod[l]
        sh1, sc1, g1, sh2, sc2, g2 = (t[:, None, :] for t in jnp.split(mod, 6, axis=-1))
        mod_c = jnp.dot(jax.nn.silu(c_ctx), w_mod[l], precision=hp) + b_mod[l]
        csh1, csc1, cg1, csh2, csc2, cg2 = (jnp.broadcast_to(t[None, None, :], (B, 1, D))
                                            for t in jnp.split(mod_c, 6, axis=-1))
        lam_init = 0.8 - 0.6 * math.exp(-0.3 * l)
        lam = (jnp.exp(jnp.sum(lambda_q1[l].astype(F32) * lambda_k1[l].astype(F32)))
               - jnp.exp(jnp.sum(lambda_q2[l].astype(F32) * lambda_k2[l].astype(F32))) + lam_init)

        wl = w_in[l]
        q0, kv0 = 0, wl.shape[-1] - 5 * COL_BLK

        def cols(start, blk, n=1):
            return wl[:, start + blk * COL_BLK:start + (blk + n) * COL_BLK]

        kdw = GQA_KV_HEADS * HEAD_DIM
        w_l = jnp.concatenate([
            wl[:, 5 * COL_BLK:kv0],
            cols(q0, 0), _deinterleave(cols(q0, 1), DIFF_QK_DIM), _deinterleave(cols(q0, 2), HEAD_DIM),
            cols(q0, 3, 2),
            cols(kv0, 0, 2), _deinterleave(cols(kv0, 2), DIFF_QK_DIM), cols(kv0, 3),
            _deinterleave(wl[:, kv0 + 4 * COL_BLK:kv0 + 4 * COL_BLK + kdw], HEAD_DIM),
            wl[:, kv0 + 4 * COL_BLK + kdw:],
        ], axis=1).astype(BF16)
        qg = jnp.tile(_deinterleave(q_gain[l].astype(F32), HEAD_DIM), GQA_Q_HEADS)[None, :]
        kg = jnp.tile(_deinterleave(k_gain[l].astype(F32), HEAD_DIM), GQA_KV_HEADS)[None, :]
        bias_in = jnp.concatenate([gate_b[l].astype(F32), jnp.zeros((w_l.shape[1] - GATE_W,), F32)])[None, :]
        n1 = norm1_g[l].astype(F32)[None, None, :]
        n2 = norm2_g[l].astype(F32)[None, None, :]
        wb = w_branch[l].astype(BF16)
        wo = w_out[l].astype(BF16)
        wr_t = w_router[l].astype(F32).T
        sgu_g = sgu_norm_g[l].astype(F32)[None, :]
        sgu_wb = sgu_w[l].astype(BF16)
        sgu_bf = jnp.repeat(sgu_b[l].astype(F32).T, BRANCH_W // SGU_GROUPS, axis=1)
        bias_a = _nbr_bias(na_rpb[l], rows)

        tn = w_l.shape[1] // 4
        p = _inproj(x, n1 * (1.0 + sc1), sh1, w_l, bias_in, GATE_W, tm=2048, tn=tn)
        ops = _prep(p, tables, qg, kg, gm, HEAD_DIM ** -0.5, ts=512)
        pc = _inproj(ctx, n1 * (1.0 + csc1), csh1, w_l, bias_in, GATE_W, tm=256, tn=tn)
        ops_c = _prep(pc, None, qg, kg, gm, HEAD_DIM ** -0.5 * LOG2E, ts=256)
        if not last:
            ys_c = _mixer_branches(pc, ops_c, None, None, lam, lam_init, diff_sub_g[l], sgu_g, sgu_wb, sgu_bf)
            ctx, hc2, lg_c = _merge(ys_c, pc, wb, wo, ctx, cg1, n2 * (1.0 + csc2), csh2, wr_t, tm=256)
        ys = _mixer_branches(p, ops, ops_c, bias_a, lam, lam_init, diff_sub_g[l], sgu_g, sgu_wb, sgu_bf)
        x, h2, lg = _merge(ys, p, wb, wo, x, g1, n2 * (1.0 + sc2), sh2, wr_t, tm=512)

        routings, xins = zip(*([_moe_dispatch(h2, lg)] + ([] if last else [_moe_dispatch(hc2, lg_c)])))
        ys_e = _ffn(list(xins), w_e_gate, w_e_up, w_e_down, l, tf=512)
        x = _moe_combine(routings[0], ys_e[0], x, g2, fg, final=last)
        if not last:
            ctx = _moe_combine(routings[1], ys_e[1], ctx, cg2, fg, final=False)
    return x
```

```python
import functools
import math

import jax
import jax.numpy as jnp
from jax import lax
from jax.experimental import pallas as pl
from jax.experimental.pallas import tpu as pltpu

F32 = jnp.float32
BF16 = jnp.bfloat16
I32 = jnp.int32

GRID_W = 64
HEAD_DIM = 64
N_BRANCH = 4
BRANCH_W = 256
NA_HEADS = 4
WIN_H = 8
WIN_W = 16
DIFF_HEADS = 4
DIFF_QK_DIM = 32
DIFF_V_DIM = 64
SGU_GROUPS = 4
SGU_CHUNK = 128
GQA_Q_HEADS = 4
GQA_KV_HEADS = 2
ROPE_THETA = 10000.0
N_EXPERTS = 16
CAPACITY_FACTOR = 2
EPS = 1e-6
NEG = -1e30

GATE_W = N_BRANCH * 1024
COL_BLK = 256
SEG_QA, SEG_QB, SEG_QD, SEG_U, SEG_V, SEG_KA, SEG_VA, SEG_KB, SEG_VB = range(16, 25)
COL_KD = GATE_W + 9 * COL_BLK
KV_COL0 = GATE_W + 5 * COL_BLK

VMEM_LIMIT = 56 * 1024 * 1024

NBR_ROWS = 8
NBR_KROWS = 16
MOE_CHUNK = 256


def _cparams(sem):
    return pltpu.CompilerParams(dimension_semantics=sem, vmem_limit_bytes=VMEM_LIMIT)


def _dot_nt(a, b):
    return lax.dot_general(a, b, (((1,), (1,)), ((), ())), preferred_element_type=F32)


def _inproj_kernel(x_ref, mult_ref, shift_ref, w_ref, bias_ref, o_ref, h_scr, *, n_gate_cols):
    j = pl.program_id(2)
    tn = o_ref.shape[-1]

    @pl.when(j == 0)
    def _():
        x = x_ref[0]
        ms = jnp.mean(x * x, axis=-1, keepdims=True)
        h = x * lax.rsqrt(ms + EPS) * mult_ref[0] + shift_ref[0]
        h_scr[...] = h.astype(BF16)

    acc = jnp.dot(h_scr[...], w_ref[...], preferred_element_type=F32) + bias_ref[...]

    @pl.when(j * tn < n_gate_cols)
    def _():
        col = j * tn + lax.broadcasted_iota(I32, acc.shape, 1)
        o_ref[0] = jnp.where(col < n_gate_cols, 0.5 + 0.5 * jnp.tanh(0.5 * acc), acc).astype(o_ref.dtype)

    @pl.when(j * tn >= n_gate_cols)
    def _():
        o_ref[0] = acc.astype(o_ref.dtype)


def _inproj(x, mult, shift, w, bias, n_gate_cols, tm, tn):
    B, S, D = x.shape
    N = w.shape[1]
    tm = min(tm, S)
    return pl.pallas_call(
        functools.partial(_inproj_kernel, n_gate_cols=n_gate_cols),
        grid=(B, S // tm, N // tn),
        in_specs=[
            pl.BlockSpec((1, tm, D), lambda b, i, j: (b, i, 0)),
            pl.BlockSpec((1, 1, D), lambda b, i, j: (b, 0, 0)),
            pl.BlockSpec((1, 1, D), lambda b, i, j: (b, 0, 0)),
            pl.BlockSpec((D, tn), lambda b, i, j: (0, j)),
            pl.BlockSpec((1, tn), lambda b, i, j: (0, j)),
        ],
        out_specs=pl.BlockSpec((1, tm, tn), lambda b, i, j: (b, i, j)),
        out_shape=jax.ShapeDtypeStruct((B, S, N), BF16),
        scratch_shapes=[pltpu.VMEM((tm, D), BF16)],
        compiler_params=_cparams(("parallel", "parallel", "arbitrary")),
        name="inproj",
    )(x, mult, shift, w, bias)


LANES = 128


def _flash_kernel(*refs, nseg, tks, J, R, dv, diff):
    if diff:
        lam_ref, subg_ref = refs[:2]
        refs = refs[2:]
    q_ref = refs[0]
    k_refs = refs[1:1 + nseg]
    v_refs = refs[1 + nseg:1 + 2 * nseg]
    o_ref = refs[1 + 2 * nseg]
    m_scr, acc_scr = refs[2 + 2 * nseg:4 + 2 * nseg]
    s_scrs = refs[4 + 2 * nseg:]
    tq, d = q_ref.shape[-2:]
    rows = R * tq
    qs = [q_ref[0, 0, j].reshape(rows, d) for j in range(J)]
    m_scr[...] = jnp.full(m_scr.shape, NEG, F32)
    acc_scr[...] = jnp.zeros(acc_scr.shape, F32)

    def scores(j, k_ref, c, tk):
        return _dot_nt(qs[j], k_ref[0, 0, j, pl.ds(pl.multiple_of(c * tk, tk), tk), :])

    def accumulate(j, s, v_ref, c, tk):
        vc = v_ref[0, 0, j, pl.ds(pl.multiple_of(c * tk, tk), tk), :]
        slabs = [s[:, t * LANES:(t + 1) * LANES] for t in range(tk // LANES)]
        m_cur = functools.reduce(jnp.maximum, slabs)
        m_prev = m_scr[j]
        m_new = jnp.maximum(m_prev, jnp.max(m_cur, axis=1, keepdims=True))
        alpha = jnp.exp2(m_prev - m_new)
        p = jnp.concatenate([jnp.exp2(sl - m_new) for sl in slabs], axis=1).astype(BF16)
        acc_scr[j] = alpha * acc_scr[j] + jnp.dot(p, vc, preferred_element_type=F32)
        m_scr[j] = m_new

    if nseg == 1:
        k_ref, v_ref, tk = k_refs[0], v_refs[0], tks[0]

        def body(c, carry):
            for j in range(J):
                accumulate(j, scores(j, k_ref, c, tk), v_ref, c, tk)
            return carry

        lax.fori_loop(0, k_ref.shape[3] // tk, body, 0)
    else:
        (kc_ref, k_ref), (vc_ref, v_ref), (tkc, tk) = k_refs, v_refs, tks
        assert kc_ref.shape[3] == tkc
        stages = [(kc_ref, vc_ref, 0, tkc)] + [(k_ref, v_ref, c, tk) for c in range(k_ref.shape[3] // tk)]
        bufs = (s_scrs[:J], s_scrs[J:])
        cur = [scores(j, stages[0][0], 0, tkc) for j in range(J)]
        for idx, (_, vr, c, t) in enumerate(stages):
            nxt = stages[idx + 1] if idx + 1 < len(stages) else None
            for j in range(J):
                if nxt is not None:
                    bufs[idx % 2][j][...] = scores(j, nxt[0], nxt[2], nxt[3])
                accumulate(j, cur[j], vr, c, t)
            if nxt is not None:
                cur = [bufs[idx % 2][j][...] for j in range(J)]
    pieces = []
    for j in range(J):
        acc = acc_scr[j]
        o = acc[:, :dv] / acc[:, dv:dv + 1]
        parts = [o[r * tq:(r + 1) * tq] for r in range(R)]
        if diff:
            y = parts[0] - lam_ref[...] * parts[1]
            y = y * lax.rsqrt(jnp.mean(y * y, axis=-1, keepdims=True) + EPS) * subg_ref[...]
            pieces.append(y)
        else:
            pieces.extend(parts)
    o_ref[0] = jnp.concatenate(pieces, axis=1).astype(o_ref.dtype)


def _flash(q, ks, vs, tq, tk, diff=None):
    B, Hs, J, R, Sq, d = q.shape
    dv = LANES // 2
    assert (J if diff else J * R) * dv == LANES
    tq = min(tq, Sq)
    tks = tuple(min(tk, k.shape[3]) for k in ks)
    in_specs = []
    if diff:
        in_specs += [pl.BlockSpec((1, dv), lambda b, h, i: (0, 0))] * 2
    in_specs.append(pl.BlockSpec((1, 1, J, R, tq, d), lambda b, h, i: (b, h, 0, 0, i, 0)))
    for k in ks:
        in_specs.append(pl.BlockSpec((1, 1, J, k.shape[3], d), lambda b, h, i: (b, h, 0, 0, 0)))
    for v in vs:
        in_specs.append(pl.BlockSpec((1, 1, J, v.shape[3], LANES), lambda b, h, i: (b, h, 0, 0, 0)))
    return pl.pallas_call(
        functools.partial(_flash_kernel, nseg=len(ks), tks=tks, J=J, R=R, dv=dv, diff=bool(diff)),
        grid=(B, Hs, Sq // tq),
        in_specs=in_specs,
        out_specs=pl.BlockSpec((1, tq, LANES), lambda b, h, i: (b, i, h)),
        out_shape=jax.ShapeDtypeStruct((B, Sq, Hs * LANES), BF16),
        scratch_shapes=[pltpu.VMEM((J, R * tq, LANES), F32), pltpu.VMEM((J, R * tq, LANES), F32)]
        + ([pltpu.VMEM((R * tq, tks[1]), F32)] * (2 * J) if len(ks) == 2 else []),
        compiler_params=_cparams(("parallel", "parallel", "arbitrary")),
        name="flash_diff" if diff else "flash",
    )(*(diff or ()), q, *ks, *vs)


NBR_HEADS = 2


def _nbr_kernel(q_ref, k0, k1, k2, k3, v0, v1, v2, v3, kc_ref, vc_ref, bias_ref, o_ref):
    outs = []
    for j in range(NBR_HEADS):
        q = q_ref[0, j]
        k = jnp.concatenate([k0[0, j], k1[0, j], k2[0, j], k3[0, j]], axis=0)
        v = jnp.concatenate([v0[0, j], v1[0, j], v2[0, j], v3[0, j]], axis=0)
        s_loc = _dot_nt(q, k) + bias_ref[0, j]
        s_ctx = _dot_nt(q, kc_ref[0, j])
        m = jnp.maximum(jnp.max(s_loc, axis=1, keepdims=True), jnp.max(s_ctx, axis=1, keepdims=True))
        p_loc = jnp.exp(s_loc - m)
        p_ctx = jnp.exp(s_ctx - m)
        l = jnp.sum(p_loc, axis=1, keepdims=True) + jnp.sum(p_ctx, axis=1, keepdims=True)
        o = (jnp.dot(p_ctx.astype(BF16), vc_ref[0, j], preferred_element_type=F32)
             + jnp.dot(p_loc.astype(BF16), v, preferred_element_type=F32))
        outs.append(o / l)
    o_ref[0] = jnp.concatenate(outs, axis=1).astype(o_ref.dtype)


def _nbr_bias(rpb, rows):
    kh, kw = WIN_H, WIN_W
    qc = jnp.arange(GRID_W)
    kc = jnp.arange(GRID_W)
    c0 = jnp.clip(qc - kw // 2, 0, GRID_W - kw)
    col_ok = (kc[None, :] >= c0[:, None]) & (kc[None, :] < c0[:, None] + kw)
    dj = kc[None, :] - qc[:, None] + WIN_W - 1
    oh_c = ((dj[..., None] == jnp.arange(2 * WIN_W - 1)) & col_ok[..., None]).astype(F32)

    def variant(r_start, k_start):
        r = r_start + jnp.arange(NBR_ROWS)
        kr = k_start + jnp.arange(NBR_KROWS)
        r0 = jnp.clip(r - kh // 2, 0, rows - kh)
        row_ok = (kr[None, :] >= r0[:, None]) & (kr[None, :] < r0[:, None] + kh)
        di = kr[None, :] - r[:, None] + WIN_H - 1
        oh_r = ((di[..., None] == jnp.arange(2 * WIN_H - 1)) & row_ok[..., None]).astype(F32)
        b = jnp.einsum('jka,hab,qcb->hjqkc', oh_r, rpb.astype(F32), oh_c, precision=lax.Precision.HIGHEST)
        valid = row_ok[:, None, :, None] & col_ok[None, :, None, :]
        b = jnp.where(valid[None], b, NEG)
        return b.reshape(rpb.shape[0], NBR_ROWS * GRID_W, NBR_KROWS * GRID_W)

    return jnp.stack([variant(0, 0), variant(NBR_ROWS, NBR_ROWS - kh // 2),
                      variant(rows - NBR_ROWS, rows - NBR_KROWS)])


def _nbr(q, k, v, kc, vc, bias):
    B, H, S, d = q.shape
    L = kc.shape[2]
    tq = NBR_ROWS * GRID_W
    kb = tq // 2
    nb = S // tq
    nh = NBR_HEADS
    assert S % tq == 0 and nb >= 3 and nh * d == LANES

    def kmap(j):
        return lambda h, i, b: (b, h, jnp.clip(2 * i - 1, 0, 2 * nb - 4) + j, 0)

    def bmap(h, i, b):
        return (jnp.where(i == 0, 0, jnp.where(i == nb - 1, 2, 1)), h, 0, 0)

    kv_specs = [pl.BlockSpec((1, nh, kb, d), kmap(j)) for j in range(4)]
    return pl.pallas_call(
        _nbr_kernel,
        grid=(H // nh, nb, B),
        in_specs=[pl.BlockSpec((1, nh, tq, d), lambda h, i, b: (b, h, i, 0))] + kv_specs + kv_specs + [
            pl.BlockSpec((1, nh, L, d), lambda h, i, b: (b, h, 0, 0)),
            pl.BlockSpec((1, nh, L, d), lambda h, i, b: (b, h, 0, 0)),
            pl.BlockSpec((1, nh, tq, NBR_KROWS * GRID_W), bmap),
        ],
        out_specs=pl.BlockSpec((1, tq, LANES), lambda h, i, b: (b, i, h)),
        out_shape=jax.ShapeDtypeStruct((B, S, H * d), BF16),
        compiler_params=_cparams(("parallel", "parallel", "parallel")),
        name="nbr_attn",
    )(q, k, k, k, k, v, v, v, v, kc, vc, bias)


def _gelu(x):
    return 0.5 * x * (1.0 + jnp.tanh(math.sqrt(2.0 / math.pi) * (x + 0.044715 * (x * x * x))))


def _sgu_kernel(u_ref, v_ref, g_ref, w_ref, b_ref, o_ref, *, nchunk):
    grp = lax.broadcasted_iota(I32, (SGU_CHUNK, BRANCH_W), 1) // (BRANCH_W // SGU_GROUPS)
    for c in range(nchunk):
        rows = slice(c * SGU_CHUNK, (c + 1) * SGU_CHUNK)
        u = u_ref[0, rows, :].astype(F32)
        v = _gelu(v_ref[0, rows, :].astype(F32))
        mu = jnp.mean(v, axis=-1, keepdims=True)
        var = jnp.mean(jnp.square(v - mu), axis=-1, keepdims=True)
        vn = ((v - mu) * lax.rsqrt(var + EPS) * g_ref[...]).astype(BF16)
        mixed = b_ref[...]
        for g in range(SGU_GROUPS):
            mg = jnp.dot(w_ref[g], vn, preferred_element_type=F32)
            mixed = mixed + jnp.where(grp == g, mg, 0.0)
        o_ref[0, rows, :] = (_gelu(u) * mixed).astype(o_ref.dtype)


def _sgu(p, norm_g, w_s, b_full, tt):
    B, S, _ = p.shape
    tt = min(tt, S)
    return pl.pallas_call(
        functools.partial(_sgu_kernel, nchunk=tt // SGU_CHUNK),
        grid=(B, S // tt),
        in_specs=[
            pl.BlockSpec((1, tt, BRANCH_W), lambda b, i: (b, i, SEG_U)),
            pl.BlockSpec((1, tt, BRANCH_W), lambda b, i: (b, i, SEG_V)),
            pl.BlockSpec((1, BRANCH_W), lambda b, i: (0, 0)),
            pl.BlockSpec((SGU_GROUPS, SGU_CHUNK, SGU_CHUNK), lambda b, i: (0, 0, 0)),
            pl.BlockSpec((SGU_CHUNK, BRANCH_W), lambda b, i: (0, 0)),
        ],
        out_specs=pl.BlockSpec((1, tt, BRANCH_W), lambda b, i: (b, i, 0)),
        out_shape=jax.ShapeDtypeStruct((B, S, BRANCH_W), BF16),
        compiler_params=_cparams(("parallel", "parallel")),
        name="sgu",
    )(p, p, norm_g, w_s, b_full)


def _merge_kernel(ya, yb, yc, yd, g0, g1, g2, g3, wb_ref, wo_ref, x_ref, gate_ref, mult_ref, shift_ref,
                  wr_ref, xo_ref, h2_ref, lg_ref):
    mix = None
    for n, (y, g) in enumerate(((ya, g0), (yb, g1), (yc, g2), (yd, g3))):
        pr = jnp.dot(y[0], wb_ref[n], preferred_element_type=F32)
        t = g[0].astype(F32) * pr
        mix = t if mix is None else mix + t
    out = jnp.dot(mix.astype(BF16), wo_ref[...], preferred_element_type=F32)
    xn = x_ref[0] + gate_ref[0] * out
    xo_ref[0] = xn
    ms = jnp.mean(xn * xn, axis=-1, keepdims=True)
    h2 = xn * lax.rsqrt(ms + EPS) * mult_ref[0] + shift_ref[0]
    h2_ref[0] = h2.astype(BF16)
    lg_ref[0] = lax.dot_general(wr_ref[...], h2, (((1,), (1,)), ((), ())),
                                precision=lax.Precision.HIGHEST, preferred_element_type=F32)


def _merge(ys, p, w_branch, w_out, x, gate, mult2, shift2, w_router_t, tm):
    B, S, D = x.shape
    E = w_router_t.shape[0]
    tm = min(tm, S)
    y_spec = pl.BlockSpec((1, tm, BRANCH_W), lambda b, i: (b, i, 0))
    g_specs = [pl.BlockSpec((1, tm, D), functools.partial(lambda b, i, n: (b, i, n), n=n)) for n in range(N_BRANCH)]
    vec = pl.BlockSpec((1, 1, D), lambda b, i: (b, 0, 0))
    return pl.pallas_call(
        _merge_kernel,
        grid=(B, S // tm),
        in_specs=[y_spec] * 4 + g_specs + [
            pl.BlockSpec((N_BRANCH, BRANCH_W, D), lambda b, i: (0, 0, 0)),
            pl.BlockSpec((D, D), lambda b, i: (0, 0)),
            pl.BlockSpec((1, tm, D), lambda b, i: (b, i, 0)),
            vec, vec, vec,
            pl.BlockSpec((E, D), lambda b, i: (0, 0)),
        ],
        out_specs=[
            pl.BlockSpec((1, tm, D), lambda b, i: (b, i, 0)),
            pl.BlockSpec((1, tm, D), lambda b, i: (b, i, 0)),
            pl.BlockSpec((1, E, tm), lambda b, i: (b, 0, i)),
        ],
        out_shape=[
            jax.ShapeDtypeStruct((B, S, D), F32),
            jax.ShapeDtypeStruct((B, S, D), BF16),
            jax.ShapeDtypeStruct((B, E, S), F32),
        ],
        compiler_params=_cparams(("parallel", "parallel")),
        name="merge",
    )(*ys, p, p, p, p, w_branch, w_out, x, gate, mult2, shift2, w_router_t)


def _cumsum_excl(x, tri):
    n = x.shape[1]
    outs = []
    carry = jnp.zeros((x.shape[0], 1), F32)
    for c in range(n // 128):
        xc = x[:, c * 128:(c + 1) * 128]
        outs.append(jnp.dot(xc.astype(BF16), tri, preferred_element_type=F32) + carry)
        carry = carry + jnp.sum(xc, axis=1, keepdims=True)
    return jnp.concatenate(outs, axis=1)


def _route_kernel(lg_ref, rank_ref, score_ref, count_ref, *, cap):
    lg = lg_ref[0]
    mx = jnp.max(lg, axis=0, keepdims=True)
    ex = jnp.exp(lg - mx)
    aff = ex / jnp.sum(ex, axis=0, keepdims=True)
    E = lg.shape[0]

    def bisect(i, thr_bits):
        cand = thr_bits | jnp.left_shift(jnp.int32(1), 30 - i)
        cnt = jnp.sum(jnp.where(aff >= pltpu.bitcast(cand, F32), 1, 0), axis=1, keepdims=True)
        return jnp.where(cnt >= cap, cand, thr_bits)

    thr = pltpu.bitcast(lax.fori_loop(0, 31, bisect, jnp.zeros((E, 1), I32)), F32)
    gt = aff > thr
    eq = aff == thr
    need = (cap - jnp.sum(jnp.where(gt, 1, 0), axis=1, keepdims=True)).astype(F32)
    ri = lax.broadcasted_iota(I32, (128, 128), 0)
    ci = lax.broadcasted_iota(I32, (128, 128), 1)
    tri = jnp.where(ri < ci, 1.0, 0.0).astype(BF16)
    eq_before = _cumsum_excl(jnp.where(eq, 1.0, 0.0), tri)
    sel = gt | (eq & (eq_before < need))
    rank = _cumsum_excl(jnp.where(sel, 1.0, 0.0), tri)
    sel = sel & (rank < cap)
    rank_ref[0] = jnp.where(sel, rank.astype(I32), -1)
    score_ref[0] = jnp.where(sel, aff, 0.0)
    n = lg.shape[1]
    chunk_of_token = lax.broadcasted_iota(I32, (n, LANES), 0) // MOE_CHUNK
    member = jnp.where(chunk_of_token == lax.broadcasted_iota(I32, (n, LANES), 1), 1.0, 0.0).astype(BF16)
    count_ref[0] = jnp.dot(jnp.where(sel, 1.0, 0.0).astype(BF16), member, preferred_element_type=F32).astype(I32)


def _route(logits, cap):
    B, E, n = logits.shape
    assert n // MOE_CHUNK <= LANES
    spec = pl.BlockSpec((1, E, n), lambda b: (b, 0, 0))
    rank, score, count = pl.pallas_call(
        functools.partial(_route_kernel, cap=cap),
        grid=(B,),
        in_specs=[spec],
        out_specs=[spec, spec, pl.BlockSpec((1, E, LANES), lambda b: (b, 0, 0))],
        out_shape=[jax.ShapeDtypeStruct((B, E, n), I32), jax.ShapeDtypeStruct((B, E, n), F32),
                   jax.ShapeDtypeStruct((B, E, LANES), I32)],
        compiler_params=_cparams(("parallel",)),
        name="route",
    )(logits)
    return rank, score, count[..., :n // MOE_CHUNK]


SLOT_ALIGN = 16


def _window_start(first, j, W, cap):
    return pl.multiple_of(jnp.minimum((first // SLOT_ALIGN) * SLOT_ALIGN + j * W, cap - W), SLOT_ALIGN)


def _windows_needed(lo, hi, W):
    return (hi - (lo // SLOT_ALIGN) * SLOT_ALIGN + W - 1) // W


def _gather_kernel(cnt_ref, rank_ref, h_ref, o_ref, *, nc, W, unroll, flag_off):
    b = pl.program_id(0)
    e = pl.program_id(1)
    be = b * pl.num_programs(1) + e
    base = be * (nc + 1)
    cap = o_ref.shape[2]
    o_ref[...] = jnp.zeros(o_ref.shape, o_ref.dtype)
    T = MOE_CHUNK

    def window(c, j):
        lo = cnt_ref[base + c]
        r = rank_ref[0, 0, pl.ds(c, 1), :]
        hc = h_ref[0, pl.ds(pl.multiple_of(c * T, T), T), :]
        st = _window_start(lo, j, W, cap)
        slot = lax.broadcasted_iota(I32, (W, T), 0) + st
        hit = (slot == r) & (slot >= (lo // SLOT_ALIGN) * SLOT_ALIGN + j * W)
        got = jnp.dot(jnp.where(hit, 1.0, 0.0).astype(BF16), hc, preferred_element_type=F32)
        o_ref[0, 0, pl.ds(st, W), :] = o_ref[0, 0, pl.ds(st, W), :] + got.astype(o_ref.dtype)

    def group(g, carry):
        for u in range(unroll):
            window(g * unroll + u, 0)
        return carry

    lax.fori_loop(0, nc // unroll, group, 0)

    def tail(c, carry):
        def more(j, carry2):
            window(c, j)
            return carry2

        lax.fori_loop(1, _windows_needed(cnt_ref[base + c], cnt_ref[base + c + 1], W), more, 0)
        return carry

    @pl.when(cnt_ref[flag_off + be] > 0)
    def _():
        lax.fori_loop(0, nc, tail, 0)


def _moe_window(cap):
    return min(128, cap)


def _moe_tables(per_chunk, cap, tt, eg):
    B, E, nc = per_chunk.shape
    n = nc * MOE_CHUNK
    W = _moe_window(cap)
    cnt = jnp.concatenate([jnp.zeros((B, E, 1), I32), jnp.cumsum(per_chunk, axis=-1, dtype=I32)], axis=-1)
    extra = _windows_needed(cnt[..., :-1], cnt[..., 1:], W) > 1
    g_flag = jnp.any(extra, axis=-1)
    nsub = min(tt, n) // MOE_CHUNK
    s_flag = jnp.any(extra.reshape(B, E // eg, eg, nc // nsub, nsub), axis=(2, 4))
    tbl = jnp.concatenate([cnt.reshape(-1), g_flag.reshape(-1).astype(I32), s_flag.reshape(-1).astype(I32)])
    return tbl, cnt.size, cnt.size + g_flag.size


def _gather(tbl, flag_off, rank, h, cap):
    B, E, n = rank.shape
    D = h.shape[-1]
    nc = n // MOE_CHUNK
    return pl.pallas_call(
        functools.partial(_gather_kernel, nc=nc, W=_moe_window(cap), unroll=min(16, nc), flag_off=flag_off),
        grid_spec=pltpu.PrefetchScalarGridSpec(
            num_scalar_prefetch=1,
            grid=(B, E),
            in_specs=[
                pl.BlockSpec((1, 1, nc, MOE_CHUNK), lambda b, e, tbl: (b, e, 0, 0)),
                pl.BlockSpec((1, n, D), lambda b, e, tbl: (b, 0, 0)),
            ],
            out_specs=pl.BlockSpec((1, 1, cap, D), lambda b, e, tbl: (b, e, 0, 0)),
        ),
        out_shape=jax.ShapeDtypeStruct((B, E, cap, D), BF16),
        compiler_params=_cparams(("parallel", "arbitrary")),
        name="moe_gather",
    )(tbl, rank.reshape(B, E, nc, MOE_CHUNK), h)


def _ffn_kernel(*refs, nsets):
    x_refs = refs[:nsets]
    wg_ref, wu_ref, wd_ref = refs[nsets:nsets + 3]
    o_refs = refs[nsets + 3:2 * nsets + 3]
    acc_scrs = refs[2 * nsets + 3:]
    f = pl.program_id(1)

    @pl.when(f == 0)
    def _():
        for acc_scr in acc_scrs:
            acc_scr[...] = jnp.zeros(acc_scr.shape, F32)

    wg = wg_ref[0, 0].astype(BF16)
    wu = wu_ref[0, 0].astype(BF16)
    wd = wd_ref[0, 0].astype(BF16)
    for x_ref, acc_scr in zip(x_refs, acc_scrs):
        for b in range(x_ref.shape[0]):
            x = x_ref[b, 0]
            g = jnp.dot(x, wg, preferred_element_type=F32)
            u = jnp.dot(x, wu, preferred_element_type=F32)
            hid = (g * (0.5 + 0.5 * jnp.tanh(0.5 * g)) * u).astype(BF16)
            acc_scr[b] = acc_scr[b] + jnp.dot(hid, wd, preferred_element_type=F32)

    @pl.when(f == pl.num_programs(1) - 1)
    def _():
        for o_ref, acc_scr in zip(o_refs, acc_scrs):
            o_ref[:, 0] = acc_scr[...].astype(o_ref.dtype)


def _ffn(xins, w_gate, w_up, w_down, layer, tf):
    E, D = xins[0].shape[1], xins[0].shape[3]
    Fh = w_gate.shape[-1]
    x_specs = [pl.BlockSpec((x.shape[0], 1, x.shape[2], D), lambda e, f: (0, e, 0, 0)) for x in xins]
    return pl.pallas_call(
        functools.partial(_ffn_kernel, nsets=len(xins)),
        grid=(E, Fh // tf),
        in_specs=x_specs + [
            pl.BlockSpec((1, 1, D, tf), lambda e, f: (layer, e, 0, f)),
            pl.BlockSpec((1, 1, D, tf), lambda e, f: (layer, e, 0, f)),
            pl.BlockSpec((1, 1, tf, D), lambda e, f: (layer, e, f, 0)),
        ],
        out_specs=x_specs,
        out_shape=[jax.ShapeDtypeStruct(x.shape, BF16) for x in xins],
        scratch_shapes=[pltpu.VMEM((x.shape[0], x.shape[2], D), F32) for x in xins],
        compiler_params=_cparams(("parallel", "arbitrary")),
        name="moe_ffn",
    )(*xins, w_gate, w_up, w_down)


SCATTER_EG = 4
SCATTER_TT = 1024


def _scatter_kernel(cnt_ref, rank_ref, score_ref, y_ref, x_ref, g2_ref, fg_ref, o_ref, acc_scr, *,
                    nc, W, final, flag_off):
    b = pl.program_id(0)
    i = pl.program_id(1)
    g = pl.program_id(2)
    G = pl.num_programs(2)
    eg = y_ref.shape[1]
    T = MOE_CHUNK
    nsub = acc_scr.shape[0] // T
    cap = y_ref.shape[2]

    @pl.when(g == 0)
    def _():
        acc_scr[...] = jnp.zeros(acc_scr.shape, F32)

    def window(c, ee, j):
        rows = slice(c * T, (c + 1) * T)
        lo = cnt_ref[((b * G + g) * eg + ee) * (nc + 1) + i * nsub + c]
        rc = rank_ref[0, 0, rows, ee:ee + 1]
        sc = score_ref[0, 0, rows, ee:ee + 1]
        st = _window_start(lo, j, W, cap)
        slot = lax.broadcasted_iota(I32, (T, W), 1) + st
        hit = (slot == rc) & (slot >= (lo // SLOT_ALIGN) * SLOT_ALIGN + j * W)
        got = jnp.dot(jnp.where(hit, 1.0, 0.0).astype(BF16), y_ref[0, ee, pl.ds(st, W), :],
                      preferred_element_type=F32)
        return sc * got

    for c in range(nsub):
        rows = slice(c * T, (c + 1) * T)
        acc_scr[rows, :] = acc_scr[rows, :] + functools.reduce(jnp.add, [window(c, ee, 0) for ee in range(eg)])

    @pl.when(cnt_ref[flag_off + (b * G + g) * pl.num_programs(1) + i] > 0)
    def _():
        for c in range(nsub):
            for ee in range(eg):
                def more(j, carry, c=c, ee=ee):
                    rows = slice(c * T, (c + 1) * T)
                    acc_scr[rows, :] = acc_scr[rows, :] + window(c, ee, j)
                    return carry

                base = ((b * G + g) * eg + ee) * (nc + 1) + i * nsub + c
                lax.fori_loop(1, _windows_needed(cnt_ref[base], cnt_ref[base + 1], W), more, 0)

    @pl.when(g == G - 1)
    def _():
        xn = x_ref[0] + g2_ref[0] * acc_scr[...]
        if final:
            ms = jnp.mean(xn * xn, axis=-1, keepdims=True)
            xn = xn * lax.rsqrt(ms + EPS) * fg_ref[...]
        o_ref[0] = xn


def _scatter(tbl, flag_off, rank, score, y, x, g2, final_g, final):
    B, E, n = rank.shape
    cap, D = y.shape[2], y.shape[3]
    tt = min(SCATTER_TT, n)
    eg = SCATTER_EG
    nc = n // MOE_CHUNK

    def token_major(t):
        return t.reshape(B, E // eg, eg, n).transpose(0, 1, 3, 2)

    return pl.pallas_call(
        functools.partial(_scatter_kernel, nc=nc, W=_moe_window(cap), final=final, flag_off=flag_off),
        grid_spec=pltpu.PrefetchScalarGridSpec(
            num_scalar_prefetch=1,
            grid=(B, n // tt, E // eg),
            in_specs=[
                pl.BlockSpec((1, 1, tt, eg), lambda b, i, g, tbl: (b, g, i, 0)),
                pl.BlockSpec((1, 1, tt, eg), lambda b, i, g, tbl: (b, g, i, 0)),
                pl.BlockSpec((1, eg, cap, D), lambda b, i, g, tbl: (b, g, 0, 0)),
                pl.BlockSpec((1, tt, D), lambda b, i, g, tbl: (b, i, 0)),
                pl.BlockSpec((1, 1, D), lambda b, i, g, tbl: (b, 0, 0)),
                pl.BlockSpec((1, D), lambda b, i, g, tbl: (0, 0)),
            ],
            out_specs=pl.BlockSpec((1, tt, D), lambda b, i, g, tbl: (b, i, 0)),
            scratch_shapes=[pltpu.VMEM((tt, D), F32)],
        ),
        out_shape=jax.ShapeDtypeStruct((B, n, D), F32),
        compiler_params=_cparams(("parallel", "parallel", "arbitrary")),
        name="moe_scatter",
    )(tbl, token_major(rank), token_major(score), y, x, g2, final_g)


def _moe_dispatch(h2, logits):
    n = h2.shape[1]
    cap = CAPACITY_FACTOR * n // N_EXPERTS
    rank, score, per_chunk = _route(logits, cap)
    tbl, g_off, s_off = _moe_tables(per_chunk, cap, SCATTER_TT, SCATTER_EG)
    return (tbl, s_off, rank, score), _gather(tbl, g_off, rank, h2, cap)


def _moe_combine(routing, y, x, g2, final_g, final):
    tbl, s_off, rank, score = routing
    return _scatter(tbl, s_off, rank, score, y, x, g2, final_g, final)


LOG2E = math.log2(math.e)
HEADS_PER_BLK = COL_BLK // HEAD_DIM


def _prep_kernel(*refs, rope, qa_scale):
    qa_r, qb_r, qd_r, ka_r, va_r, kb_r, vb_r, kvd_r = refs[:8]
    refs = refs[8:]
    if rope:
        cb_r, sb_r, cd_r, sd_r = refs[:4]
        refs = refs[4:]
    qg_r, kg_r, gm_r = refs[:3]
    qa_o, qb_o, qd_o, ka_o, va_o, kb_o, vb_o, kd_o, vd_o = refs[3:]
    ts = qa_r.shape[1]
    kvw = GQA_KV_HEADS * HEAD_DIM

    def partner(x, half):
        n = x.shape[1]
        lane = lax.broadcasted_iota(I32, x.shape, 1)
        return jnp.where((lane & half) == 0, pltpu.roll(x, n - half, 1), pltpu.roll(x, half, 1))

    def rot(x, cos, sin, half):
        return x * cos + partner(x, half) * sin

    def group_norm(x, gm, gain):
        ms = jnp.dot(x * x, gm, precision=lax.Precision.HIGHEST, preferred_element_type=F32)
        return x * lax.rsqrt(ms + EPS) * gain

    def head(x, h):
        return x[:, h * HEAD_DIM:(h + 1) * HEAD_DIM]

    lane64 = lax.broadcasted_iota(I32, (ts, HEAD_DIM), 1)
    ones_tail = jnp.where(lane64 == 0, 1.0, 0.0).astype(BF16)

    qa = qa_r[0].astype(F32) * qa_scale
    for h in range(HEADS_PER_BLK):
        qa_o[0, h] = head(qa, h).astype(BF16)
        ka_o[0, h] = head(ka_r[0], h)
        va_o[0, h] = head(va_r[0], h)

    qb = qb_r[0].astype(F32)
    kb = kb_r[0].astype(F32)
    if rope:
        qb = rot(qb, cb_r[...], sb_r[...], DIFF_QK_DIM // 2)
        kb = rot(kb, cb_r[...], sb_r[...], DIFF_QK_DIM // 2)
    qb = qb * (DIFF_QK_DIM ** -0.5 * LOG2E)
    for h in range(HEADS_PER_BLK):
        xh = head(qb, h)
        qb_o[0, h // 2, h % 2, 0] = jnp.where(lane64 < DIFF_QK_DIM, xh, 0.0).astype(BF16)
        qb_o[0, h // 2, h % 2, 1] = jnp.where(lane64 >= DIFF_QK_DIM, xh, 0.0).astype(BF16)
        kb_o[0, h // 2, h % 2] = head(kb, h).astype(BF16)
        vb_o[0, h // 2, h % 2] = jnp.concatenate([head(vb_r[0], h), ones_tail], axis=1)

    qd = group_norm(qd_r[0].astype(F32), gm_r[...], qg_r[...])
    kd = group_norm(kvd_r[0, :, :kvw].astype(F32), gm_r[:kvw, :kvw], kg_r[...])
    if rope:
        qd = rot(qd, cd_r[...], sd_r[...], HEAD_DIM // 2)
        kd = rot(kd, cd_r[:, :kvw], sd_r[:, :kvw], HEAD_DIM // 2)
    qd = qd * (HEAD_DIM ** -0.5 * LOG2E)
    grp = GQA_Q_HEADS // GQA_KV_HEADS
    for h in range(GQA_Q_HEADS):
        qd_o[0, h // grp, 0, h % grp] = head(qd, h).astype(BF16)
    vd = kvd_r[0, :, kvw:]
    for h in range(GQA_KV_HEADS):
        kd_o[0, h, 0] = head(kd, h).astype(BF16)
        vd_o[0, h, 0] = jnp.concatenate([head(vd, h), ones_tail], axis=1)


def _prep(p, tables, qg, kg, gm, qa_scale, ts):
    B, S, _ = p.shape
    ts = min(ts, S)

    def seg(blk):
        return pl.BlockSpec((1, ts, COL_BLK), lambda b, i: (b, i, blk))

    in_specs = [seg(s) for s in (SEG_QA, SEG_QB, SEG_QD, SEG_KA, SEG_VA, SEG_KB, SEG_VB, SEG_VB + 1)]
    args = [p] * 8
    if tables is not None:
        in_specs += [pl.BlockSpec((ts, COL_BLK), lambda b, i: (i, 0))] * 4
        args += list(tables)
    in_specs += [pl.BlockSpec(a.shape, lambda b, i: (0, 0)) for a in (qg, kg, gm)]
    args += [qg, kg, gm]

    def out(lead, width):
        shape = (B,) + lead + (S, width)
        nl = len(lead)
        spec = pl.BlockSpec((1,) + lead + (ts, width), lambda b, i: (b,) + (0,) * nl + (i, 0))
        return jax.ShapeDtypeStruct(shape, BF16), spec

    grp = GQA_Q_HEADS // GQA_KV_HEADS
    outs = [out((NA_HEADS,), HEAD_DIM), out((DIFF_HEADS // 2, 2, 2), HEAD_DIM),
            out((GQA_KV_HEADS, 1, grp), HEAD_DIM), out((NA_HEADS,), HEAD_DIM), out((NA_HEADS,), HEAD_DIM),
            out((DIFF_HEADS // 2, 2), HEAD_DIM), out((DIFF_HEADS // 2, 2), LANES),
            out((GQA_KV_HEADS, 1), HEAD_DIM), out((GQA_KV_HEADS, 1), LANES)]
    return pl.pallas_call(
        functools.partial(_prep_kernel, rope=tables is not None, qa_scale=qa_scale),
        grid=(B, S // ts),
        in_specs=in_specs,
        out_specs=[o[1] for o in outs],
        out_shape=[o[0] for o in outs],
        compiler_params=_cparams(("parallel", "parallel")),
        name="attn_prep",
    )(*args)


def _rope_lane_tables(n, dim):
    t = jnp.arange(n)
    row = (t // GRID_W).astype(F32)
    col = (t % GRID_W).astype(F32)
    n_pairs = dim // 4
    inv = ROPE_THETA ** (-jnp.arange(n_pairs, dtype=F32) / n_pairs)
    ang = jnp.concatenate([row[:, None] * inv, col[:, None] * inv], axis=-1)
    cos, sin = jnp.cos(ang), jnp.sin(ang)
    reps = COL_BLK // dim
    return jnp.tile(jnp.concatenate([cos, cos], -1), (1, reps)), jnp.tile(jnp.concatenate([-sin, sin], -1), (1, reps))


def _deinterleave(w, width):
    lead = w.shape[:-1]
    n = w.shape[-1]
    return jnp.swapaxes(w.reshape(lead + (n // width, width // 2, 2)), -1, -2).reshape(lead + (n,))


def _with_ones(v):
    pad = [(0, 0)] * (v.ndim - 1) + [(0, LANES - v.shape[-1] - 1)]
    return jnp.pad(jnp.concatenate([v, jnp.ones(v.shape[:-1] + (1,), v.dtype)], axis=-1), pad)


def _mixer_branches(p, ops, ops_ctx, bias_a, lam, lam_init, sub_g, sgu_g, sgu_w, sgu_bf):
    qa, qb, qd, ka, va, kb, vb, kd, vd = ops
    b, _, s, _ = qa.shape
    if ops_ctx is None:
        y_a = _flash(qa.reshape(b, NA_HEADS // 2, 2, 1, s, HEAD_DIM), [ka.reshape(b, NA_HEADS // 2, 2, s, HEAD_DIM)],
                     [_with_ones(va).reshape(b, NA_HEADS // 2, 2, s, LANES)], tq=256, tk=256)
        kbs, vbs, kds, vds = [kb], [vb], [kd], [vd]
    else:
        _, _, _, kac, vac, kbc, vbc, kdc, vdc = ops_ctx
        y_a = _nbr(qa, ka, va, kac, vac, bias_a)
        kbs, vbs, kds, vds = [kbc, kb], [vbc, vb], [kdc, kd], [vdc, vd]
    lam_v = jnp.full((1, DIFF_V_DIM), lam, F32)
    gain_v = (sub_g.astype(F32) * (1.0 - lam_init))[None, :]
    y_b = _flash(qb, kbs, vbs, tq=256, tk=2048, diff=(lam_v, gain_v))
    y_d = _flash(qd, kds, vds, tq=512, tk=2048)
    y_c = _sgu(p, sgu_g, sgu_w, sgu_bf, tt=1024)
    return [y_a, y_b, y_c, y_d]


def kernel(x, c, ctx, c_ctx, w_mod, b_mod, norm1_g, norm2_g, w_in, q_gain, k_gain, na_rpb, lambda_q1, lambda_k1, lambda_q2, lambda_k2, diff_sub_g, sgu_norm_g, sgu_w, sgu_b, gate_b, w_branch, w_out, w_router, w_e_gate, w_e_up, w_e_down, final_g):
    B, S, D = x.shape
    depth = w_mod.shape[0]
    rows = S // GRID_W
    tables = _rope_lane_tables(S, DIFF_QK_DIM) + _rope_lane_tables(S, HEAD_DIM)
    hp = lax.Precision.HIGHEST
    fg = final_g.astype(F32)[None, :]
    grp_id = jnp.arange(COL_BLK) // HEAD_DIM
    gm = (grp_id[:, None] == grp_id[None, :]).astype(F32) / HEAD_DIM
    for l in range(depth):
        last = l == depth - 1
        mod = jnp.dot(jax.nn.silu(c), w_mod[l], precision=hp) + b_mod[l]
        sh1, sc1, g1, sh2, sc2, g2 = (t[:, None, :] for t in jnp.split(mod, 6, axis=-1))
        mod_c = jnp.dot(jax.nn.silu(c_ctx), w_mod[l], precision=hp) + b_mod[l]
        csh1, csc1, cg1, csh2, csc2, cg2 = (jnp.broadcast_to(t[None, None, :], (B, 1, D))
                                            for t in jnp.split(mod_c, 6, axis=-1))
        lam_init = 0.8 - 0.6 * math.exp(-0.3 * l)
        lam = (jnp.exp(jnp.sum(lambda_q1[l].astype(F32) * lambda_k1[l].astype(F32)))
               - jnp.exp(jnp.sum(lambda_q2[l].astype(F32) * lambda_k2[l].astype(F32))) + lam_init)

        wl = w_in[l]
        q0, kv0 = 0, wl.shape[-1] - 5 * COL_BLK

        def cols(start, blk, n=1):
            return wl[:, start + blk * COL_BLK:start + (blk + n) * COL_BLK]

        kdw = GQA_KV_HEADS * HEAD_DIM
        w_l = jnp.concatenate([
            wl[:, 5 * COL_BLK:kv0],
            cols(q0, 0), _deinterleave(cols(q0, 1), DIFF_QK_DIM), _deinterleave(cols(q0, 2), HEAD_DIM),
            cols(q0, 3, 2),
            cols(kv0, 0, 2), _deinterleave(cols(kv0, 2), DIFF_QK_DIM), cols(kv0, 3),
            _deinterleave(wl[:, kv0 + 4 * COL_BLK:kv0 + 4 * COL_BLK + kdw], HEAD_DIM),
            wl[:, kv0 + 4 * COL_BLK + kdw:],
        ], axis=1).astype(BF16)
        qg = jnp.tile(_deinterleave(q_gain[l].astype(F32), HEAD_DIM), GQA_Q_HEADS)[None, :]
        kg = jnp.tile(_deinterleave(k_gain[l].astype(F32), HEAD_DIM), GQA_KV_HEADS)[None, :]
        bias_in = jnp.concatenate([gate_b[l].astype(F32), jnp.zeros((w_l.shape[1] - GATE_W,), F32)])[None, :]
        n1 = norm1_g[l].astype(F32)[None, None, :]
        n2 = norm2_g[l].astype(F32)[None, None, :]
        wb = w_branch[l].astype(BF16)
        wo = w_out[l].astype(BF16)
        wr_t = w_router[l].astype(F32).T
        sgu_g = sgu_norm_g[l].astype(F32)[None, :]
        sgu_wb = sgu_w[l].astype(BF16)
        sgu_bf = jnp.repeat(sgu_b[l].astype(F32).T, BRANCH_W // SGU_GROUPS, axis=1)
        bias_a = _nbr_bias(na_rpb[l], rows)

        tn = w_l.shape[1] // 4
        p = _inproj(x, n1 * (1.0 + sc1), sh1, w_l, bias_in, GATE_W, tm=2048, tn=tn)
        ops = _prep(p, tables, qg, kg, gm, HEAD_DIM ** -0.5, ts=512)
        pc = _inproj(ctx, n1 * (1.0 + csc1), csh1, w_l, bias_in, GATE_W, tm=256, tn=tn)
        ops_c = _prep(pc, None, qg, kg, gm, HEAD_DIM ** -0.5 * LOG2E, ts=256)
        if not last:
            ys_c = _mixer_branches(pc, ops_c, None, None, lam, lam_init, diff_sub_g[l], sgu_g, sgu_wb, sgu_bf)
            ctx, hc2, lg_c = _merge(ys_c, pc, wb, wo, ctx, cg1, n2 * (1.0 + csc2), csh2, wr_t, tm=256)
        ys = _mixer_branches(p, ops, ops_c, bias_a, lam, lam_init, diff_sub_g[l], sgu_g, sgu_wb, sgu_bf)
        x, h2, lg = _merge(ys, p, wb, wo, x, g1, n2 * (1.0 + sc2), sh2, wr_t, tm=512)

        routings, xins = zip(*([_moe_dispatch(h2, lg)] + ([] if last else [_moe_dispatch(hc2, lg_c)])))
        ys_e = _ffn(list(xins), w_e_gate, w_e_up, w_e_down, l, tf=512)
        x = _moe_combine(routings[0], ys_e[0], x, g2, fg, final=last)
        if not last:
            ctx = _moe_combine(routings[1], ys_e[1], ctx, cg2, fg, final=False)
    return x
```

```python
import functools
import math

import jax
import jax.numpy as jnp
from jax import lax
from jax.experimental import pallas as pl
from jax.experimental.pallas import tpu as pltpu

F32 = jnp.float32
BF16 = jnp.bfloat16
I32 = jnp.int32

GRID_W = 64
HEAD_DIM = 64
N_BRANCH = 4
BRANCH_W = 256
NA_HEADS = 4
WIN_H = 8
WIN_W = 16
DIFF_HEADS = 4
DIFF_QK_DIM = 32
DIFF_V_DIM = 64
SGU_GROUPS = 4
SGU_CHUNK = 128
GQA_Q_HEADS = 4
GQA_KV_HEADS = 2
ROPE_THETA = 10000.0
N_EXPERTS = 16
CAPACITY_FACTOR = 2
EPS = 1e-6
NEG = -1e30

GATE_W = N_BRANCH * 1024
COL_BLK = 256
SEG_QA, SEG_QB, SEG_QD, SEG_U, SEG_V, SEG_KA, SEG_VA, SEG_KB, SEG_VB = range(16, 25)
COL_KD = GATE_W + 9 * COL_BLK
KV_COL0 = GATE_W + 5 * COL_BLK

VMEM_LIMIT = 56 * 1024 * 1024

NBR_ROWS = 8
NBR_KROWS = 16
MOE_CHUNK = 256


def _cparams(sem):
    return pltpu.CompilerParams(dimension_semantics=sem, vmem_limit_bytes=VMEM_LIMIT)


def _dot_nt(a, b):
    return lax.dot_general(a, b, (((1,), (1,)), ((), ())), preferred_element_type=F32)


def _inproj_kernel(x_ref, mult_ref, shift_ref, w_ref, bias_ref, o_ref, h_scr, *, n_gate_cols):
    j = pl.program_id(2)
    tn = o_ref.shape[-1]

    @pl.when(j == 0)
    def _():
        x = x_ref[0]
        ms = jnp.mean(x * x, axis=-1, keepdims=True)
        h = x * lax.rsqrt(ms + EPS) * mult_ref[0] + shift_ref[0]
        h_scr[...] = h.astype(BF16)

    acc = jnp.dot(h_scr[...], w_ref[...], preferred_element_type=F32) + bias_ref[...]

    @pl.when(j * tn < n_gate_cols)
    def _():
        col = j * tn + lax.broadcasted_iota(I32, acc.shape, 1)
        o_ref[0] = jnp.where(col < n_gate_cols, 0.5 + 0.5 * jnp.tanh(0.5 * acc), acc).astype(o_ref.dtype)

    @pl.when(j * tn >= n_gate_cols)
    def _():
        o_ref[0] = acc.astype(o_ref.dtype)


def _inproj(x, mult, shift, w, bias, n_gate_cols, tm, tn):
    B, S, D = x.shape
    N = w.shape[1]
    tm = min(tm, S)
    return pl.pallas_call(
        functools.partial(_inproj_kernel, n_gate_cols=n_gate_cols),
        grid=(B, S // tm, N // tn),
        in_specs=[
            pl.BlockSpec((1, tm, D), lambda b, i, j: (b, i, 0)),
            pl.BlockSpec((1, 1, D), lambda b, i, j: (b, 0, 0)),
            pl.BlockSpec((1, 1, D), lambda b, i, j: (b, 0, 0)),
            pl.BlockSpec((D, tn), lambda b, i, j: (0, j)),
            pl.BlockSpec((1, tn), lambda b, i, j: (0, j)),
        ],
        out_specs=pl.BlockSpec((1, tm, tn), lambda b, i, j: (b, i, j)),
        out_shape=jax.ShapeDtypeStruct((B, S, N), BF16),
        scratch_shapes=[pltpu.VMEM((tm, D), BF16)],
        compiler_params=_cparams(("parallel", "parallel", "arbitrary")),
        name="inproj",
    )(x, mult, shift, w, bias)


LANES = 128


def _flash_kernel(*refs, nseg, tks, J, R, dv, diff):
    if diff:
        lam_ref, subg_ref = refs[:2]
        refs = refs[2:]
    q_ref = refs[0]
    k_refs = refs[1:1 + nseg]
    v_refs = refs[1 + nseg:1 + 2 * nseg]
    o_ref = refs[1 + 2 * nseg]
    m_scr, acc_scr = refs[2 + 2 * nseg:4 + 2 * nseg]
    s_scrs = refs[4 + 2 * nseg:]
    tq, d = q_ref.shape[-2:]
    rows = R * tq
    qs = [q_ref[0, 0, j].reshape(rows, d) for j in range(J)]
    m_scr[...] = jnp.full(m_scr.shape, NEG, F32)
    acc_scr[...] = jnp.zeros(acc_scr.shape, F32)

    def scores(j, k_ref, c, tk):
        return _dot_nt(qs[j], k_ref[0, 0, j, pl.ds(pl.multiple_of(c * tk, tk), tk), :])

    def accumulate(j, s, v_ref, c, tk):
        vc = v_ref[0, 0, j, pl.ds(pl.multiple_of(c * tk, tk), tk), :]
        slabs = [s[:, t * LANES:(t + 1) * LANES] for t in range(tk // LANES)]
        m_cur = functools.reduce(jnp.maximum, slabs)
        m_prev = m_scr[j]
        m_new = jnp.maximum(m_prev, jnp.max(m_cur, axis=1, keepdims=True))
        alpha = jnp.exp2(m_prev - m_new)
        p = jnp.concatenate([jnp.exp2(sl - m_new) for sl in slabs], axis=1).astype(BF16)
        acc_scr[j] = alpha * acc_scr[j] + jnp.dot(p, vc, preferred_element_type=F32)
        m_scr[j] = m_new

    if nseg == 1:
        k_ref, v_ref, tk = k_refs[0], v_refs[0], tks[0]

        def body(c, carry):
            for j in range(J):
                accumulate(j, scores(j, k_ref, c, tk), v_ref, c, tk)
            return carry

        lax.fori_loop(0, k_ref.shape[3] // tk, body, 0)
    else:
        (kc_ref, k_ref), (vc_ref, v_ref), (tkc, tk) = k_refs, v_refs, tks
        assert kc_ref.shape[3] == tkc
        stages = [(kc_ref, vc_ref, 0, tkc)] + [(k_ref, v_ref, c, tk) for c in range(k_ref.shape[3] // tk)]
        bufs = (s_scrs[:J], s_scrs[J:])
        cur = [scores(j, stages[0][0], 0, tkc) for j in range(J)]
        for idx, (_, vr, c, t) in enumerate(stages):
            nxt = stages[idx + 1] if idx + 1 < len(stages) else None
            for j in range(J):
                if nxt is not None:
                    bufs[idx % 2][j][...] = scores(j, nxt[0], nxt[2], nxt[3])
                accumulate(j, cur[j], vr, c, t)
            if nxt is not None:
                cur = [bufs[idx % 2][j][...] for j in range(J)]
    pieces = []
    for j in range(J):
        acc = acc_scr[j]
        o = acc[:, :dv] / acc[:, dv:dv + 1]
        parts = [o[r * tq:(r + 1) * tq] for r in range(R)]
        if diff:
            y = parts[0] - lam_ref[...] * parts[1]
            y = y * lax.rsqrt(jnp.mean(y * y, axis=-1, keepdims=True) + EPS) * subg_ref[...]
            pieces.append(y)
        else:
            pieces.extend(parts)
    o_ref[0] = jnp.concatenate(pieces, axis=1).astype(o_ref.dtype)


def _flash(q, ks, vs, tq, tk, diff=None):
    B, Hs, J, R, Sq, d = q.shape
    dv = LANES // 2
    assert (J if diff else J * R) * dv == LANES
    tq = min(tq, Sq)
    tks = tuple(min(tk, k.shape[3]) for k in ks)
    in_specs = []
    if diff:
        in_specs += [pl.BlockSpec((1, dv), lambda b, h, i: (0, 0))] * 2
    in_specs.append(pl.BlockSpec((1, 1, J, R, tq, d), lambda b, h, i: (b, h, 0, 0, i, 0)))
    for k in ks:
        in_specs.append(pl.BlockSpec((1, 1, J, k.shape[3], d), lambda b, h, i: (b, h, 0, 0, 0)))
    for v in vs:
        in_specs.append(pl.BlockSpec((1, 1, J, v.shape[3], LANES), lambda b, h, i: (b, h, 0, 0, 0)))
    return pl.pallas_call(
        functools.partial(_flash_kernel, nseg=len(ks), tks=tks, J=J, R=R, dv=dv, diff=bool(diff)),
        grid=(B, Hs, Sq // tq),
        in_specs=in_specs,
        out_specs=pl.BlockSpec((1, tq, LANES), lambda b, h, i: (b, i, h)),
        out_shape=jax.ShapeDtypeStruct((B, Sq, Hs * LANES), BF16),
        scratch_shapes=[pltpu.VMEM((J, R * tq, LANES), F32), pltpu.VMEM((J, R * tq, LANES), F32)]
        + ([pltpu.VMEM((R * tq, tks[1]), F32)] * (2 * J) if len(ks) == 2 else []),
        compiler_params=_cparams(("parallel", "parallel", "arbitrary")),
        name="flash_diff" if diff else "flash",
    )(*(diff or ()), q, *ks, *vs)


NBR_HEADS = 2


def _nbr_kernel(q_ref, k0, k1, k2, k3, v0, v1, v2, v3, kc_ref, vc_ref, bias_ref, o_ref):
    outs = []
    for j in range(NBR_HEADS):
        q = q_ref[0, j]
        k = jnp.concatenate([k0[0, j], k1[0, j], k2[0, j], k3[0, j]], axis=0)
        v = jnp.concatenate([v0[0, j], v1[0, j], v2[0, j], v3[0, j]], axis=0)
        s_loc = _dot_nt(q, k) + bias_ref[0, j].astype(F32)
        s_ctx = _dot_nt(q, kc_ref[0, j])
        m = jnp.maximum(jnp.max(s_loc, axis=1, keepdims=True), jnp.max(s_ctx, axis=1, keepdims=True))
        p_loc = jnp.exp2(s_loc - m)
        p_ctx = jnp.exp2(s_ctx - m)
        l = jnp.sum(p_loc, axis=1, keepdims=True) + jnp.sum(p_ctx, axis=1, keepdims=True)
        o = (jnp.dot(p_ctx.astype(BF16), vc_ref[0, j], preferred_element_type=F32)
             + jnp.dot(p_loc.astype(BF16), v, preferred_element_type=F32))
        outs.append(o / l)
    o_ref[0] = jnp.concatenate(outs, axis=1).astype(o_ref.dtype)


def _nbr_bias(rpb, rows):
    kh, kw = WIN_H, WIN_W
    qc = jnp.arange(GRID_W)
    kc = jnp.arange(GRID_W)
    c0 = jnp.clip(qc - kw // 2, 0, GRID_W - kw)
    col_ok = (kc[None, :] >= c0[:, None]) & (kc[None, :] < c0[:, None] + kw)
    dj = kc[None, :] - qc[:, None] + WIN_W - 1
    oh_c = ((dj[..., None] == jnp.arange(2 * WIN_W - 1)) & col_ok[..., None]).astype(F32)

    def variant(r_start, k_start):
        r = r_start + jnp.arange(NBR_ROWS)
        kr = k_start + jnp.arange(NBR_KROWS)
        r0 = jnp.clip(r - kh // 2, 0, rows - kh)
        row_ok = (kr[None, :] >= r0[:, None]) & (kr[None, :] < r0[:, None] + kh)
        di = kr[None, :] - r[:, None] + WIN_H - 1
        oh_r = ((di[..., None] == jnp.arange(2 * WIN_H - 1)) & row_ok[..., None]).astype(F32)
        b = jnp.einsum('jka,hab,qcb->hjqkc', oh_r, rpb.astype(F32), oh_c, precision=lax.Precision.HIGHEST)
        valid = row_ok[:, None, :, None] & col_ok[None, :, None, :]
        b = jnp.where(valid[None], b * LOG2E, NEG).astype(BF16)
        return b.reshape(rpb.shape[0], NBR_ROWS * GRID_W, NBR_KROWS * GRID_W)

    return jnp.stack([variant(0, 0), variant(NBR_ROWS, NBR_ROWS - kh // 2),
                      variant(rows - NBR_ROWS, rows - NBR_KROWS)])


def _nbr(q, k, v, kc, vc, bias):
    B, H, S, d = q.shape
    L = kc.shape[2]
    tq = NBR_ROWS * GRID_W
    kb = tq // 2
    nb = S // tq
    nh = NBR_HEADS
    assert S % tq == 0 and nb >= 3 and nh * d == LANES

    def kmap(j):
        return lambda h, i, b: (b, h, jnp.clip(2 * i - 1, 0, 2 * nb - 4) + j, 0)

    def bmap(h, i, b):
        return (jnp.where(i == 0, 0, jnp.where(i == nb - 1, 2, 1)), h, 0, 0)

    kv_specs = [pl.BlockSpec((1, nh, kb, d), kmap(j)) for j in range(4)]
    return pl.pallas_call(
        _nbr_kernel,
        grid=(H // nh, nb, B),
        in_specs=[pl.BlockSpec((1, nh, tq, d), lambda h, i, b: (b, h, i, 0))] + kv_specs + kv_specs + [
            pl.BlockSpec((1, nh, L, d), lambda h, i, b: (b, h, 0, 0)),
            pl.BlockSpec((1, nh, L, d), lambda h, i, b: (b, h, 0, 0)),
            pl.BlockSpec((1, nh, tq, NBR_KROWS * GRID_W), bmap),
        ],
        out_specs=pl.BlockSpec((1, tq, LANES), lambda h, i, b: (b, i, h)),
        out_shape=jax.ShapeDtypeStruct((B, S, H * d), BF16),
        compiler_params=_cparams(("parallel", "parallel", "parallel")),
        name="nbr_attn",
    )(q, k, k, k, k, v, v, v, v, kc, vc, bias)


def _gelu(x):
    return 0.5 * x * (1.0 + jnp.tanh(math.sqrt(2.0 / math.pi) * (x + 0.044715 * (x * x * x))))


def _sgu_kernel(u_ref, v_ref, g_ref, w_ref, b_ref, o_ref, *, nchunk):
    grp = lax.broadcasted_iota(I32, (SGU_CHUNK, BRANCH_W), 1) // (BRANCH_W // SGU_GROUPS)
    for c in range(nchunk):
        rows = slice(c * SGU_CHUNK, (c + 1) * SGU_CHUNK)
        u = u_ref[0, rows, :].astype(F32)
        v = _gelu(v_ref[0, rows, :].astype(F32))
        mu = jnp.mean(v, axis=-1, keepdims=True)
        var = jnp.mean(jnp.square(v - mu), axis=-1, keepdims=True)
        vn = ((v - mu) * lax.rsqrt(var + EPS) * g_ref[...]).astype(BF16)
        mixed = b_ref[...]
        for g in range(SGU_GROUPS):
            mg = jnp.dot(w_ref[g], vn, preferred_element_type=F32)
            mixed = mixed + jnp.where(grp == g, mg, 0.0)
        o_ref[0, rows, :] = (_gelu(u) * mixed).astype(o_ref.dtype)


def _sgu(p, norm_g, w_s, b_full, tt):
    B, S, _ = p.shape
    tt = min(tt, S)
    return pl.pallas_call(
        functools.partial(_sgu_kernel, nchunk=tt // SGU_CHUNK),
        grid=(B, S // tt),
        in_specs=[
            pl.BlockSpec((1, tt, BRANCH_W), lambda b, i: (b, i, SEG_U)),
            pl.BlockSpec((1, tt, BRANCH_W), lambda b, i: (b, i, SEG_V)),
            pl.BlockSpec((1, BRANCH_W), lambda b, i: (0, 0)),
            pl.BlockSpec((SGU_GROUPS, SGU_CHUNK, SGU_CHUNK), lambda b, i: (0, 0, 0)),
            pl.BlockSpec((SGU_CHUNK, BRANCH_W), lambda b, i: (0, 0)),
        ],
        out_specs=pl.BlockSpec((1, tt, BRANCH_W), lambda b, i: (b, i, 0)),
        out_shape=jax.ShapeDtypeStruct((B, S, BRANCH_W), BF16),
        compiler_params=_cparams(("parallel", "parallel")),
        name="sgu",
    )(p, p, norm_g, w_s, b_full)


def _merge_kernel(ya, yb, yc, yd, g0, g1, g2, g3, wb_ref, wo_ref, x_ref, gate_ref, mult_ref, shift_ref,
                  wr_ref, xo_ref, h2_ref, lg_ref):
    mix = None
    for n, (y, g) in enumerate(((ya, g0), (yb, g1), (yc, g2), (yd, g3))):
        pr = jnp.dot(y[0], wb_ref[n], preferred_element_type=F32)
        t = g[0].astype(F32) * pr
        mix = t if mix is None else mix + t
    out = jnp.dot(mix.astype(BF16), wo_ref[...], preferred_element_type=F32)
    xn = x_ref[0] + gate_ref[0] * out
    xo_ref[0] = xn
    ms = jnp.mean(xn * xn, axis=-1, keepdims=True)
    h2 = xn * lax.rsqrt(ms + EPS) * mult_ref[0] + shift_ref[0]
    h2_ref[0] = h2.astype(BF16)
    lg_ref[0] = lax.dot_general(wr_ref[...], h2, (((1,), (1,)), ((), ())),
                                precision=lax.Precision.HIGHEST, preferred_element_type=F32)


def _merge(ys, p, w_branch, w_out, x, gate, mult2, shift2, w_router_t, tm):
    B, S, D = x.shape
    E = w_router_t.shape[0]
    tm = min(tm, S)
    y_spec = pl.BlockSpec((1, tm, BRANCH_W), lambda b, i: (b, i, 0))
    g_specs = [pl.BlockSpec((1, tm, D), functools.partial(lambda b, i, n: (b, i, n), n=n)) for n in range(N_BRANCH)]
    vec = pl.BlockSpec((1, 1, D), lambda b, i: (b, 0, 0))
    return pl.pallas_call(
        _merge_kernel,
        grid=(B, S // tm),
        in_specs=[y_spec] * 4 + g_specs + [
            pl.BlockSpec((N_BRANCH, BRANCH_W, D), lambda b, i: (0, 0, 0)),
            pl.BlockSpec((D, D), lambda b, i: (0, 0)),
            pl.BlockSpec((1, tm, D), lambda b, i: (b, i, 0)),
            vec, vec, vec,
            pl.BlockSpec((E, D), lambda b, i: (0, 0)),
        ],
        out_specs=[
            pl.BlockSpec((1, tm, D), lambda b, i: (b, i, 0)),
            pl.BlockSpec((1, tm, D), lambda b, i: (b, i, 0)),
            pl.BlockSpec((1, E, tm), lambda b, i: (b, 0, i)),
        ],
        out_shape=[
            jax.ShapeDtypeStruct((B, S, D), F32),
            jax.ShapeDtypeStruct((B, S, D), BF16),
            jax.ShapeDtypeStruct((B, E, S), F32),
        ],
        compiler_params=_cparams(("parallel", "parallel")),
        name="merge",
    )(*ys, p, p, p, p, w_branch, w_out, x, gate, mult2, shift2, w_router_t)


def _cumsum_excl(x, tri):
    n = x.shape[1]
    outs = []
    carry = jnp.zeros((x.shape[0], 1), F32)
    for c in range(n // 128):
        xc = x[:, c * 128:(c + 1) * 128]
        outs.append(jnp.dot(xc.astype(BF16), tri, preferred_element_type=F32) + carry)
        carry = carry + jnp.sum(xc, axis=1, keepdims=True)
    return jnp.concatenate(outs, axis=1)


def _route_kernel(lg_ref, rank_ref, score_ref, count_ref, *, cap):
    lg = lg_ref[0]
    mx = jnp.max(lg, axis=0, keepdims=True)
    ex = jnp.exp(lg - mx)
    aff = ex / jnp.sum(ex, axis=0, keepdims=True)
    E = lg.shape[0]

    def bisect(i, thr_bits):
        cand = thr_bits | jnp.left_shift(jnp.int32(1), 30 - i)
        cnt = jnp.sum(jnp.where(aff >= pltpu.bitcast(cand, F32), 1, 0), axis=1, keepdims=True)
        return jnp.where(cnt >= cap, cand, thr_bits)

    thr = pltpu.bitcast(lax.fori_loop(0, 31, bisect, jnp.zeros((E, 1), I32)), F32)
    gt = aff > thr
    eq = aff == thr
    need = (cap - jnp.sum(jnp.where(gt, 1, 0), axis=1, keepdims=True)).astype(F32)
    ri = lax.broadcasted_iota(I32, (128, 128), 0)
    ci = lax.broadcasted_iota(I32, (128, 128), 1)
    tri = jnp.where(ri < ci, 1.0, 0.0).astype(BF16)
    eq_before = _cumsum_excl(jnp.where(eq, 1.0, 0.0), tri)
    sel = gt | (eq & (eq_before < need))
    rank = _cumsum_excl(jnp.where(sel, 1.0, 0.0), tri)
    sel = sel & (rank < cap)
    rank_ref[0] = jnp.where(sel, rank.astype(I32), -1)
    score_ref[0] = jnp.where(sel, aff, 0.0)
    n = lg.shape[1]
    chunk_of_token = lax.broadcasted_iota(I32, (n, LANES), 0) // MOE_CHUNK
    member = jnp.where(chunk_of_token == lax.broadcasted_iota(I32, (n, LANES), 1), 1.0, 0.0).astype(BF16)
    count_ref[0] = jnp.dot(jnp.where(sel, 1.0, 0.0).astype(BF16), member, preferred_element_type=F32).astype(I32)


def _route(logits, cap):
    B, E, n = logits.shape
    assert n // MOE_CHUNK <= LANES
    spec = pl.BlockSpec((1, E, n), lambda b: (b, 0, 0))
    rank, score, count = pl.pallas_call(
        functools.partial(_route_kernel, cap=cap),
        grid=(B,),
        in_specs=[spec],
        out_specs=[spec, spec, pl.BlockSpec((1, E, LANES), lambda b: (b, 0, 0))],
        out_shape=[jax.ShapeDtypeStruct((B, E, n), I32), jax.ShapeDtypeStruct((B, E, n), F32),
                   jax.ShapeDtypeStruct((B, E, LANES), I32)],
        compiler_params=_cparams(("parallel",)),
        name="route",
    )(logits)
    return rank, score, count[..., :n // MOE_CHUNK]


SLOT_ALIGN = 16


def _window_start(first, j, W, cap):
    return pl.multiple_of(jnp.minimum((first // SLOT_ALIGN) * SLOT_ALIGN + j * W, cap - W), SLOT_ALIGN)


def _windows_needed(lo, hi, W):
    return (hi - (lo // SLOT_ALIGN) * SLOT_ALIGN + W - 1) // W


def _gather_kernel(cnt_ref, rank_ref, h_ref, o_ref, *, nc, W, unroll, flag_off):
    b = pl.program_id(0)
    e = pl.program_id(1)
    be = b * pl.num_programs(1) + e
    base = be * (nc + 1)
    cap = o_ref.shape[2]
    o_ref[...] = jnp.zeros(o_ref.shape, o_ref.dtype)
    T = MOE_CHUNK

    def window(c, j):
        lo = cnt_ref[base + c]
        r = rank_ref[0, 0, pl.ds(c, 1), :]
        hc = h_ref[0, pl.ds(pl.multiple_of(c * T, T), T), :]
        st = _window_start(lo, j, W, cap)
        slot = lax.broadcasted_iota(I32, (W, T), 0) + st
        hit = (slot == r) & (slot >= (lo // SLOT_ALIGN) * SLOT_ALIGN + j * W)
        got = jnp.dot(jnp.where(hit, 1.0, 0.0).astype(BF16), hc, preferred_element_type=F32)
        o_ref[0, 0, pl.ds(st, W), :] = o_ref[0, 0, pl.ds(st, W), :] + got.astype(o_ref.dtype)

    def group(g, carry):
        for u in range(unroll):
            window(g * unroll + u, 0)
        return carry

    lax.fori_loop(0, nc // unroll, group, 0)

    def tail(c, carry):
        def more(j, carry2):
            window(c, j)
            return carry2

        lax.fori_loop(1, _windows_needed(cnt_ref[base + c], cnt_ref[base + c + 1], W), more, 0)
        return carry

    @pl.when(cnt_ref[flag_off + be] > 0)
    def _():
        lax.fori_loop(0, nc, tail, 0)


def _moe_window(cap):
    return min(128, cap)


def _moe_tables(per_chunk, cap, tt, eg):
    B, E, nc = per_chunk.shape
    n = nc * MOE_CHUNK
    W = _moe_window(cap)
    cnt = jnp.concatenate([jnp.zeros((B, E, 1), I32), jnp.cumsum(per_chunk, axis=-1, dtype=I32)], axis=-1)
    extra = _windows_needed(cnt[..., :-1], cnt[..., 1:], W) > 1
    g_flag = jnp.any(extra, axis=-1)
    nsub = min(tt, n) // MOE_CHUNK
    s_flag = jnp.any(extra.reshape(B, E // eg, eg, nc // nsub, nsub), axis=(2, 4))
    tbl = jnp.concatenate([cnt.reshape(-1), g_flag.reshape(-1).astype(I32), s_flag.reshape(-1).astype(I32)])
    return tbl, cnt.size, cnt.size + g_flag.size


def _gather(tbl, flag_off, rank, h, cap):
    B, E, n = rank.shape
    D = h.shape[-1]
    nc = n // MOE_CHUNK
    return pl.pallas_call(
        functools.partial(_gather_kernel, nc=nc, W=_moe_window(cap), unroll=min(16, nc), flag_off=flag_off),
        grid_spec=pltpu.PrefetchScalarGridSpec(
            num_scalar_prefetch=1,
            grid=(B, E),
            in_specs=[
                pl.BlockSpec((1, 1, nc, MOE_CHUNK), lambda b, e, tbl: (b, e, 0, 0)),
                pl.BlockSpec((1, n, D), lambda b, e, tbl: (b, 0, 0)),
            ],
            out_specs=pl.BlockSpec((1, 1, cap, D), lambda b, e, tbl: (b, e, 0, 0)),
        ),
        out_shape=jax.ShapeDtypeStruct((B, E, cap, D), BF16),
        compiler_params=_cparams(("parallel", "arbitrary")),
        name="moe_gather",
    )(tbl, rank.reshape(B, E, nc, MOE_CHUNK), h)


def _ffn_kernel(*refs, nsets):
    x_refs = refs[:nsets]
    wg_ref, wu_ref, wd_ref = refs[nsets:nsets + 3]
    o_refs = refs[nsets + 3:2 * nsets + 3]
    acc_scrs = refs[2 * nsets + 3:]
    f = pl.program_id(1)

    @pl.when(f == 0)
    def _():
        for acc_scr in acc_scrs:
            acc_scr[...] = jnp.zeros(acc_scr.shape, F32)

    wg = wg_ref[0, 0].astype(BF16)
    wu = wu_ref[0, 0].astype(BF16)
    wd = wd_ref[0, 0].astype(BF16)
    for x_ref, acc_scr in zip(x_refs, acc_scrs):
        for b in range(x_ref.shape[0]):
            x = x_ref[b, 0]
            g = jnp.dot(x, wg, preferred_element_type=F32)
            u = jnp.dot(x, wu, preferred_element_type=F32)
            hid = (g * (0.5 + 0.5 * jnp.tanh(0.5 * g)) * u).astype(BF16)
            acc_scr[b] = acc_scr[b] + jnp.dot(hid, wd, preferred_element_type=F32)

    @pl.when(f == pl.num_programs(1) - 1)
    def _():
        for o_ref, acc_scr in zip(o_refs, acc_scrs):
            o_ref[:, 0] = acc_scr[...].astype(o_ref.dtype)


def _ffn(xins, w_gate, w_up, w_down, layer, tf):
    E, D = xins[0].shape[1], xins[0].shape[3]
    Fh = w_gate.shape[-1]
    x_specs = [pl.BlockSpec((x.shape[0], 1, x.shape[2], D), lambda e, f: (0, e, 0, 0)) for x in xins]
    return pl.pallas_call(
        functools.partial(_ffn_kernel, nsets=len(xins)),
        grid=(E, Fh // tf),
        in_specs=x_specs + [
            pl.BlockSpec((1, 1, D, tf), lambda e, f: (layer, e, 0, f)),
            pl.BlockSpec((1, 1, D, tf), lambda e, f: (layer, e, 0, f)),
            pl.BlockSpec((1, 1, tf, D), lambda e, f: (layer, e, f, 0)),
        ],
        out_specs=x_specs,
        out_shape=[jax.ShapeDtypeStruct(x.shape, BF16) for x in xins],
        scratch_shapes=[pltpu.VMEM((x.shape[0], x.shape[2], D), F32) for x in xins],
        compiler_params=_cparams(("parallel", "arbitrary")),
        name="moe_ffn",
    )(*xins, w_gate, w_up, w_down)


SCATTER_EG = 4
SCATTER_TT = 2048


def _scatter_kernel(cnt_ref, rank_ref, score_ref, y_ref, x_ref, g2_ref, fg_ref, o_ref, acc_scr, *,
                    nc, W, final, flag_off):
    b = pl.program_id(0)
    i = pl.program_id(1)
    g = pl.program_id(2)
    G = pl.num_programs(2)
    eg = y_ref.shape[1]
    T = MOE_CHUNK
    nsub = acc_scr.shape[0] // T
    cap = y_ref.shape[2]

    @pl.when(g == 0)
    def _():
        acc_scr[...] = jnp.zeros(acc_scr.shape, F32)

    def window(c, ee, j):
        rows = slice(c * T, (c + 1) * T)
        lo = cnt_ref[((b * G + g) * eg + ee) * (nc + 1) + i * nsub + c]
        rc = rank_ref[0, 0, rows, ee:ee + 1]
        sc = score_ref[0, 0, rows, ee:ee + 1]
        st = _window_start(lo, j, W, cap)
        slot = lax.broadcasted_iota(I32, (T, W), 1) + st
        hit = (slot == rc) & (slot >= (lo // SLOT_ALIGN) * SLOT_ALIGN + j * W)
        got = jnp.dot(jnp.where(hit, 1.0, 0.0).astype(BF16), y_ref[0, ee, pl.ds(st, W), :],
                      preferred_element_type=F32)
        return sc * got

    for c in range(nsub):
        rows = slice(c * T, (c + 1) * T)
        acc_scr[rows, :] = acc_scr[rows, :] + functools.reduce(jnp.add, [window(c, ee, 0) for ee in range(eg)])

    @pl.when(cnt_ref[flag_off + (b * G + g) * pl.num_programs(1) + i] > 0)
    def _():
        for c in range(nsub):
            for ee in range(eg):
                def more(j, carry, c=c, ee=ee):
                    rows = slice(c * T, (c + 1) * T)
                    acc_scr[rows, :] = acc_scr[rows, :] + window(c, ee, j)
                    return carry

                base = ((b * G + g) * eg + ee) * (nc + 1) + i * nsub + c
                lax.fori_loop(1, _windows_needed(cnt_ref[base], cnt_ref[base + 1], W), more, 0)

    @pl.when(g == G - 1)
    def _():
        xn = x_ref[0] + g2_ref[0] * acc_scr[...]
        if final:
            ms = jnp.mean(xn * xn, axis=-1, keepdims=True)
            xn = xn * lax.rsqrt(ms + EPS) * fg_ref[...]
        o_ref[0] = xn


def _scatter(tbl, flag_off, rank, score, y, x, g2, final_g, final):
    B, E, n = rank.shape
    cap, D = y.shape[2], y.shape[3]
    tt = min(SCATTER_TT, n)
    eg = SCATTER_EG
    nc = n // MOE_CHUNK

    def token_major(t):
        return t.reshape(B, E // eg, eg, n).transpose(0, 1, 3, 2)

    return pl.pallas_call(
        functools.partial(_scatter_kernel, nc=nc, W=_moe_window(cap), final=final, flag_off=flag_off),
        grid_spec=pltpu.PrefetchScalarGridSpec(
            num_scalar_prefetch=1,
            grid=(B, n // tt, E // eg),
            in_specs=[
                pl.BlockSpec((1, 1, tt, eg), lambda b, i, g, tbl: (b, g, i, 0)),
                pl.BlockSpec((1, 1, tt, eg), lambda b, i, g, tbl: (b, g, i, 0)),
                pl.BlockSpec((1, eg, cap, D), lambda b, i, g, tbl: (b, g, 0, 0)),
                pl.BlockSpec((1, tt, D), lambda b, i, g, tbl: (b, i, 0), pipeline_mode=pl.Buffered(1)),
                pl.BlockSpec((1, 1, D), lambda b, i, g, tbl: (b, 0, 0)),
                pl.BlockSpec((1, D), lambda b, i, g, tbl: (0, 0)),
            ],
            out_specs=pl.BlockSpec((1, tt, D), lambda b, i, g, tbl: (b, i, 0)),
            scratch_shapes=[pltpu.VMEM((tt, D), F32)],
        ),
        out_shape=jax.ShapeDtypeStruct((B, n, D), F32),
        compiler_params=_cparams(("parallel", "parallel", "arbitrary")),
        name="moe_scatter",
    )(tbl, token_major(rank), token_major(score), y, x, g2, final_g)


def _moe_dispatch(h2, logits):
    n = h2.shape[1]
    cap = CAPACITY_FACTOR * n // N_EXPERTS
    rank, score, per_chunk = _route(logits, cap)
    tbl, g_off, s_off = _moe_tables(per_chunk, cap, SCATTER_TT, SCATTER_EG)
    return (tbl, s_off, rank, score), _gather(tbl, g_off, rank, h2, cap)


def _moe_combine(routing, y, x, g2, final_g, final):
    tbl, s_off, rank, score = routing
    return _scatter(tbl, s_off, rank, score, y, x, g2, final_g, final)


LOG2E = math.log2(math.e)
HEADS_PER_BLK = COL_BLK // HEAD_DIM


def _prep_kernel(*refs, rope, qa_scale):
    qa_r, qb_r, qd_r, ka_r, va_r, kb_r, vb_r, kvd_r = refs[:8]
    refs = refs[8:]
    if rope:
        cb_r, sb_r, cd_r, sd_r = refs[:4]
        refs = refs[4:]
    qg_r, kg_r, gm_r = refs[:3]
    qa_o, qb_o, qd_o, ka_o, va_o, kb_o, vb_o, kd_o, vd_o = refs[3:]
    ts = qa_r.shape[1]
    kvw = GQA_KV_HEADS * HEAD_DIM

    def partner(x, half):
        n = x.shape[1]
        lane = lax.broadcasted_iota(I32, x.shape, 1)
        return jnp.where((lane & half) == 0, pltpu.roll(x, n - half, 1), pltpu.roll(x, half, 1))

    def rot(x, cos, sin, half):
        return x * cos + partner(x, half) * sin

    def group_norm(x, gm, gain):
        ms = jnp.dot(x * x, gm, precision=lax.Precision.HIGHEST, preferred_element_type=F32)
        return x * lax.rsqrt(ms + EPS) * gain

    def head(x, h):
        return x[:, h * HEAD_DIM:(h + 1) * HEAD_DIM]

    lane64 = lax.broadcasted_iota(I32, (ts, HEAD_DIM), 1)
    ones_tail = jnp.where(lane64 == 0, 1.0, 0.0).astype(BF16)

    qa = qa_r[0].astype(F32) * qa_scale
    for h in range(HEADS_PER_BLK):
        qa_o[0, h] = head(qa, h).astype(BF16)
        ka_o[0, h] = head(ka_r[0], h)
        va_o[0, h] = head(va_r[0], h)

    qb = qb_r[0].astype(F32)
    kb = kb_r[0].astype(F32)
    if rope:
        qb = rot(qb, cb_r[...], sb_r[...], DIFF_QK_DIM // 2)
        kb = rot(kb, cb_r[...], sb_r[...], DIFF_QK_DIM // 2)
    qb = qb * (DIFF_QK_DIM ** -0.5 * LOG2E)
    for h in range(HEADS_PER_BLK):
        xh = head(qb, h)
        qb_o[0, h // 2, h % 2, 0] = jnp.where(lane64 < DIFF_QK_DIM, xh, 0.0).astype(BF16)
        qb_o[0, h // 2, h % 2, 1] = jnp.where(lane64 >= DIFF_QK_DIM, xh, 0.0).astype(BF16)
        kb_o[0, h // 2, h % 2] = head(kb, h).astype(BF16)
        vb_o[0, h // 2, h % 2] = jnp.concatenate([head(vb_r[0], h), ones_tail], axis=1)

    qd = group_norm(qd_r[0].astype(F32), gm_r[...], qg_r[...])
    kd = group_norm(kvd_r[0, :, :kvw].astype(F32), gm_r[:kvw, :kvw], kg_r[...])
    if rope:
        qd = rot(qd, cd_r[...], sd_r[...], HEAD_DIM // 2)
        kd = rot(kd, cd_r[:, :kvw], sd_r[:, :kvw], HEAD_DIM // 2)
    qd = qd * (HEAD_DIM ** -0.5 * LOG2E)
    grp = GQA_Q_HEADS // GQA_KV_HEADS
    for h in range(GQA_Q_HEADS):
        qd_o[0, h // grp, 0, h % grp] = head(qd, h).astype(BF16)
    vd = kvd_r[0, :, kvw:]
    for h in range(GQA_KV_HEADS):
        kd_o[0, h, 0] = head(kd, h).astype(BF16)
        vd_o[0, h, 0] = jnp.concatenate([head(vd, h), ones_tail], axis=1)


def _prep(p, tables, qg, kg, gm, qa_scale, ts):
    B, S, _ = p.shape
    ts = min(ts, S)

    def seg(blk):
        return pl.BlockSpec((1, ts, COL_BLK), lambda b, i: (b, i, blk))

    in_specs = [seg(s) for s in (SEG_QA, SEG_QB, SEG_QD, SEG_KA, SEG_VA, SEG_KB, SEG_VB, SEG_VB + 1)]
    args = [p] * 8
    if tables is not None:
        in_specs += [pl.BlockSpec((ts, COL_BLK), lambda b, i: (i, 0))] * 4
        args += list(tables)
    in_specs += [pl.BlockSpec(a.shape, lambda b, i: (0, 0)) for a in (qg, kg, gm)]
    args += [qg, kg, gm]

    def out(lead, width):
        shape = (B,) + lead + (S, width)
        nl = len(lead)
        spec = pl.BlockSpec((1,) + lead + (ts, width), lambda b, i: (b,) + (0,) * nl + (i, 0))
        return jax.ShapeDtypeStruct(shape, BF16), spec

    grp = GQA_Q_HEADS // GQA_KV_HEADS
    outs = [out((NA_HEADS,), HEAD_DIM), out((DIFF_HEADS // 2, 2, 2), HEAD_DIM),
            out((GQA_KV_HEADS, 1, grp), HEAD_DIM), out((NA_HEADS,), HEAD_DIM), out((NA_HEADS,), HEAD_DIM),
            out((DIFF_HEADS // 2, 2), HEAD_DIM), out((DIFF_HEADS // 2, 2), LANES),
            out((GQA_KV_HEADS, 1), HEAD_DIM), out((GQA_KV_HEADS, 1), LANES)]
    return pl.pallas_call(
        functools.partial(_prep_kernel, rope=tables is not None, qa_scale=qa_scale),
        grid=(B, S // ts),
        in_specs=in_specs,
        out_specs=[o[1] for o in outs],
        out_shape=[o[0] for o in outs],
        compiler_params=_cparams(("parallel", "parallel")),
        name="attn_prep",
    )(*args)


def _rope_lane_tables(n, dim):
    t = jnp.arange(n)
    row = (t // GRID_W).astype(F32)
    col = (t % GRID_W).astype(F32)
    n_pairs = dim // 4
    inv = ROPE_THETA ** (-jnp.arange(n_pairs, dtype=F32) / n_pairs)
    ang = jnp.concatenate([row[:, None] * inv, col[:, None] * inv], axis=-1)
    cos, sin = jnp.cos(ang), jnp.sin(ang)
    reps = COL_BLK // dim
    return jnp.tile(jnp.concatenate([cos, cos], -1), (1, reps)), jnp.tile(jnp.concatenate([-sin, sin], -1), (1, reps))


def _deinterleave(w, width):
    lead = w.shape[:-1]
    n = w.shape[-1]
    return jnp.swapaxes(w.reshape(lead + (n // width, width // 2, 2)), -1, -2).reshape(lead + (n,))


def _with_ones(v):
    pad = [(0, 0)] * (v.ndim - 1) + [(0, LANES - v.shape[-1] - 1)]
    return jnp.pad(jnp.concatenate([v, jnp.ones(v.shape[:-1] + (1,), v.dtype)], axis=-1), pad)


def _mixer_branches(p, ops, ops_ctx, bias_a, lam, lam_init, sub_g, sgu_g, sgu_w, sgu_bf):
    qa, qb, qd, ka, va, kb, vb, kd, vd = ops
    b, _, s, _ = qa.shape
    if ops_ctx is None:
        y_a = _flash(qa.reshape(b, NA_HEADS // 2, 2, 1, s, HEAD_DIM), [ka.reshape(b, NA_HEADS // 2, 2, s, HEAD_DIM)],
                     [_with_ones(va).reshape(b, NA_HEADS // 2, 2, s, LANES)], tq=256, tk=256)
        kbs, vbs, kds, vds = [kb], [vb], [kd], [vd]
    else:
        _, _, _, kac, vac, kbc, vbc, kdc, vdc = ops_ctx
        y_a = _nbr(qa, ka, va, kac, vac, bias_a)
        kbs, vbs, kds, vds = [kbc, kb], [vbc, vb], [kdc, kd], [vdc, vd]
    lam_v = jnp.full((1, DIFF_V_DIM), lam, F32)
    gain_v = (sub_g.astype(F32) * (1.0 - lam_init))[None, :]
    y_b = _flash(qb, kbs, vbs, tq=256, tk=2048, diff=(lam_v, gain_v))
    y_d = _flash(qd, kds, vds, tq=512, tk=2048)
    y_c = _sgu(p, sgu_g, sgu_w, sgu_bf, tt=1024)
    return [y_a, y_b, y_c, y_d]


def kernel(x, c, ctx, c_ctx, w_mod, b_mod, norm1_g, norm2_g, w_in, q_gain, k_gain, na_rpb, lambda_q1, lambda_k1, lambda_q2, lambda_k2, diff_sub_g, sgu_norm_g, sgu_w, sgu_b, gate_b, w_branch, w_out, w_router, w_e_gate, w_e_up, w_e_down, final_g):
    B, S, D = x.shape
    depth = w_mod.shape[0]
    rows = S // GRID_W
    tables = _rope_lane_tables(S, DIFF_QK_DIM) + _rope_lane_tables(S, HEAD_DIM)
    hp = lax.Precision.HIGHEST
    fg = final_g.astype(F32)[None, :]
    grp_id = jnp.arange(COL_BLK) // HEAD_DIM
    gm = (grp_id[:, None] == grp_id[None, :]).astype(F32) / HEAD_DIM
    for l in range(depth):
        last = l == depth - 1
        mod = jnp.dot(jax.nn.silu(c), w_mod[l], precision=hp) + b_mod[l]
        sh1, sc1, g1, sh2, sc2, g2 = (t[:, None, :] for t in jnp.split(mod, 6, axis=-1))
        mod_c = jnp.dot(jax.nn.silu(c_ctx), w_mod[l], precision=hp) + b_mod[l]
        csh1, csc1, cg1, csh2, csc2, cg2 = (jnp.broadcast_to(t[None, None, :], (B, 1, D))
                                            for t in jnp.split(mod_c, 6, axis=-1))
        lam_init = 0.8 - 0.6 * math.exp(-0.3 * l)
        lam = (jnp.exp(jnp.sum(lambda_q1[l].astype(F32) * lambda_k1[l].astype(F32)))
               - jnp.exp(jnp.sum(lambda_q2[l].astype(F32) * lambda_k2[l].astype(F32))) + lam_init)

        wl = w_in[l]
        q0, kv0 = 0, wl.shape[-1] - 5 * COL_BLK

        def cols(start, blk, n=1):
            return wl[:, start + blk * COL_BLK:start + (blk + n) * COL_BLK]

        kdw = GQA_KV_HEADS * HEAD_DIM
        w_l = jnp.concatenate([
            wl[:, 5 * COL_BLK:kv0],
            cols(q0, 0), _deinterleave(cols(q0, 1), DIFF_QK_DIM), _deinterleave(cols(q0, 2), HEAD_DIM),
            cols(q0, 3, 2),
            cols(kv0, 0, 2), _deinterleave(cols(kv0, 2), DIFF_QK_DIM), cols(kv0, 3),
            _deinterleave(wl[:, kv0 + 4 * COL_BLK:kv0 + 4 * COL_BLK + kdw], HEAD_DIM),
            wl[:, kv0 + 4 * COL_BLK + kdw:],
        ], axis=1).astype(BF16)
        qg = jnp.tile(_deinterleave(q_gain[l].astype(F32), HEAD_DIM), GQA_Q_HEADS)[None, :]
        kg = jnp.tile(_deinterleave(k_gain[l].astype(F32), HEAD_DIM), GQA_KV_HEADS)[None, :]
        bias_in = jnp.concatenate([gate_b[l].astype(F32), jnp.zeros((w_l.shape[1] - GATE_W,), F32)])[None, :]
        n1 = norm1_g[l].astype(F32)[None, None, :]
        n2 = norm2_g[l].astype(F32)[None, None, :]
        wb = w_branch[l].astype(BF16)
        wo = w_out[l].astype(BF16)
        wr_t = w_router[l].astype(F32).T
        sgu_g = sgu_norm_g[l].astype(F32)[None, :]
        sgu_wb = sgu_w[l].astype(BF16)
        sgu_bf = jnp.repeat(sgu_b[l].astype(F32).T, BRANCH_W // SGU_GROUPS, axis=1)
        bias_a = _nbr_bias(na_rpb[l], rows)

        tn = w_l.shape[1] // 4
        p = _inproj(x, n1 * (1.0 + sc1), sh1, w_l, bias_in, GATE_W, tm=2048, tn=tn)
        ops = _prep(p, tables, qg, kg, gm, HEAD_DIM ** -0.5 * LOG2E, ts=512)
        pc = _inproj(ctx, n1 * (1.0 + csc1), csh1, w_l, bias_in, GATE_W, tm=256, tn=tn)
        ops_c = _prep(pc, None, qg, kg, gm, HEAD_DIM ** -0.5 * LOG2E, ts=256)
        if not last:
            ys_c = _mixer_branches(pc, ops_c, None, None, lam, lam_init, diff_sub_g[l], sgu_g, sgu_wb, sgu_bf)
            ctx, hc2, lg_c = _merge(ys_c, pc, wb, wo, ctx, cg1, n2 * (1.0 + csc2), csh2, wr_t, tm=256)
        ys = _mixer_branches(p, ops, ops_c, bias_a, lam, lam_init, diff_sub_g[l], sgu_g, sgu_wb, sgu_bf)
        x, h2, lg = _merge(ys, p, wb, wo, x, g1, n2 * (1.0 + sc2), sh2, wr_t, tm=512)

        routings, xins = zip(*([_moe_dispatch(h2, lg)] + ([] if last else [_moe_dispatch(hc2, lg_c)])))
        ys_e = _ffn(list(xins), w_e_gate, w_e_up, w_e_down, l, tf=512)
        x = _moe_combine(routings[0], ys_e[0], x, g2, fg, final=last)
        if not last:
            ctx = _moe_combine(routings[1], ys_e[1], ctx, cg2, fg, final=False)
    return x
```
